```python
import math
import jax, jax.numpy as jnp
from jax import lax
import numpy as np

D_MODEL = 1024
BATCH = 4
SEQ = 8192
DEPTH = 1
DEC_BATCH = 32
DEC_SEQ = 64
PAST_LEN = 2048

CHUNK = 64
Q_BLOCK = 128
ATTN_WIDTH = D_MODEL // 2
LRU_WIDTH = D_MODEL - ATTN_WIDTH
HEAD_DIM = 64
N_ATTN_HEADS = ATTN_WIDTH // (2 * HEAD_DIM)
V_DIM = 2 * HEAD_DIM
LRU_BLOCKS = 8
LRU_BLOCK_W = LRU_WIDTH // LRU_BLOCKS
CONV_WIDTH = 4
LRU_C = 8.0
D_FF = 2816
NUM_BUCKETS = 32
MAX_DISTANCE = 128
IN_WIDTH = 3 * ATTN_WIDTH + 2 * LRU_WIDTH
EPS = 1e-6
NEG_INF = -1e30

kernel_name = "hymba_diffattn_rglru_macaron_stream"


def rms_norm(x, g):
    xf = x.astype(jnp.float32)
    y = xf * lax.rsqrt(jnp.mean(xf * xf, axis=-1, keepdims=True) + EPS)
    return (y * g.astype(jnp.float32)).astype(x.dtype)


def swiglu(x, w_gate, w_up, w_down):
    return (jax.nn.silu(x @ w_gate) * (x @ w_up)) @ w_down


def t5_bucket(rel):
    n = NUM_BUCKETS // 2
    max_exact = n // 2
    ret = jnp.where(rel > 0, n, 0)
    rel = jnp.abs(rel)
    relf = jnp.maximum(rel, 1).astype(jnp.float32)
    large = max_exact + (jnp.log(relf / max_exact) / math.log(MAX_DISTANCE / max_exact)
                         * (n - max_exact)).astype(jnp.int32)
    large = jnp.minimum(large, n - 1)
    return ret + jnp.where(rel < max_exact, rel, large)


def diff_attend(q, k, v, q_pos, k_pos, rel_table, lam):
    logits = jnp.einsum('bqhcd,bkhcd->bhcqk', q, k).astype(jnp.float32) * (HEAD_DIM ** -0.5)
    bias = rel_table.astype(jnp.float32)[t5_bucket(k_pos[None, :] - q_pos[:, None])]
    bias = jnp.transpose(bias, (2, 0, 1))[None, :, None]
    visible = (k_pos[None, :] // CHUNK) <= (q_pos[:, None] // CHUNK)
    logits = jnp.where(visible, logits + bias, NEG_INF)
    p = jax.nn.softmax(logits, axis=-1)
    attn = p[:, :, 0] - lam * p[:, :, 1]
    return jnp.einsum('bhqk,bkhe->bqhe', attn.astype(v.dtype), v)


def prompt_attention(q, k, v, rel_table, lam):
    B, S = q.shape[0], q.shape[1]
    nb = S // Q_BLOCK
    q_blocks = jnp.moveaxis(q.reshape(B, nb, Q_BLOCK, N_ATTN_HEADS, 2, HEAD_DIM), 1, 0)
    k_pos = jnp.arange(S, dtype=jnp.int32)

    def one_block(args):
        qb, start = args
        return diff_attend(qb, k, v, start + jnp.arange(Q_BLOCK, dtype=jnp.int32), k_pos, rel_table, lam)

    o = lax.map(one_block, (q_blocks, jnp.arange(nb, dtype=jnp.int32) * Q_BLOCK))
    return jnp.moveaxis(o, 0, 1).reshape(B, S, N_ATTN_HEADS, V_DIM)


def lru_combine(left, right):
    a1, b1 = left
    a2, b2 = right
    return a1 * a2, a2 * b1 + b2


def rglru_group(xr, xg, p, l, h0, conv0):
    B, S = xr.shape[0], xr.shape[1]
    xpad = jnp.concatenate([conv0.astype(xr.dtype), xr], axis=1)
    w = p['conv_w'][l]
    xc = p['conv_b'][l] + sum(xpad[:, j:j + S] * w[j] for j in range(CONV_WIDTH))
    conv_new = xpad[:, -(CONV_WIDTH - 1):]
    xb = xc.reshape(B, S, LRU_BLOCKS, LRU_BLOCK_W)
    r = jax.nn.sigmoid(jnp.einsum('bsnc,ncd->bsnd', xb, p['gate_a_w'][l]).reshape(B, S, LRU_WIDTH) + p['gate_a_b'][l])
    i = jax.nn.sigmoid(jnp.einsum('bsnc,ncd->bsnd', xb, p['gate_x_w'][l]).reshape(B, S, LRU_WIDTH) + p['gate_x_b'][l])
    log_a = -LRU_C * r.astype(jnp.float32) * jax.nn.softplus(-p['lru_L'][l].astype(jnp.float32))
    a = jnp.exp(log_a)
    b = jnp.sqrt(-jnp.expm1(2.0 * log_a)) * (i * xc).astype(jnp.float32)
    b = b.at[:, 0].add(a[:, 0] * h0.astype(jnp.float32))
    _, h = lax.associative_scan(lru_combine, (a, b), axis=1)
    out = h.astype(xr.dtype) * jax.nn.gelu(xg)
    out = rms_norm(out, p['lru_out_norm'][l])
    return out, h[:, -1].astype(h0.dtype), conv_new.astype(conv0.dtype)


def encoder_layer(x, p, l, rel_table, k_past, v_past, h0, conv0):
    B, S = x.shape[0], x.shape[1]
    x = x + 0.5 * swiglu(rms_norm(x, p['norm_ffn1'][l]), p['ffn1_gate'][l], p['ffn1_up'][l], p['ffn1_down'][l])
    hn = rms_norm(x, p['norm_mix'][l])
    proj = hn @ p['w_in'][l]
    q, k, v, xr, xg = jnp.split(proj, [ATTN_WIDTH, 2 * ATTN_WIDTH, 3 * ATTN_WIDTH, 3 * ATTN_WIDTH + LRU_WIDTH], axis=-1)
    q = rms_norm(q.reshape(B, S, N_ATTN_HEADS, 2, HEAD_DIM), p['q_norm'][l])
    k = rms_norm(k.reshape(B, S, N_ATTN_HEADS, 2, HEAD_DIM), p['k_norm'][l])
    v = v.reshape(B, S, N_ATTN_HEADS, V_DIM)
    lam_init = 0.8 - 0.6 * math.exp(-0.3 * l)
    lam = (jnp.exp(jnp.sum(p['lambda_q1'][l].astype(jnp.float32) * p['lambda_k1'][l].astype(jnp.float32)))
           - jnp.exp(jnp.sum(p['lambda_q2'][l].astype(jnp.float32) * p['lambda_k2'][l].astype(jnp.float32)))
           + lam_init)
    if k_past is None:
        o = prompt_attention(q, k, v, rel_table, lam)
    else:
        P = k_past.shape[1]
        k_all = jnp.concatenate([k_past.astype(k.dtype), k], axis=1)
        v_all = jnp.concatenate([v_past.astype(v.dtype), v], axis=1)
        q_pos = P + jnp.arange(S, dtype=jnp.int32)
        k_pos = jnp.arange(P + S, dtype=jnp.int32)
        o = diff_attend(q, k_all, v_all, q_pos, k_pos, rel_table, lam)
    o = (rms_norm(o, p['subln'][l]) * (1.0 - lam_init)).reshape(B, S, ATTN_WIDTH)
    r_out, h_last, conv_new = rglru_group(xr, xg, p, l, h0, conv0)
    x = x + jnp.concatenate([o, r_out], axis=-1) @ p['w_out'][l]
    x = x + 0.5 * swiglu(rms_norm(x, p['norm_ffn2'][l]), p['ffn2_gate'][l], p['ffn2_up'][l], p['ffn2_down'][l])
    return x, k, v, h_last, conv_new


def setup_inputs(seed: int = 0) -> dict:
    key = jax.random.key(seed)
    ks = jax.random.split(key, 40)

    def nrm(k, shape, scale):
        return jax.random.normal(k, shape, jnp.float32) * scale

    def gain(k, shape):
        return 1.0 + 0.01 * jax.random.normal(k, shape, jnp.float32)

    u = jax.random.uniform(ks[27], (DEPTH, LRU_WIDTH), jnp.float32, minval=0.9, maxval=0.999)
    a0 = u ** (1.0 / LRU_C)
    lru_L = jnp.log(a0) - jnp.log1p(-a0)
    return {
        "x_prompt": nrm(ks[0], (BATCH, SEQ, D_MODEL), 1.0),
        "x_sample": nrm(ks[1], (DEC_BATCH, DEC_SEQ, D_MODEL), 1.0),
        "cache_k": nrm(ks[2], (DEPTH, DEC_BATCH, PAST_LEN, N_ATTN_HEADS, 2, HEAD_DIM), 1.0),
        "cache_v": nrm(ks[3], (DEPTH, DEC_BATCH, PAST_LEN, N_ATTN_HEADS, V_DIM), 1.0),
        "state_lru": nrm(ks[4], (DEPTH, DEC_BATCH, LRU_WIDTH), 0.5),
        "state_conv": nrm(ks[5], (DEPTH, DEC_BATCH, CONV_WIDTH - 1, LRU_WIDTH), 1.0),
        "rel_bias": nrm(ks[6], (NUM_BUCKETS, N_ATTN_HEADS), 0.2),
        "norm_ffn1": gain(ks[7], (DEPTH, D_MODEL)),
        "ffn1_gate": nrm(ks[8], (DEPTH, D_MODEL, D_FF), D_MODEL ** -0.5),
        "ffn1_up": nrm(ks[9], (DEPTH, D_MODEL, D_FF), D_MODEL ** -0.5),
        "ffn1_down": nrm(ks[10], (DEPTH, D_FF, D_MODEL), D_FF ** -0.5),
        "norm_mix": gain(ks[11], (DEPTH, D_MODEL)),
        "w_in": nrm(ks[12], (DEPTH, D_MODEL, IN_WIDTH), D_MODEL ** -0.5),
        "q_norm": gain(ks[13], (DEPTH, HEAD_DIM)),
        "k_norm": gain(ks[14], (DEPTH, HEAD_DIM)),
        "lambda_q1": nrm(ks[15], (DEPTH, HEAD_DIM), 0.1),
        "lambda_k1": nrm(ks[16], (DEPTH, HEAD_DIM), 0.1),
        "lambda_q2": nrm(ks[17], (DEPTH, HEAD_DIM), 0.1),
        "lambda_k2": nrm(ks[18], (DEPTH, HEAD_DIM), 0.1),
        "subln": gain(ks[19], (DEPTH, V_DIM)),
        "conv_w": nrm(ks[20], (DEPTH, CONV_WIDTH, LRU_WIDTH), CONV_WIDTH ** -0.5),
        "conv_b": nrm(ks[21], (DEPTH, LRU_WIDTH), 0.01),
        "gate_a_w": nrm(ks[22], (DEPTH, LRU_BLOCKS, LRU_BLOCK_W, LRU_BLOCK_W), LRU_BLOCK_W ** -0.5),
        "gate_a_b": nrm(ks[23], (DEPTH, LRU_WIDTH), 0.01),
        "gate_x_w": nrm(ks[24], (DEPTH, LRU_BLOCKS, LRU_BLOCK_W, LRU_BLOCK_W), LRU_BLOCK_W ** -0.5),
        "gate_x_b": nrm(ks[25], (DEPTH, LRU_WIDTH), 0.01),
        "lru_L": lru_L,
        "lru_out_norm": gain(ks[26], (DEPTH, LRU_WIDTH)),
        "w_out": nrm(ks[28], (DEPTH, D_MODEL, D_MODEL), D_MODEL ** -0.5),
        "norm_ffn2": gain(ks[29], (DEPTH, D_MODEL)),
        "ffn2_gate": nrm(ks[30], (DEPTH, D_MODEL, D_FF), D_MODEL ** -0.5),
        "ffn2_up": nrm(ks[31], (DEPTH, D_MODEL, D_FF), D_MODEL ** -0.5),
        "ffn2_down": nrm(ks[32], (DEPTH, D_FF, D_MODEL), D_FF ** -0.5),
    }


def reference(x_prompt, x_sample, cache_k, cache_v, state_lru, state_conv, rel_bias,
              norm_ffn1, ffn1_gate, ffn1_up, ffn1_down, norm_mix, w_in, q_norm, k_norm,
              lambda_q1, lambda_k1, lambda_q2, lambda_k2, subln, conv_w, conv_b,
              gate_a_w, gate_a_b, gate_x_w, gate_x_b, lru_L, lru_out_norm, w_out,
              norm_ffn2, ffn2_gate, ffn2_up, ffn2_down):
    p = dict(norm_ffn1=norm_ffn1, ffn1_gate=ffn1_gate, ffn1_up=ffn1_up, ffn1_down=ffn1_down,
             norm_mix=norm_mix, w_in=w_in, q_norm=q_norm, k_norm=k_norm,
             lambda_q1=lambda_q1, lambda_k1=lambda_k1, lambda_q2=lambda_q2, lambda_k2=lambda_k2,
             subln=subln, conv_w=conv_w, conv_b=conv_b, gate_a_w=gate_a_w, gate_a_b=gate_a_b,
             gate_x_w=gate_x_w, gate_x_b=gate_x_b, lru_L=lru_L, lru_out_norm=lru_out_norm,
             w_out=w_out, norm_ffn2=norm_ffn2, ffn2_gate=ffn2_gate, ffn2_up=ffn2_up, ffn2_down=ffn2_down)
    B = x_prompt.shape[0]
    yp, ys = x_prompt, x_sample
    kp_l, vp_l, hp_l, cp_l, ks_l, vs_l, hs_l, cs_l = [], [], [], [], [], [], [], []
    for l in range(DEPTH):
        h0_p = jnp.zeros((B, LRU_WIDTH), state_lru.dtype)
        c0_p = jnp.zeros((B, CONV_WIDTH - 1, LRU_WIDTH), state_conv.dtype)
        yp, kp, vp, hp, cp = encoder_layer(yp, p, l, rel_bias, None, None, h0_p, c0_p)
        ys, kn, vn, hn, cn = encoder_layer(ys, p, l, rel_bias, cache_k[l], cache_v[l], state_lru[l], state_conv[l])
        kp_l.append(kp); vp_l.append(vp); hp_l.append(hp); cp_l.append(cp)
        ks_l.append(kn); vs_l.append(vn); hs_l.append(hn); cs_l.append(cn)
    new_k_prompt = jnp.stack(kp_l, axis=0)
    new_v_prompt = jnp.stack(vp_l, axis=0)
    new_lru_prompt = jnp.stack(hp_l, axis=0)
    new_conv_prompt = jnp.stack(cp_l, axis=0)
    new_k_sample = jnp.stack(ks_l, axis=0)
    new_v_sample = jnp.stack(vs_l, axis=0)
    new_lru_sample = jnp.stack(hs_l, axis=0)
    new_conv_sample = jnp.stack(cs_l, axis=0)
    return (yp, ys, new_k_prompt, new_v_prompt, new_lru_prompt, new_conv_prompt,
            new_k_sample, new_v_sample, new_lru_sample, new_conv_sample)
```

```python
import functools
import math

import numpy as np
import jax
import jax.numpy as jnp
from jax import lax
from jax.experimental import pallas as pl
from jax.experimental.pallas import tpu as pltpu

F32 = jnp.float32
BF16 = jnp.bfloat16

HEAD_DIM = 64
V_DIM = 2 * HEAD_DIM
CHUNK = 64
NUM_BUCKETS = 32
MAX_DISTANCE = 128
CONV_WIDTH = 4
LRU_BLOCKS = 8
LRU_C = 8.0
EPS = 1e-6
NEG_INF = -1e30
LAM_INIT = 0.8 - 0.6 * math.exp(-0.3 * 0)

LANES = 128
SUBLANES = 8
MXU_DIM = 256
VMEM_BYTES = 64 * 1024 * 1024

FF_CHUNK = MXU_DIM
ATTN_TILE = 256
TOKEN_TILE = 256
LRU_TILE = 256


def _vmem_limit(nbytes):
    return int(min(VMEM_BYTES - 4 * 1024 * 1024, max(nbytes, 16 * 1024 * 1024)))


def _rms(x, g):
    ms = jnp.mean(x * x, axis=-1, keepdims=True)
    return x * lax.rsqrt(ms + EPS) * g


def _sigmoid(x):
    return 1.0 / (1.0 + jnp.exp(-x))


def _const_spec(shape):
    n = len(shape)
    return pl.BlockSpec(shape, lambda *_: (0,) * n, pipeline_mode=pl.Buffered(1))


def _swiglu_into(acc_ref, xn, wg_ref, wu_ref, wd_ref):
    acc_ref[...] = jnp.zeros_like(acc_ref)

    def body(c, carry):
        g = jnp.dot(xn, wg_ref[c], preferred_element_type=F32)
        u = jnp.dot(xn, wu_ref[c], preferred_element_type=F32)
        h = (g * _sigmoid(g) * u).astype(BF16)
        acc_ref[...] += jnp.dot(h, wd_ref[c], preferred_element_type=F32)
        return carry

    lax.fori_loop(0, wg_ref.shape[0], body, 0)


def _stage_a_kernel(x_ref, g1_ref, wg_ref, wu_ref, wd_ref, gm_ref, win_ref, gmat_ref, qg_ref, kg_ref,
                    x1_ref, k_ref, v_ref, xr_ref, xg_ref, qb_ref, kb_ref, vb_ref, acc_ref,
                    *, attn_width, lru_width, transposed, tile):
    x = x_ref[...]
    xn = _rms(x, g1_ref[...]).astype(BF16)
    _swiglu_into(acc_ref, xn, wg_ref, wu_ref, wd_ref)
    x1 = x + 0.5 * acc_ref[...]
    x1_ref[...] = x1
    hn = _rms(x1, gm_ref[...]).astype(BF16)
    proj = jnp.dot(hn, win_ref[...], preferred_element_type=F32)
    a = attn_width
    q = proj[:, :a]
    k = proj[:, a:2 * a]
    v = proj[:, 2 * a:3 * a]
    xr_ref[...] = proj[:, 3 * a:3 * a + lru_width]
    xg_ref[...] = proj[:, 3 * a + lru_width:]

    def group_norm(t, g):
        ms = jnp.dot((t * t).astype(BF16), gmat_ref[...], preferred_element_type=F32)
        return t * lax.rsqrt(ms + EPS) * g

    qn = group_norm(q, qg_ref[...]) * (HEAD_DIM ** -0.5)
    kn = group_norm(k, kg_ref[...])
    k_ref[...] = kn
    v_ref[...] = v
    kb_ref[...] = kn.astype(BF16)
    if transposed:
        for j in range(x.shape[0] // tile):
            rows = slice(j * tile, (j + 1) * tile)
            qb_ref[0, j] = qn[rows, :].T.astype(BF16)
            vb_ref[0, j] = v[rows, :].T.astype(BF16)
    else:
        qb_ref[...] = qn.astype(BF16)
        vb_ref[...] = v.astype(BF16)


def _stage_a(x, w, *, transposed):
    B, S, D = x.shape
    a = w["attn_width"]
    lw = w["lru_width"]
    tm = TOKEN_TILE
    T = ATTN_TILE
    assert (B * S) % tm == 0 and (not transposed or (tm % T == 0 and S % tm == 0))
    nt = S // tm
    x2 = x.reshape(B * S, D)
    n_ff = w["ffn1_gate"].shape[0]
    tok = lambda width: pl.BlockSpec((tm, width), lambda i: (i, 0))
    out_shape = [
        jax.ShapeDtypeStruct((B * S, D), F32),
        jax.ShapeDtypeStruct((B * S, a), F32),
        jax.ShapeDtypeStruct((B * S, a), F32),
        jax.ShapeDtypeStruct((B * S, lw), F32),
        jax.ShapeDtypeStruct((B * S, lw), F32),
    ]
    out_specs = [tok(D), tok(a), tok(a), tok(lw), tok(lw)]
    if transposed:
        tshape = jax.ShapeDtypeStruct((B * nt, tm // T, a, T), BF16)
        tspec = pl.BlockSpec((1, tm // T, a, T), lambda i: (i, 0, 0, 0))
        out_shape += [tshape, jax.ShapeDtypeStruct((B * S, a), BF16), tshape]
        out_specs += [tspec, tok(a), tspec]
    else:
        bshape = jax.ShapeDtypeStruct((B * S, a), BF16)
        out_shape += [bshape, bshape, bshape]
        out_specs += [tok(a), tok(a), tok(a)]
    weight_bytes = 2 * (3 * D * n_ff * FF_CHUNK + D * (3 * a + 2 * lw) + a * a)
    act_bytes = 4 * tm * (2 * 2 * D + 2 * 2 * (2 * a + 2 * lw) + D + 2 * (3 * a + 2 * lw) + 4 * FF_CHUNK)
    kern = functools.partial(_stage_a_kernel, attn_width=a, lru_width=lw, transposed=transposed, tile=T)
    outs = pl.pallas_call(
        kern,
        grid=(B * S // tm,),
        in_specs=[
            tok(D),
            _const_spec((1, D)),
            _const_spec(w["ffn1_gate"].shape),
            _const_spec(w["ffn1_up"].shape),
            _const_spec(w["ffn1_down"].shape),
            _const_spec((1, D)),
            _const_spec(w["w_in"].shape),
            _const_spec((a, a)),
            _const_spec((1, a)),
            _const_spec((1, a)),
        ],
        out_specs=out_specs,
        out_shape=out_shape,
        scratch_shapes=[pltpu.VMEM((tm, D), F32)],
        compiler_params=pltpu.CompilerParams(
            dimension_semantics=("parallel",),
            vmem_limit_bytes=_vmem_limit(weight_bytes + act_bytes)),
        name="stage_a_t" if transposed else "stage_a_n",
    )(x2, w["norm_ffn1"], w["ffn1_gate"], w["ffn1_up"], w["ffn1_down"], w["norm_mix"], w["w_in"],
      w["group_mat"], w["q_gain"], w["k_gain"])
    return outs


def _stage_d_kernel(x1_ref, o_ref, r_ref, woa_ref, wor_ref, g2_ref, wg_ref, wu_ref, wd_ref,
                    y_ref, acc_ref):
    x2 = (x1_ref[...]
          + jnp.dot(o_ref[...], woa_ref[...], preferred_element_type=F32)
          + jnp.dot(r_ref[...], wor_ref[...], preferred_element_type=F32))
    xn = _rms(x2, g2_ref[...]).astype(BF16)
    _swiglu_into(acc_ref, xn, wg_ref, wu_ref, wd_ref)
    y_ref[...] = x2 + 0.5 * acc_ref[...]


def _stage_d(x1, o, r, w):
    N, D = x1.shape
    a = o.shape[1]
    lw = r.shape[1]
    tm = min(TOKEN_TILE, N)
    assert N % tm == 0
    n_ff = w["ffn2_gate"].shape[0]
    tok = lambda width: pl.BlockSpec((tm, width), lambda i: (i, 0))
    weight_bytes = 2 * (3 * D * n_ff * FF_CHUNK + D * D)
    act_bytes = 4 * tm * (2 * 2 * D + 2 * (a + lw) + 2 * D + 4 * FF_CHUNK)
    return pl.pallas_call(
        _stage_d_kernel,
        grid=(N // tm,),
        in_specs=[
            tok(D), tok(a), tok(lw),
            _const_spec((a, D)), _const_spec((lw, D)), _const_spec((1, D)),
            _const_spec(w["ffn2_gate"].shape), _const_spec(w["ffn2_up"].shape),
            _const_spec(w["ffn2_down"].shape),
        ],
        out_specs=tok(D),
        out_shape=jax.ShapeDtypeStruct((N, D), F32),
        scratch_shapes=[pltpu.VMEM((tm, D), F32)],
        compiler_params=pltpu.CompilerParams(
            dimension_semantics=("parallel",),
            vmem_limit_bytes=_vmem_limit(weight_bytes + act_bytes)),
        name="stage_d",
    )(x1, o, r, w["w_out_attn"], w["w_out_lru"], w["norm_ffn2"],
      w["ffn2_gate"], w["ffn2_up"], w["ffn2_down"])


def _t5_bucket(rel):
    n = NUM_BUCKETS // 2
    max_exact = n // 2
    ret = jnp.where(rel > 0, n, 0)
    rel = jnp.abs(rel)
    relf = jnp.maximum(rel, 1).astype(jnp.float32)
    large = max_exact + (jnp.log(relf / max_exact) / math.log(MAX_DISTANCE / max_exact)
                         * (n - max_exact)).astype(jnp.int32)
    large = jnp.minimum(large, n - 1)
    return ret + jnp.where(rel < max_exact, rel, large)


FAR_BUCKET = NUM_BUCKETS // 2 - 1


def _bucket_tile(q_pos, k_pos, keys_first):
    rel = k_pos[None, :] - q_pos[:, None]
    visible = (k_pos[None, :] // CHUNK) <= (q_pos[:, None] // CHUNK)
    idx = jnp.where(visible, _t5_bucket(rel), -1)
    return idx.T if keys_first else idx


def _bias_kernel(table_ref, idx_ref, out_ref):
    h = pl.program_id(0)
    idx = idx_ref[0]
    acc = jnp.zeros(idx.shape, F32)
    for b in range(NUM_BUCKETS):
        acc = jnp.where(idx == b, table_ref[b, h], acc)
    out_ref[0, 0] = jnp.where(idx < 0, NEG_INF, acc - table_ref[FAR_BUCKET, h])


def _bias_tiles(rel_bias, idx):
    n, R, C = idx.shape
    H = rel_bias.shape[1]
    return pl.pallas_call(
        _bias_kernel,
        grid=(H, n),
        in_specs=[pl.BlockSpec(memory_space=pltpu.SMEM),
                  pl.BlockSpec((1, R, C), lambda h, i: (i, 0, 0))],
        out_specs=pl.BlockSpec((1, 1, R, C), lambda h, i: (h, i, 0, 0)),
        out_shape=jax.ShapeDtypeStruct((H, n, R, C), F32),
        name="bias_tiles",
    )(rel_bias, idx)


def _lambda(lq1_ref, lk1_ref, lq2_ref, lk2_ref):
    s1 = jnp.sum(lq1_ref[...] * lk1_ref[...], axis=-1, keepdims=True)
    s2 = jnp.sum(lq2_ref[...] * lk2_ref[...], axis=-1, keepdims=True)
    return jnp.exp(s1) - jnp.exp(s2) + LAM_INIT


def _prompt_attn_kernel(qT_ref, k_ref, vT_ref, bias_ref, lq1_ref, lk1_ref, lq2_ref, lk2_ref, sub_ref,
                        o_ref, m_ref, l_ref, acc_ref):
    nq, d, T = qT_ref.shape[1:]
    lam = _lambda(lq1_ref, lk1_ref, lq2_ref, lk2_ref)
    first = lax.broadcasted_iota(jnp.int32, (d, T), 0) < HEAD_DIM

    def update(ki, q2, bias):
        kt = k_ref[0, pl.ds(pl.multiple_of(ki * T, T), T), :]
        s = jnp.dot(kt, q2, preferred_element_type=F32)
        if bias is not None:
            s = s + jnp.concatenate([bias, bias], axis=1)
        m_prev = m_ref[...]
        m_new = jnp.maximum(m_prev, jnp.max(s, axis=0, keepdims=True))
        alpha = jnp.exp(m_prev - m_new)
        p = jnp.exp(s - m_new)
        l_ref[...] = alpha * l_ref[...] + jnp.sum(p, axis=0, keepdims=True)
        acc_ref[...] = alpha * acc_ref[...] + jnp.dot(vT_ref[0, ki], p.astype(BF16),
                                                      preferred_element_type=F32)
        m_ref[...] = m_new

    def q_tile(qi, carry):
        qt = qT_ref[0, qi]
        zero = jnp.zeros_like(qt)
        q2 = jnp.concatenate([jnp.where(first, qt, zero), jnp.where(first, zero, qt)], axis=1)
        m_ref[...] = jnp.full_like(m_ref, NEG_INF)
        l_ref[...] = jnp.zeros_like(l_ref)
        acc_ref[...] = jnp.zeros_like(acc_ref)

        def far(ki, c):
            update(ki, q2, None)
            return c

        lax.fori_loop(0, qi - 1, far, 0)

        @pl.when(qi > 0)
        def _():
            update(qi - 1, q2, bias_ref[0, 1])

        update(qi, q2, bias_ref[0, 0])

        inv = 1.0 / l_ref[...]
        on = acc_ref[...] * inv
        o = on[:, :T] - lam * on[:, T:]
        o = o * lax.rsqrt(jnp.mean(o * o, axis=0, keepdims=True) + EPS)
        ot = o.T * sub_ref[...] * (1.0 - LAM_INIT)
        o_ref[0, pl.ds(pl.multiple_of(qi * T, T), T), :] = ot.astype(o_ref.dtype)
        return carry

    lax.fori_loop(0, nq, q_tile, 0)


def _prompt_attention(qT, kb, vT, bias, w, B, S):
    T = ATTN_TILE
    A = kb.shape[-1]
    H = A // V_DIM
    nq = S // T
    lam_spec = pl.BlockSpec((1, HEAD_DIM), lambda b, h: (0, 0))
    tile_spec = pl.BlockSpec((1, nq, V_DIM, T), lambda b, h: (b, 0, h, 0))
    seq_spec = pl.BlockSpec((1, S, V_DIM), lambda b, h: (b, 0, h))
    blk_bytes = 2 * 2 * (3 * S * V_DIM) + 2 * 2 * S * V_DIM + 2 * 4 * 2 * T * T
    tmp_bytes = 4 * (V_DIM * 2 * T + 6 * T * 2 * T)
    return pl.pallas_call(
        _prompt_attn_kernel,
        grid=(B, H),
        in_specs=[tile_spec, seq_spec, tile_spec,
                  pl.BlockSpec((1, 2, T, T), lambda b, h: (h, 0, 0, 0)),
                  lam_spec, lam_spec, lam_spec, lam_spec,
                  pl.BlockSpec((1, V_DIM), lambda b, h: (0, 0))],
        out_specs=seq_spec,
        out_shape=jax.ShapeDtypeStruct((B, S, A), BF16),
        scratch_shapes=[pltpu.VMEM((1, 2 * T), F32), pltpu.VMEM((1, 2 * T), F32),
                        pltpu.VMEM((V_DIM, 2 * T), F32)],
        compiler_params=pltpu.CompilerParams(
            dimension_semantics=("parallel", "parallel"),
            vmem_limit_bytes=_vmem_limit(blk_bytes + tmp_bytes)),
        name="prompt_attn",
    )(qT, kb, vT, bias, w["lambda_q1"], w["lambda_k1"], w["lambda_q2"], w["lambda_k2"], w["subln"])


def _sample_attn_kernel(q_ref, kn_ref, vn_ref, ck_ref, cv_ref, bc_ref, bn_ref,
                        lq1_ref, lk1_ref, lq2_ref, lk2_ref, sub_ref, o_ref, *, near):
    Q = q_ref.shape[1]
    P = ck_ref.shape[1]
    H = q_ref.shape[2] // V_DIM
    lam = _lambda(lq1_ref, lk1_ref, lq2_ref, lk2_ref)
    first = lax.broadcasted_iota(jnp.int32, (Q, V_DIM), 1) < HEAD_DIM
    nt = (((1,), (1,)), ((), ()))
    far = P - near
    for h in range(H):
        cols = slice(h * V_DIM, (h + 1) * V_DIM)
        qh = q_ref[0, :, cols]
        zero = jnp.zeros_like(qh)
        q2 = jnp.concatenate([jnp.where(first, qh, zero), jnp.where(first, zero, qh)], axis=0)
        k_far = ck_ref[0, :far, cols].astype(BF16)
        k_near = ck_ref[0, far:, cols].astype(BF16)
        bc = bc_ref[h]
        bn = bn_ref[h]
        s_far = lax.dot_general(q2, k_far, nt, preferred_element_type=F32)
        s_near = (lax.dot_general(q2, k_near, nt, preferred_element_type=F32)
                  + jnp.concatenate([bc, bc], axis=0))
        s_new = (lax.dot_general(q2, kn_ref[0, :, cols], nt, preferred_element_type=F32)
                 + jnp.concatenate([bn, bn], axis=0))
        m = jnp.maximum(jnp.maximum(jnp.max(s_far, axis=-1, keepdims=True),
                                    jnp.max(s_near, axis=-1, keepdims=True)),
                        jnp.max(s_new, axis=-1, keepdims=True))
        p_far = jnp.exp(s_far - m)
        p_near = jnp.exp(s_near - m)
        p_new = jnp.exp(s_new - m)
        l = (jnp.sum(p_far, axis=-1, keepdims=True) + jnp.sum(p_near, axis=-1, keepdims=True)
             + jnp.sum(p_new, axis=-1, keepdims=True))
        acc = (jnp.dot(p_far.astype(BF16), cv_ref[0, :far, cols].astype(BF16),
                       preferred_element_type=F32)
               + jnp.dot(p_near.astype(BF16), cv_ref[0, far:, cols].astype(BF16),
                         preferred_element_type=F32)
               + jnp.dot(p_new.astype(BF16), vn_ref[0, :, cols], preferred_element_type=F32))
        on = acc * (1.0 / l)
        o = on[:Q] - lam * on[Q:]
        o = o * lax.rsqrt(jnp.mean(o * o, axis=-1, keepdims=True) + EPS)
        o_ref[0, :, cols] = (o * sub_ref[...] * (1.0 - LAM_INIT)).astype(o_ref.dtype)


def _sample_attention(qb, kb, vb, cache_k, cache_v, bias_c, bias_n, w):
    B, Q, A = qb.shape
    P = cache_k.shape[1]
    near = bias_c.shape[-1]
    new_spec = pl.BlockSpec((1, Q, A), lambda b: (b, 0, 0))
    cache_spec = pl.BlockSpec((1, P, A), lambda b: (b, 0, 0))
    lam_spec = pl.BlockSpec((1, HEAD_DIM), lambda b: (0, 0))
    blk_bytes = 2 * 2 * 4 * P * A + 2 * 4 * 2 * Q * A
    tmp_bytes = 4 * 2 * Q * (3 * P + 2 * P) + 2 * 2 * P * V_DIM
    return pl.pallas_call(
        functools.partial(_sample_attn_kernel, near=near),
        grid=(B,),
        in_specs=[new_spec, new_spec, new_spec, cache_spec, cache_spec,
                  pl.BlockSpec(bias_c.shape, lambda b: (0, 0, 0)),
                  pl.BlockSpec(bias_n.shape, lambda b: (0, 0, 0)),
                  lam_spec, lam_spec, lam_spec, lam_spec,
                  pl.BlockSpec((1, V_DIM), lambda b: (0, 0))],
        out_specs=new_spec,
        out_shape=jax.ShapeDtypeStruct((B, Q, A), BF16),
        compiler_params=pltpu.CompilerParams(
            dimension_semantics=("parallel",),
            vmem_limit_bytes=_vmem_limit(blk_bytes + tmp_bytes)),
        name="sample_attn",
    )(qb, kb, vb, cache_k, cache_v, bias_c, bias_n,
      w["lambda_q1"], w["lambda_k1"], w["lambda_q2"], w["lambda_k2"], w["subln"])


def _gelu_tanh(x):
    return 0.5 * x * (1.0 + jnp.tanh(math.sqrt(2.0 / math.pi) * (x + 0.044715 * (x * x * x))))


def _lru_kernel(xr_ref, xg_ref, h0_ref, c0_ref, cw_ref, cb_ref, wgate_ref, bgate_ref, L_ref, gn_ref,
                r_ref, hl_ref, cn_ref, xbuf_ref, a_ref, b_ref, h_ref, hc_ref):
    t = pl.program_id(1)
    ts, W = xr_ref.shape[1:]
    tail = CONV_WIDTH - 1
    base = SUBLANES

    @pl.when(t == 0)
    def _():
        xbuf_ref[base - tail:base, :] = c0_ref[0]
        hc_ref[...] = jnp.broadcast_to(h0_ref[0], hc_ref.shape)

    xbuf_ref[base:base + ts, :] = xr_ref[0]
    xc = cb_ref[...]
    for j in range(CONV_WIDTH):
        xc = xc + xbuf_ref[base - tail + j:base - tail + j + ts, :] * cw_ref[j:j + 1, :]
    xbuf_ref[base - tail:base, :] = xbuf_ref[base + ts - tail:base + ts, :]

    gates = jnp.dot(xc.astype(BF16), wgate_ref[...], preferred_element_type=F32) + bgate_ref[...]
    r = _sigmoid(gates[:, :W])
    i = _sigmoid(gates[:, W:])
    z = -L_ref[...]
    softplus = jnp.maximum(z, 0.0) + jnp.log1p(jnp.exp(-jnp.abs(z)))
    log_a = -LRU_C * r * softplus
    a_ref[...] = jnp.exp(log_a)
    th = jnp.tanh(log_a)
    b_ref[...] = jnp.sqrt(-2.0 * th / (1.0 - th)) * (i * xc)

    row = lax.broadcasted_iota(jnp.int32, (SUBLANES, W), 0)

    def group(g, hprev):
        rows = pl.ds(pl.multiple_of(g * SUBLANES, SUBLANES), SUBLANES)
        A = a_ref[rows, :]
        Bv = b_ref[rows, :]
        d = 1
        while d < SUBLANES:
            keep = row >= d
            A_sh = jnp.where(keep, pltpu.roll(A, d, 0), 1.0)
            B_sh = jnp.where(keep, pltpu.roll(Bv, d, 0), 0.0)
            Bv = A * B_sh + Bv
            A = A * A_sh
            d *= 2
        Hg = A * hprev + Bv
        h_ref[rows, :] = Hg
        return jnp.broadcast_to(Hg[SUBLANES - 1:SUBLANES, :], hprev.shape)

    hc_ref[...] = lax.fori_loop(0, ts // SUBLANES, group, hc_ref[...])

    out = h_ref[...] * _gelu_tanh(xg_ref[0])
    r_ref[0] = _rms(out, gn_ref[...]).astype(r_ref.dtype)

    @pl.when(t == pl.num_programs(1) - 1)
    def _():
        hl_ref[0] = hc_ref[0:1, :]
        cn_ref[0] = xbuf_ref[base - tail:base, :]


def _lru(xr, xg, h0, conv0, w):
    B, S, W = xr.shape
    ts = min(LRU_TILE, S)
    assert S % ts == 0 and ts % SUBLANES == 0 and S >= CONV_WIDTH - 1
    tail = CONV_WIDTH - 1
    seq = pl.BlockSpec((1, ts, W), lambda b, t: (b, t, 0))
    per_b = lambda r: pl.BlockSpec((1, r, W), lambda b, t: (b, 0, 0))
    const = lambda shape: pl.BlockSpec(shape, lambda b, t: (0, 0))
    return pl.pallas_call(
        _lru_kernel,
        grid=(B, S // ts),
        in_specs=[seq, seq, per_b(1), per_b(tail),
                  const((CONV_WIDTH, W)), const((1, W)), const((W, 2 * W)), const((1, 2 * W)),
                  const((1, W)), const((1, W))],
        out_specs=[seq, per_b(1), per_b(tail)],
        out_shape=[jax.ShapeDtypeStruct((B, S, W), BF16),
                   jax.ShapeDtypeStruct((B, 1, W), F32),
                   jax.ShapeDtypeStruct((B, tail, W), F32)],
        scratch_shapes=[pltpu.VMEM((ts + SUBLANES, W), F32), pltpu.VMEM((ts, W), F32),
                        pltpu.VMEM((ts, W), F32), pltpu.VMEM((ts, W), F32),
                        pltpu.VMEM((SUBLANES, W), F32)],
        compiler_params=pltpu.CompilerParams(dimension_semantics=("parallel", "arbitrary")),
        name="rg_lru",
    )(xr, xg, h0.reshape(B, 1, W), conv0, w["conv_w"], w["conv_b"], w["w_gate"], w["b_gate"],
      w["lru_L"], w["lru_out_norm"])


def _block_diag(wb):
    n, c, d = wb.shape
    eye = jnp.eye(n, dtype=wb.dtype)
    return (eye[:, None, :, None] * wb[:, :, None, :]).reshape(n * c, n * d)


def _prepare_weights(l, p):
    D, d_ff = p["ffn1_gate"].shape[1:]
    assert d_ff % FF_CHUNK == 0
    n_ff = d_ff // FF_CHUNK
    lw = p["conv_w"].shape[-1]
    in_width = p["w_in"].shape[-1]
    a = (in_width - 2 * lw) // 3
    assert a % V_DIM == 0

    def up_chunks(wm):
        return wm.reshape(D, n_ff, FF_CHUNK).transpose(1, 0, 2).astype(BF16)

    def down_chunks(wm):
        return wm.reshape(n_ff, FF_CHUNK, D).astype(BF16)

    group = np.arange(a) // HEAD_DIM
    w = dict(attn_width=a, lru_width=lw)
    for name in ("ffn1", "ffn2"):
        w[name + "_gate"] = up_chunks(p[name + "_gate"][l])
        w[name + "_up"] = up_chunks(p[name + "_up"][l])
        w[name + "_down"] = down_chunks(p[name + "_down"][l])
    for name in ("norm_ffn1", "norm_mix", "norm_ffn2", "lru_out_norm", "conv_b", "lru_L", "subln",
                 "lambda_q1", "lambda_k1", "lambda_q2", "lambda_k2"):
        w[name] = p[name][l][None, :]
    w["w_in"] = p["w_in"][l].astype(BF16)
    w["group_mat"] = jnp.asarray((group[:, None] == group[None, :]) / HEAD_DIM, BF16)
    w["q_gain"] = jnp.tile(p["q_norm"][l], a // HEAD_DIM)[None, :]
    w["k_gain"] = jnp.tile(p["k_norm"][l], a // HEAD_DIM)[None, :]
    w["w_out_attn"] = p["w_out"][l][:a].astype(BF16)
    w["w_out_lru"] = p["w_out"][l][a:].astype(BF16)
    w["conv_w"] = p["conv_w"][l]
    w["w_gate"] = jnp.concatenate([_block_diag(p["gate_a_w"][l]), _block_diag(p["gate_x_w"][l])],
                                  axis=1).astype(BF16)
    w["b_gate"] = jnp.concatenate([p["gate_a_b"][l], p["gate_x_b"][l]])[None, :]
    return w


def _layer_prompt(x, w, rel_bias):
    B, S, D = x.shape
    T = ATTN_TILE
    a, lw = w["attn_width"], w["lru_width"]
    H = a // V_DIM
    assert S % T == 0 and T % CHUNK == 0 and T >= MAX_DISTANCE
    x1, k, v, xr, xg, qT, kb, vT = _stage_a(x, w, transposed=True)
    pos = jnp.arange(T, dtype=jnp.int32)
    idx = jnp.stack([_bucket_tile(pos + T, pos + T, True), _bucket_tile(pos + T, pos, True)])
    bias = _bias_tiles(rel_bias, idx)
    nq = S // T
    o = _prompt_attention(qT.reshape(B, nq, a, T), kb.reshape(B, S, a), vT.reshape(B, nq, a, T),
                          bias, w, B, S)
    r, h_last, conv_new = _lru(xr.reshape(B, S, lw), xg.reshape(B, S, lw),
                               jnp.zeros((B, lw), F32), jnp.zeros((B, CONV_WIDTH - 1, lw), F32), w)
    y = _stage_d(x1, o.reshape(B * S, a), r.reshape(B * S, lw), w)
    return (y.reshape(B, S, D), k.reshape(B, S, H, 2, HEAD_DIM), v.reshape(B, S, H, V_DIM),
            h_last.reshape(B, lw), conv_new)


def _layer_sample(x, w, rel_bias, k_past, v_past, h0, conv0):
    B, Q, D = x.shape
    P = k_past.shape[1]
    a, lw = w["attn_width"], w["lru_width"]
    H = a // V_DIM
    near = LANES
    assert P % CHUNK == 0 and Q <= CHUNK and near >= MAX_DISTANCE and P > near
    x1, k, v, xr, xg, qb, kb, vb = _stage_a(x, w, transposed=False)
    q_pos = P + jnp.arange(Q, dtype=jnp.int32)
    idx_c = _bucket_tile(q_pos, jnp.arange(P - near, P, dtype=jnp.int32), False)
    idx_n = _bucket_tile(q_pos, q_pos, False)
    bias_c = _bias_tiles(rel_bias, idx_c[None])[:, 0]
    bias_n = _bias_tiles(rel_bias, idx_n[None])[:, 0]
    o = _sample_attention(qb.reshape(B, Q, a), kb.reshape(B, Q, a), vb.reshape(B, Q, a),
                          k_past.reshape(B, P, a), v_past.reshape(B, P, a), bias_c, bias_n, w)
    r, h_last, conv_new = _lru(xr.reshape(B, Q, lw), xg.reshape(B, Q, lw), h0, conv0, w)
    y = _stage_d(x1, o.reshape(B * Q, a), r.reshape(B * Q, lw), w)
    return (y.reshape(B, Q, D), k.reshape(B, Q, H, 2, HEAD_DIM), v.reshape(B, Q, H, V_DIM),
            h_last.reshape(B, lw), conv_new)


def kernel(x_prompt, x_sample, cache_k, cache_v, state_lru, state_conv, rel_bias, norm_ffn1, ffn1_gate, ffn1_up, ffn1_down, norm_mix, w_in, q_norm, k_norm, lambda_q1, lambda_k1, lambda_q2, lambda_k2, subln, conv_w, conv_b, gate_a_w, gate_a_b, gate_x_w, gate_x_b, lru_L, lru_out_norm, w_out, norm_ffn2, ffn2_gate, ffn2_up, ffn2_down):
    p = dict(norm_ffn1=norm_ffn1, ffn1_gate=ffn1_gate, ffn1_up=ffn1_up, ffn1_down=ffn1_down,
             norm_mix=norm_mix, w_in=w_in, q_norm=q_norm, k_norm=k_norm,
             lambda_q1=lambda_q1, lambda_k1=lambda_k1, lambda_q2=lambda_q2, lambda_k2=lambda_k2,
             subln=subln, conv_w=conv_w, conv_b=conv_b, gate_a_w=gate_a_w, gate_a_b=gate_a_b,
             gate_x_w=gate_x_w, gate_x_b=gate_x_b, lru_L=lru_L, lru_out_norm=lru_out_norm,
             w_out=w_out, norm_ffn2=norm_ffn2, ffn2_gate=ffn2_gate, ffn2_up=ffn2_up,
             ffn2_down=ffn2_down)
    depth = cache_k.shape[0]
    assert depth == 1, "the lambda initial value is specialised to a single layer"
    w = _prepare_weights(0, p)
    yp, kp, vp, hp, cp = _layer_prompt(x_prompt, w, rel_bias)
    ys, kn, vn, hn, cn = _layer_sample(x_sample, w, rel_bias, cache_k[0], cache_v[0],
                                       state_lru[0], state_conv[0])
    stack = lambda t: t[None]
    return (yp, ys, stack(kp), stack(vp), stack(hp), stack(cp),
            stack(kn), stack(vn), stack(hn), stack(cn))
```

```python
import functools
import math

import numpy as np
import jax
import jax.numpy as jnp
from jax import lax
from jax.experimental import pallas as pl
from jax.experimental.pallas import tpu as pltpu

F32 = jnp.float32
BF16 = jnp.bfloat16

HEAD_DIM = 64
V_DIM = 2 * HEAD_DIM
CHUNK = 64
NUM_BUCKETS = 32
MAX_DISTANCE = 128
CONV_WIDTH = 4
LRU_BLOCKS = 8
LRU_C = 8.0
EPS = 1e-6
NEG_INF = -1e30
LAM_INIT = 0.8 - 0.6 * math.exp(-0.3 * 0)

LANES = 128
SUBLANES = 8
MXU_DIM = 256
VMEM_BYTES = 64 * 1024 * 1024

FF_CHUNK = MXU_DIM
ATTN_TILE = 512
LAYOUT_TILE = 256
LOG2E = 1.4426950408889634
TOKEN_TILE = 256
LRU_TILE = 256


def _vmem_limit(nbytes):
    return int(min(VMEM_BYTES - 4 * 1024 * 1024, max(nbytes, 16 * 1024 * 1024)))


def _rms(x, g):
    ms = jnp.mean(x * x, axis=-1, keepdims=True)
    return x * lax.rsqrt(ms + EPS) * g


def _sigmoid(x):
    return 1.0 / (1.0 + jnp.exp(-x))


def _const_spec(shape):
    n = len(shape)
    return pl.BlockSpec(shape, lambda *_: (0,) * n, pipeline_mode=pl.Buffered(1))


def _swiglu_into(acc_ref, xn, wg_ref, wu_ref, wd_ref):
    acc_ref[...] = jnp.zeros_like(acc_ref)

    def body(c, carry):
        g = jnp.dot(xn, wg_ref[c], preferred_element_type=F32)
        u = jnp.dot(xn, wu_ref[c], preferred_element_type=F32)
        h = (g * _sigmoid(g) * u).astype(BF16)
        acc_ref[...] += jnp.dot(h, wd_ref[c], preferred_element_type=F32)
        return carry

    lax.fori_loop(0, wg_ref.shape[0], body, 0)


def _stage_a_kernel(x_ref, g1_ref, wg_ref, wu_ref, wd_ref, gm_ref, win_ref, gmat_ref, qg_ref, kg_ref,
                    x1_ref, k_ref, v_ref, xr_ref, xg_ref, qb_ref, kb_ref, vb_ref, acc_ref,
                    *, attn_width, lru_width, transposed, tile, q_scale):
    x = x_ref[...]
    xn = _rms(x, g1_ref[...]).astype(BF16)
    _swiglu_into(acc_ref, xn, wg_ref, wu_ref, wd_ref)
    x1 = x + 0.5 * acc_ref[...]
    x1_ref[...] = x1
    hn = _rms(x1, gm_ref[...]).astype(BF16)
    proj = jnp.dot(hn, win_ref[...], preferred_element_type=F32)
    a = attn_width
    q = proj[:, :a]
    k = proj[:, a:2 * a]
    v = proj[:, 2 * a:3 * a]
    xr_ref[...] = proj[:, 3 * a:3 * a + lru_width]
    xg_ref[...] = proj[:, 3 * a + lru_width:]

    def group_norm(t, g):
        ms = jnp.dot((t * t).astype(BF16), gmat_ref[...], preferred_element_type=F32)
        return t * lax.rsqrt(ms + EPS) * g

    qn = group_norm(q, qg_ref[...]) * q_scale
    kn = group_norm(k, kg_ref[...])
    k_ref[...] = kn
    v_ref[...] = v
    kb_ref[...] = kn.astype(BF16)
    if transposed:
        for j in range(x.shape[0] // tile):
            rows = slice(j * tile, (j + 1) * tile)
            qb_ref[0, j] = qn[rows, :].T.astype(BF16)
            vb_ref[0, j] = v[rows, :].T.astype(BF16)
    else:
        qb_ref[...] = qn.astype(BF16)
        vb_ref[...] = v.astype(BF16)


def _stage_a(x, w, *, transposed):
    B, S, D = x.shape
    a = w["attn_width"]
    lw = w["lru_width"]
    tm = TOKEN_TILE
    T = LAYOUT_TILE
    q_scale = HEAD_DIM ** -0.5 * (LOG2E if transposed else 1.0)
    assert (B * S) % tm == 0 and (not transposed or (tm % T == 0 and S % tm == 0))
    nt = S // tm
    x2 = x.reshape(B * S, D)
    n_ff = w["ffn1_gate"].shape[0]
    tok = lambda width: pl.BlockSpec((tm, width), lambda i: (i, 0))
    out_shape = [
        jax.ShapeDtypeStruct((B * S, D), F32),
        jax.ShapeDtypeStruct((B * S, a), F32),
        jax.ShapeDtypeStruct((B * S, a), F32),
        jax.ShapeDtypeStruct((B * S, lw), F32),
        jax.ShapeDtypeStruct((B * S, lw), F32),
    ]
    out_specs = [tok(D), tok(a), tok(a), tok(lw), tok(lw)]
    if transposed:
        tshape = jax.ShapeDtypeStruct((B * nt, tm // T, a, T), BF16)
        tspec = pl.BlockSpec((1, tm // T, a, T), lambda i: (i, 0, 0, 0))
        out_shape += [tshape, jax.ShapeDtypeStruct((B * S, a), BF16), tshape]
        out_specs += [tspec, tok(a), tspec]
    else:
        bshape = jax.ShapeDtypeStruct((B * S, a), BF16)
        out_shape += [bshape, bshape, bshape]
        out_specs += [tok(a), tok(a), tok(a)]
    weight_bytes = 2 * (3 * D * n_ff * FF_CHUNK + D * (3 * a + 2 * lw) + a * a)
    act_bytes = 4 * tm * (2 * 2 * D + 2 * 2 * (2 * a + 2 * lw) + D + 2 * (3 * a + 2 * lw) + 4 * FF_CHUNK)
    kern = functools.partial(_stage_a_kernel, attn_width=a, lru_width=lw, transposed=transposed,
                             tile=T, q_scale=q_scale)
    outs = pl.pallas_call(
        kern,
        grid=(B * S // tm,),
        in_specs=[
            tok(D),
            _const_spec((1, D)),
            _const_spec(w["ffn1_gate"].shape),
            _const_spec(w["ffn1_up"].shape),
            _const_spec(w["ffn1_down"].shape),
            _const_spec((1, D)),
            _const_spec(w["w_in"].shape),
            _const_spec((a, a)),
            _const_spec((1, a)),
            _const_spec((1, a)),
        ],
        out_specs=out_specs,
        out_shape=out_shape,
        scratch_shapes=[pltpu.VMEM((tm, D), F32)],
        compiler_params=pltpu.CompilerParams(
            dimension_semantics=("parallel",),
            vmem_limit_bytes=_vmem_limit(weight_bytes + act_bytes)),
        name="stage_a_t" if transposed else "stage_a_n",
    )(x2, w["norm_ffn1"], w["ffn1_gate"], w["ffn1_up"], w["ffn1_down"], w["norm_mix"], w["w_in"],
      w["group_mat"], w["q_gain"], w["k_gain"])
    return outs


def _stage_d_kernel(x1_ref, o_ref, r_ref, woa_ref, wor_ref, g2_ref, wg_ref, wu_ref, wd_ref,
                    y_ref, acc_ref):
    x2 = (x1_ref[...]
          + jnp.dot(o_ref[...], woa_ref[...], preferred_element_type=F32)
          + jnp.dot(r_ref[...], wor_ref[...], preferred_element_type=F32))
    xn = _rms(x2, g2_ref[...]).astype(BF16)
    _swiglu_into(acc_ref, xn, wg_ref, wu_ref, wd_ref)
    y_ref[...] = x2 + 0.5 * acc_ref[...]


def _stage_d(x1, o, r, w):
    N, D = x1.shape
    a = o.shape[1]
    lw = r.shape[1]
    tm = min(TOKEN_TILE, N)
    assert N % tm == 0
    n_ff = w["ffn2_gate"].shape[0]
    tok = lambda width: pl.BlockSpec((tm, width), lambda i: (i, 0))
    weight_bytes = 2 * (3 * D * n_ff * FF_CHUNK + D * D)
    act_bytes = 4 * tm * (2 * 2 * D + 2 * (a + lw) + 2 * D + 4 * FF_CHUNK)
    return pl.pallas_call(
        _stage_d_kernel,
        grid=(N // tm,),
        in_specs=[
            tok(D), tok(a), tok(lw),
            _const_spec((a, D)), _const_spec((lw, D)), _const_spec((1, D)),
            _const_spec(w["ffn2_gate"].shape), _const_spec(w["ffn2_up"].shape),
            _const_spec(w["ffn2_down"].shape),
        ],
        out_specs=tok(D),
        out_shape=jax.ShapeDtypeStruct((N, D), F32),
        scratch_shapes=[pltpu.VMEM((tm, D), F32)],
        compiler_params=pltpu.CompilerParams(
            dimension_semantics=("parallel",),
            vmem_limit_bytes=_vmem_limit(weight_bytes + act_bytes)),
        name="stage_d",
    )(x1, o, r, w["w_out_attn"], w["w_out_lru"], w["norm_ffn2"],
      w["ffn2_gate"], w["ffn2_up"], w["ffn2_down"])


def _t5_bucket(rel):
    n = NUM_BUCKETS // 2
    max_exact = n // 2
    ret = jnp.where(rel > 0, n, 0)
    rel = jnp.abs(rel)
    relf = jnp.maximum(rel, 1).astype(jnp.float32)
    large = max_exact + (jnp.log(relf / max_exact) / math.log(MAX_DISTANCE / max_exact)
                         * (n - max_exact)).astype(jnp.int32)
    large = jnp.minimum(large, n - 1)
    return ret + jnp.where(rel < max_exact, rel, large)


FAR_BUCKET = NUM_BUCKETS // 2 - 1


def _bucket_tile(q_pos, k_pos, keys_first):
    rel = k_pos[None, :] - q_pos[:, None]
    visible = (k_pos[None, :] // CHUNK) <= (q_pos[:, None] // CHUNK)
    idx = jnp.where(visible, _t5_bucket(rel), -1)
    return idx.T if keys_first else idx


def _bias_kernel(table_ref, idx_ref, out_ref, *, scale):
    h = pl.program_id(0)
    idx = idx_ref[0]
    acc = jnp.zeros(idx.shape, F32)
    for b in range(NUM_BUCKETS):
        acc = jnp.where(idx == b, table_ref[b, h], acc)
    out_ref[0, 0] = jnp.where(idx < 0, NEG_INF, (acc - table_ref[FAR_BUCKET, h]) * scale)


def _bias_tiles(rel_bias, idx, scale=1.0):
    n, R, C = idx.shape
    H = rel_bias.shape[1]
    return pl.pallas_call(
        functools.partial(_bias_kernel, scale=scale),
        grid=(H, n),
        in_specs=[pl.BlockSpec(memory_space=pltpu.SMEM),
                  pl.BlockSpec((1, R, C), lambda h, i: (i, 0, 0))],
        out_specs=pl.BlockSpec((1, 1, R, C), lambda h, i: (h, i, 0, 0)),
        out_shape=jax.ShapeDtypeStruct((H, n, R, C), F32),
        name="bias_tiles",
    )(rel_bias, idx)


def _lambda(lq1_ref, lk1_ref, lq2_ref, lk2_ref):
    s1 = jnp.sum(lq1_ref[...] * lk1_ref[...], axis=-1, keepdims=True)
    s2 = jnp.sum(lq2_ref[...] * lk2_ref[...], axis=-1, keepdims=True)
    return jnp.exp(s1) - jnp.exp(s2) + LAM_INIT


def _prompt_attn_kernel(qT_ref, k_ref, vT_ref, bias_ref, lq1_ref, lk1_ref, lq2_ref, lk2_ref, sub_ref,
                        o_ref, m_ref, l_ref, acc_ref, *, tile):
    d, Tl = qT_ref.shape[2:]
    T = tile
    sub = T // Tl
    nq = qT_ref.shape[1] // sub
    lam = _lambda(lq1_ref, lk1_ref, lq2_ref, lk2_ref)
    first = lax.broadcasted_iota(jnp.int32, (d, T), 0) < HEAD_DIM

    def lane_tiles(ref, i):
        return jnp.concatenate([ref[0, i * sub + j] for j in range(sub)], axis=1)

    def scores(ki, q2):
        kt = k_ref[0, pl.ds(pl.multiple_of(ki * T, T), T), :]
        return jnp.dot(kt, q2, preferred_element_type=F32)

    def update(s, ki):
        m_prev = m_ref[...]
        m_new = jnp.maximum(m_prev, jnp.max(s, axis=0, keepdims=True))
        alpha = jnp.exp2(m_prev - m_new)
        p = jnp.exp2(s - m_new)
        l_ref[...] = alpha * l_ref[...] + jnp.sum(p, axis=0, keepdims=True)
        acc_ref[...] = alpha * acc_ref[...] + jnp.dot(lane_tiles(vT_ref, ki), p.astype(BF16),
                                                      preferred_element_type=F32)
        m_ref[...] = m_new

    def q_tile(qi, carry):
        qt = lane_tiles(qT_ref, qi)
        zero = jnp.zeros_like(qt)
        q2 = jnp.concatenate([jnp.where(first, qt, zero), jnp.where(first, zero, qt)], axis=1)
        m_ref[...] = jnp.full_like(m_ref, NEG_INF)
        l_ref[...] = jnp.zeros_like(l_ref)
        acc_ref[...] = jnp.zeros_like(acc_ref)

        bias = bias_ref[0, 0]
        update(scores(qi, q2) + jnp.concatenate([bias, bias], axis=1), qi)
        kp = jnp.maximum(qi - 1, 0)
        bias = bias_ref[0, jnp.where(qi > 0, 1, 2)]
        update(scores(kp, q2) + jnp.concatenate([bias, bias], axis=1), kp)

        n_far = qi - 1

        def far(ki, s_cur):
            s_next = scores(jnp.minimum(ki + 1, jnp.maximum(n_far - 1, 0)), q2)
            update(s_cur, ki)
            return s_next

        lax.fori_loop(0, n_far, far, scores(0, q2))

        inv = 1.0 / l_ref[...]
        on = acc_ref[...] * inv
        o = on[:, :T] - lam * on[:, T:]
        o = o * lax.rsqrt(jnp.mean(o * o, axis=0, keepdims=True) + EPS)
        ot = o.T * sub_ref[...] * (1.0 - LAM_INIT)
        o_ref[0, pl.ds(pl.multiple_of(qi * T, T), T), :] = ot.astype(o_ref.dtype)
        return carry

    lax.fori_loop(0, nq, q_tile, 0)


def _prompt_attention(qT, kb, vT, bias, w, B, S):
    T = ATTN_TILE
    Tl = qT.shape[-1]
    A = kb.shape[-1]
    H = A // V_DIM
    lam_spec = pl.BlockSpec((1, HEAD_DIM), lambda b, h: (0, 0))
    tile_spec = pl.BlockSpec((1, S // Tl, V_DIM, Tl), lambda b, h: (b, 0, h, 0))
    seq_spec = pl.BlockSpec((1, S, V_DIM), lambda b, h: (b, 0, h))
    blk_bytes = 2 * 2 * (3 * S * V_DIM) + 2 * 2 * S * V_DIM + 2 * 4 * 3 * T * T
    tmp_bytes = 4 * (V_DIM * 2 * T + 8 * T * 2 * T)
    return pl.pallas_call(
        functools.partial(_prompt_attn_kernel, tile=T),
        grid=(B, H),
        in_specs=[tile_spec, seq_spec, tile_spec,
                  pl.BlockSpec((1, 3, T, T), lambda b, h: (h, 0, 0, 0)),
                  lam_spec, lam_spec, lam_spec, lam_spec,
                  pl.BlockSpec((1, V_DIM), lambda b, h: (0, 0))],
        out_specs=seq_spec,
        out_shape=jax.ShapeDtypeStruct((B, S, A), BF16),
        scratch_shapes=[pltpu.VMEM((1, 2 * T), F32), pltpu.VMEM((1, 2 * T), F32),
                        pltpu.VMEM((V_DIM, 2 * T), F32)],
        compiler_params=pltpu.CompilerParams(
            dimension_semantics=("parallel", "parallel"),
            vmem_limit_bytes=_vmem_limit(blk_bytes + tmp_bytes)),
        name="prompt_attn",
    )(qT, kb, vT, bias, w["lambda_q1"], w["lambda_k1"], w["lambda_q2"], w["lambda_k2"], w["subln"])


def _sample_attn_kernel(q_ref, kn_ref, vn_ref, ck_ref, cv_ref, bc_ref, bn_ref,
                        lq1_ref, lk1_ref, lq2_ref, lk2_ref, sub_ref, o_ref, *, near):
    Q = q_ref.shape[1]
    P = ck_ref.shape[1]
    H = q_ref.shape[2] // V_DIM
    lam = _lambda(lq1_ref, lk1_ref, lq2_ref, lk2_ref)
    first = lax.broadcasted_iota(jnp.int32, (Q, V_DIM), 1) < HEAD_DIM
    nt = (((1,), (1,)), ((), ()))
    far = P - near
    for h in range(H):
        cols = slice(h * V_DIM, (h + 1) * V_DIM)
        qh = q_ref[0, :, cols]
        zero = jnp.zeros_like(qh)
        q2 = jnp.concatenate([jnp.where(first, qh, zero), jnp.where(first, zero, qh)], axis=0)
        k_far = ck_ref[0, :far, cols].astype(BF16)
        k_near = ck_ref[0, far:, cols].astype(BF16)
        bc = bc_ref[h]
        bn = bn_ref[h]
        s_far = lax.dot_general(q2, k_far, nt, preferred_element_type=F32)
        s_near = (lax.dot_general(q2, k_near, nt, preferred_element_type=F32)
                  + jnp.concatenate([bc, bc], axis=0))
        s_new = (lax.dot_general(q2, kn_ref[0, :, cols], nt, preferred_element_type=F32)
                 + jnp.concatenate([bn, bn], axis=0))
        m = jnp.maximum(jnp.maximum(jnp.max(s_far, axis=-1, keepdims=True),
                                    jnp.max(s_near, axis=-1, keepdims=True)),
                        jnp.max(s_new, axis=-1, keepdims=True))
        p_far = jnp.exp(s_far - m)
        p_near = jnp.exp(s_near - m)
        p_new = jnp.exp(s_new - m)
        l = (jnp.sum(p_far, axis=-1, keepdims=True) + jnp.sum(p_near, axis=-1, keepdims=True)
             + jnp.sum(p_new, axis=-1, keepdims=True))
        acc = (jnp.dot(p_far.astype(BF16), cv_ref[0, :far, cols].astype(BF16),
                       preferred_element_type=F32)
               + jnp.dot(p_near.astype(BF16), cv_ref[0, far:, cols].astype(BF16),
                         preferred_element_type=F32)
               + jnp.dot(p_new.astype(BF16), vn_ref[0, :, cols], preferred_element_type=F32))
        on = acc * (1.0 / l)
        o = on[:Q] - lam * on[Q:]
        o = o * lax.rsqrt(jnp.mean(o * o, axis=-1, keepdims=True) + EPS)
        o_ref[0, :, cols] = (o * sub_ref[...] * (1.0 - LAM_INIT)).astype(o_ref.dtype)


def _sample_attention(qb, kb, vb, cache_k, cache_v, bias_c, bias_n, w):
    B, Q, A = qb.shape
    P = cache_k.shape[1]
    near = bias_c.shape[-1]
    new_spec = pl.BlockSpec((1, Q, A), lambda b: (b, 0, 0))
    cache_spec = pl.BlockSpec((1, P, A), lambda b: (b, 0, 0))
    lam_spec = pl.BlockSpec((1, HEAD_DIM), lambda b: (0, 0))
    blk_bytes = 2 * 2 * 4 * P * A + 2 * 4 * 2 * Q * A
    tmp_bytes = 4 * 2 * Q * (3 * P + 2 * P) + 2 * 2 * P * V_DIM
    return pl.pallas_call(
        functools.partial(_sample_attn_kernel, near=near),
        grid=(B,),
        in_specs=[new_spec, new_spec, new_spec, cache_spec, cache_spec,
                  pl.BlockSpec(bias_c.shape, lambda b: (0, 0, 0)),
                  pl.BlockSpec(bias_n.shape, lambda b: (0, 0, 0)),
                  lam_spec, lam_spec, lam_spec, lam_spec,
                  pl.BlockSpec((1, V_DIM), lambda b: (0, 0))],
        out_specs=new_spec,
        out_shape=jax.ShapeDtypeStruct((B, Q, A), BF16),
        compiler_params=pltpu.CompilerParams(
            dimension_semantics=("parallel",),
            vmem_limit_bytes=_vmem_limit(blk_bytes + tmp_bytes)),
        name="sample_attn",
    )(qb, kb, vb, cache_k, cache_v, bias_c, bias_n,
      w["lambda_q1"], w["lambda_k1"], w["lambda_q2"], w["lambda_k2"], w["subln"])


def _gelu_tanh(x):
    return 0.5 * x * (1.0 + jnp.tanh(math.sqrt(2.0 / math.pi) * (x + 0.044715 * (x * x * x))))


def _lru_kernel(xr_ref, xg_ref, h0_ref, c0_ref, cw_ref, cb_ref, wgate_ref, bgate_ref, L_ref, gn_ref,
                r_ref, hl_ref, cn_ref, xbuf_ref, a_ref, b_ref, h_ref, hc_ref):
    t = pl.program_id(1)
    ts, W = xr_ref.shape[1:]
    tail = CONV_WIDTH - 1
    base = SUBLANES

    @pl.when(t == 0)
    def _():
        xbuf_ref[base - tail:base, :] = c0_ref[0]
        hc_ref[...] = jnp.broadcast_to(h0_ref[0], hc_ref.shape)

    xbuf_ref[base:base + ts, :] = xr_ref[0]
    xc = cb_ref[...]
    for j in range(CONV_WIDTH):
        xc = xc + xbuf_ref[base - tail + j:base - tail + j + ts, :] * cw_ref[j:j + 1, :]
    xbuf_ref[base - tail:base, :] = xbuf_ref[base + ts - tail:base + ts, :]

    gates = jnp.dot(xc.astype(BF16), wgate_ref[...], preferred_element_type=F32) + bgate_ref[...]
    r = _sigmoid(gates[:, :W])
    i = _sigmoid(gates[:, W:])
    z = -L_ref[...]
    softplus = jnp.maximum(z, 0.0) + jnp.log1p(jnp.exp(-jnp.abs(z)))
    log_a = -LRU_C * r * softplus
    a_ref[...] = jnp.exp(log_a)
    th = jnp.tanh(log_a)
    b_ref[...] = jnp.sqrt(-2.0 * th / (1.0 - th)) * (i * xc)

    row = lax.broadcasted_iota(jnp.int32, (SUBLANES, W), 0)

    def group(g, hprev):
        rows = pl.ds(pl.multiple_of(g * SUBLANES, SUBLANES), SUBLANES)
        A = a_ref[rows, :]
        Bv = b_ref[rows, :]
        d = 1
        while d < SUBLANES:
            keep = row >= d
            A_sh = jnp.where(keep, pltpu.roll(A, d, 0), 1.0)
            B_sh = jnp.where(keep, pltpu.roll(Bv, d, 0), 0.0)
            Bv = A * B_sh + Bv
            A = A * A_sh
            d *= 2
        Hg = A * hprev + Bv
        h_ref[rows, :] = Hg
        return jnp.broadcast_to(Hg[SUBLANES - 1:SUBLANES, :], hprev.shape)

    hc_ref[...] = lax.fori_loop(0, ts // SUBLANES, group, hc_ref[...])

    out = h_ref[...] * _gelu_tanh(xg_ref[0])
    r_ref[0] = _rms(out, gn_ref[...]).astype(r_ref.dtype)

    @pl.when(t == pl.num_programs(1) - 1)
    def _():
        hl_ref[0] = hc_ref[0:1, :]
        cn_ref[0] = xbuf_ref[base - tail:base, :]


def _lru(xr, xg, h0, conv0, w):
    B, S, W = xr.shape
    ts = min(LRU_TILE, S)
    assert S % ts == 0 and ts % SUBLANES == 0 and S >= CONV_WIDTH - 1
    tail = CONV_WIDTH - 1
    seq = pl.BlockSpec((1, ts, W), lambda b, t: (b, t, 0))
    per_b = lambda r: pl.BlockSpec((1, r, W), lambda b, t: (b, 0, 0))
    const = lambda shape: pl.BlockSpec(shape, lambda b, t: (0, 0))
    return pl.pallas_call(
        _lru_kernel,
        grid=(B, S // ts),
        in_specs=[seq, seq, per_b(1), per_b(tail),
                  const((CONV_WIDTH, W)), const((1, W)), const((W, 2 * W)), const((1, 2 * W)),
                  const((1, W)), const((1, W))],
        out_specs=[seq, per_b(1), per_b(tail)],
        out_shape=[jax.ShapeDtypeStruct((B, S, W), BF16),
                   jax.ShapeDtypeStruct((B, 1, W), F32),
                   jax.ShapeDtypeStruct((B, tail, W), F32)],
        scratch_shapes=[pltpu.VMEM((ts + SUBLANES, W), F32), pltpu.VMEM((ts, W), F32),
                        pltpu.VMEM((ts, W), F32), pltpu.VMEM((ts, W), F32),
                        pltpu.VMEM((SUBLANES, W), F32)],
        compiler_params=pltpu.CompilerParams(dimension_semantics=("parallel", "arbitrary")),
        name="rg_lru",
    )(xr, xg, h0.reshape(B, 1, W), conv0, w["conv_w"], w["conv_b"], w["w_gate"], w["b_gate"],
      w["lru_L"], w["lru_out_norm"])


def _block_diag(wb):
    n, c, d = wb.shape
    eye = jnp.eye(n, dtype=wb.dtype)
    return (eye[:, None, :, None] * wb[:, :, None, :]).reshape(n * c, n * d)


def _prepare_weights(l, p):
    D, d_ff = p["ffn1_gate"].shape[1:]
    assert d_ff % FF_CHUNK == 0
    n_ff = d_ff // FF_CHUNK
    lw = p["conv_w"].shape[-1]
    in_width = p["w_in"].shape[-1]
    a = (in_width - 2 * lw) // 3
    assert a % V_DIM == 0

    def up_chunks(wm):
        return wm.reshape(D, n_ff, FF_CHUNK).transpose(1, 0, 2).astype(BF16)

    def down_chunks(wm):
        return wm.reshape(n_ff, FF_CHUNK, D).astype(BF16)

    group = np.arange(a) // HEAD_DIM
    w = dict(attn_width=a, lru_width=lw)
    for name in ("ffn1", "ffn2"):
        w[name + "_gate"] = up_chunks(p[name + "_gate"][l])
        w[name + "_up"] = up_chunks(p[name + "_up"][l])
        w[name + "_down"] = down_chunks(p[name + "_down"][l])
    for name in ("norm_ffn1", "norm_mix", "norm_ffn2", "lru_out_norm", "conv_b", "lru_L", "subln",
                 "lambda_q1", "lambda_k1", "lambda_q2", "lambda_k2"):
        w[name] = p[name][l][None, :]
    w["w_in"] = p["w_in"][l].astype(BF16)
    w["group_mat"] = jnp.asarray((group[:, None] == group[None, :]) / HEAD_DIM, BF16)
    w["q_gain"] = jnp.tile(p["q_norm"][l], a // HEAD_DIM)[None, :]
    w["k_gain"] = jnp.tile(p["k_norm"][l], a // HEAD_DIM)[None, :]
    w["w_out_attn"] = p["w_out"][l][:a].astype(BF16)
    w["w_out_lru"] = p["w_out"][l][a:].astype(BF16)
    w["conv_w"] = p["conv_w"][l]
    w["w_gate"] = jnp.concatenate([_block_diag(p["gate_a_w"][l]), _block_diag(p["gate_x_w"][l])],
                                  axis=1).astype(BF16)
    w["b_gate"] = jnp.concatenate([p["gate_a_b"][l], p["gate_x_b"][l]])[None, :]
    return w


def _layer_prompt(x, w, rel_bias):
    B, S, D = x.shape
    T = ATTN_TILE
    a, lw = w["attn_width"], w["lru_width"]
    H = a // V_DIM
    assert S % T == 0 and T % CHUNK == 0 and T >= MAX_DISTANCE and T % LAYOUT_TILE == 0
    x1, k, v, xr, xg, qT, kb, vT = _stage_a(x, w, transposed=True)
    pos = jnp.arange(T, dtype=jnp.int32)
    idx = jnp.stack([_bucket_tile(pos + T, pos + T, True), _bucket_tile(pos + T, pos, True),
                     jnp.full((T, T), -1, jnp.int32)])
    bias = _bias_tiles(rel_bias, idx, LOG2E)
    nl = S // LAYOUT_TILE
    o = _prompt_attention(qT.reshape(B, nl, a, LAYOUT_TILE), kb.reshape(B, S, a),
                          vT.reshape(B, nl, a, LAYOUT_TILE), bias, w, B, S)
    r, h_last, conv_new = _lru(xr.reshape(B, S, lw), xg.reshape(B, S, lw),
                               jnp.zeros((B, lw), F32), jnp.zeros((B, CONV_WIDTH - 1, lw), F32), w)
    y = _stage_d(x1, o.reshape(B * S, a), r.reshape(B * S, lw), w)
    return (y.reshape(B, S, D), k.reshape(B, S, H, 2, HEAD_DIM), v.reshape(B, S, H, V_DIM),
            h_last.reshape(B, lw), conv_new)


def _layer_sample(x, w, rel_bias, k_past, v_past, h0, conv0):
    B, Q, D = x.shape
    P = k_past.shape[1]
    a, lw = w["attn_width"], w["lru_width"]
    H = a // V_DIM
    near = LANES
    assert P % CHUNK == 0 and Q <= CHUNK and near >= MAX_DISTANCE and P > near
    x1, k, v, xr, xg, qb, kb, vb = _stage_a(x, w, transposed=False)
    q_pos = P + jnp.arange(Q, dtype=jnp.int32)
    idx_c = _bucket_tile(q_pos, jnp.arange(P - near, P, dtype=jnp.int32), False)
    idx_n = _bucket_tile(q_pos, q_pos, False)
    bias_c = _bias_tiles(rel_bias, idx_c[None])[:, 0]
    bias_n = _bias_tiles(rel_bias, idx_n[None])[:, 0]
    o = _sample_attention(qb.reshape(B, Q, a), kb.reshape(B, Q, a), vb.reshape(B, Q, a),
                          k_past.reshape(B, P, a), v_past.reshape(B, P, a), bias_c, bias_n, w)
    r, h_last, conv_new = _lru(xr.reshape(B, Q, lw), xg.reshape(B, Q, lw), h0, conv0, w)
    y = _stage_d(x1, o.reshape(B * Q, a), r.reshape(B * Q, lw), w)
    return (y.reshape(B, Q, D), k.reshape(B, Q, H, 2, HEAD_DIM), v.reshape(B, Q, H, V_DIM),
            h_last.reshape(B, lw), conv_new)


def kernel(x_prompt, x_sample, cache_k, cache_v, state_lru, state_conv, rel_bias, norm_ffn1, ffn1_gate, ffn1_up, ffn1_down, norm_mix, w_in, q_norm, k_norm, lambda_q1, lambda_k1, lambda_q2, lambda_k2, subln, conv_w, conv_b, gate_a_w, gate_a_b, gate_x_w, gate_x_b, lru_L, lru_out_norm, w_out, norm_ffn2, ffn2_gate, ffn2_up, ffn2_down):
    p = dict(norm_ffn1=norm_ffn1, ffn1_gate=ffn1_gate, ffn1_up=ffn1_up, ffn1_down=ffn1_down,
             norm_mix=norm_mix, w_in=w_in, q_norm=q_norm, k_norm=k_norm,
             lambda_q1=lambda_q1, lambda_k1=lambda_k1, lambda_q2=lambda_q2, lambda_k2=lambda_k2,
             subln=subln, conv_w=conv_w, conv_b=conv_b, gate_a_w=gate_a_w, gate_a_b=gate_a_b,
             gate_x_w=gate_x_w, gate_x_b=gate_x_b, lru_L=lru_L, lru_out_norm=lru_out_norm,
             w_out=w_out, norm_ffn2=norm_ffn2, ffn2_gate=ffn2_gate, ffn2_up=ffn2_up,
             ffn2_down=ffn2_down)
    depth = cache_k.shape[0]
    assert depth == 1, "the lambda initial value is specialised to a single layer"
    w = _prepare_weights(0, p)
    yp, kp, vp, hp, cp = _layer_prompt(x_prompt, w, rel_bias)
    ys, kn, vn, hn, cn = _layer_sample(x_sample, w, rel_bias, cache_k[0], cache_v[0],
                                       state_lru[0], state_conv[0])
    stack = lambda t: t[None]
    return (yp, ys, stack(kp), stack(vp), stack(hp), stack(cp),
            stack(kn), stack(vn), stack(hn), stack(cn))
```

```python
import functools
import math

import numpy as np
import jax
import jax.numpy as jnp
from jax import lax
from jax.experimental import pallas as pl
from jax.experimental.pallas import tpu as pltpu

F32 = jnp.float32
BF16 = jnp.bfloat16

HEAD_DIM = 64
V_DIM = 2 * HEAD_DIM
CHUNK = 64
NUM_BUCKETS = 32
MAX_DISTANCE = 128
CONV_WIDTH = 4
LRU_BLOCKS = 8
LRU_C = 8.0
EPS = 1e-6
NEG_INF = -1e30
LAM_INIT = 0.8 - 0.6 * math.exp(-0.3 * 0)

LANES = 128
SUBLANES = 8
MXU_DIM = 256
VMEM_BYTES = 64 * 1024 * 1024

FF_CHUNK = MXU_DIM
ATTN_TILE = 512
LAYOUT_TILE = 256
FAR_ROWS = 512
LOG2E = 1.4426950408889634
SHIFT_RANGE = 100.0
TOKEN_TILE = 256
LRU_TILE = 256


def _vmem_limit(nbytes):
    return int(min(VMEM_BYTES - 4 * 1024 * 1024, max(nbytes, 16 * 1024 * 1024)))


def _rms(x, g):
    ms = jnp.mean(x * x, axis=-1, keepdims=True)
    return x * lax.rsqrt(ms + EPS) * g


def _sigmoid(x):
    return 1.0 / (1.0 + jnp.exp(-x))


def _const_spec(shape):
    n = len(shape)
    return pl.BlockSpec(shape, lambda *_: (0,) * n, pipeline_mode=pl.Buffered(1))


def _swiglu_into(acc_ref, xn, wg_ref, wu_ref, wd_ref):
    d_ff = wg_ref.shape[1]
    for c in range(0, d_ff, FF_CHUNK):
        cols = slice(c, c + FF_CHUNK)
        g = jnp.dot(xn, wg_ref[:, cols], preferred_element_type=F32)
        u = jnp.dot(xn, wu_ref[:, cols], preferred_element_type=F32)
        h = (g * _sigmoid(g) * u).astype(BF16)
        part = jnp.dot(h, wd_ref[cols, :], preferred_element_type=F32)
        if c == 0:
            acc_ref[...] = part
        else:
            acc_ref[...] += part


def _stage_a_kernel(x_ref, g1_ref, wg_ref, wu_ref, wd_ref, gm_ref, win_ref, gmat_ref, qg_ref, kg_ref,
                    x1_ref, k_ref, v_ref, xr_ref, xg_ref, qb_ref, kb_ref, vb_ref, acc_ref,
                    *, attn_width, lru_width, transposed, tile, q_scale):
    x = x_ref[...]
    xn = _rms(x, g1_ref[...]).astype(BF16)
    _swiglu_into(acc_ref, xn, wg_ref, wu_ref, wd_ref)
    x1 = x + 0.5 * acc_ref[...]
    x1_ref[...] = x1
    hn = _rms(x1, gm_ref[...]).astype(BF16)
    proj = jnp.dot(hn, win_ref[...], preferred_element_type=F32)
    a = attn_width
    q = proj[:, :a]
    k = proj[:, a:2 * a]
    v = proj[:, 2 * a:3 * a]
    xr_ref[...] = proj[:, 3 * a:3 * a + lru_width]
    xg_ref[...] = proj[:, 3 * a + lru_width:]

    def group_norm(t, g):
        ms = jnp.dot((t * t).astype(BF16), gmat_ref[...], preferred_element_type=F32)
        return t * lax.rsqrt(ms + EPS) * g

    qn = group_norm(q, qg_ref[...]) * q_scale
    kn = group_norm(k, kg_ref[...])
    n_heads = a // V_DIM
    for h in range(n_heads):
        v_ref[pl.ds(h, x.shape[0], stride=n_heads), :] = v[:, h * V_DIM:(h + 1) * V_DIM]
    kb_ref[...] = kn.astype(BF16)
    if transposed:
        k_ref[0] = kn.T
        for j in range(x.shape[0] // tile):
            rows = slice(j * tile, (j + 1) * tile)
            qb_ref[0, j] = qn[rows, :].T.astype(BF16)
            vb_ref[0, j] = v[rows, :].T.astype(BF16)
    else:
        k_ref[...] = kn
        qb_ref[...] = qn.astype(BF16)
        vb_ref[...] = v.astype(BF16)


def _stage_a(x, w, *, transposed):
    B, S, D = x.shape
    a = w["attn_width"]
    lw = w["lru_width"]
    tm = TOKEN_TILE
    T = LAYOUT_TILE
    q_scale = HEAD_DIM ** -0.5 * (LOG2E if transposed else 1.0)
    assert (B * S) % tm == 0 and (not transposed or (tm % T == 0 and S % tm == 0))
    nt = S // tm
    x2 = x.reshape(B * S, D)
    d_ff = w["ffn1_gate"].shape[1]
    tok = lambda width: pl.BlockSpec((tm, width), lambda i: (i, 0))
    out_shape = [
        jax.ShapeDtypeStruct((B * S, D), F32),
        jax.ShapeDtypeStruct((B * S, a), F32),
        jax.ShapeDtypeStruct((B * S, a), F32),
        jax.ShapeDtypeStruct((B * S, lw), F32),
        jax.ShapeDtypeStruct((B * S, lw), F32),
    ]
    out_specs = [tok(D), tok(a), tok(a), tok(lw), tok(lw)]
    H = a // V_DIM
    out_shape[2] = jax.ShapeDtypeStruct((B * S * H, V_DIM), F32)
    out_specs[2] = pl.BlockSpec((tm * H, V_DIM), lambda i: (i, 0))
    if transposed:
        out_shape[1] = jax.ShapeDtypeStruct((B, a, S), F32)
        out_specs[1] = pl.BlockSpec((1, a, tm), lambda i: (i // nt, 0, i % nt))
        tshape = jax.ShapeDtypeStruct((B * nt, tm // T, a, T), BF16)
        tspec = pl.BlockSpec((1, tm // T, a, T), lambda i: (i, 0, 0, 0))
        out_shape += [tshape, jax.ShapeDtypeStruct((B * S, a), BF16), tshape]
        out_specs += [tspec, tok(a), tspec]
    else:
        bshape = jax.ShapeDtypeStruct((B * S, a), BF16)
        out_shape += [bshape, bshape, bshape]
        out_specs += [tok(a), tok(a), tok(a)]
    weight_bytes = 2 * (3 * D * d_ff + D * (3 * a + 2 * lw) + a * a)
    act_bytes = 4 * tm * (2 * 2 * D + 2 * 2 * (2 * a + 2 * lw) + D + 2 * (3 * a + 2 * lw) + 4 * FF_CHUNK)
    kern = functools.partial(_stage_a_kernel, attn_width=a, lru_width=lw, transposed=transposed,
                             tile=T, q_scale=q_scale)
    outs = pl.pallas_call(
        kern,
        grid=(B * S // tm,),
        in_specs=[
            tok(D),
            _const_spec((1, D)),
            _const_spec(w["ffn1_gate"].shape),
            _const_spec(w["ffn1_up"].shape),
            _const_spec(w["ffn1_down"].shape),
            _const_spec((1, D)),
            _const_spec(w["w_in"].shape),
            _const_spec((a, a)),
            _const_spec((1, a)),
            _const_spec((1, a)),
        ],
        out_specs=out_specs,
        out_shape=out_shape,
        scratch_shapes=[pltpu.VMEM((tm, D), F32)],
        compiler_params=pltpu.CompilerParams(
            dimension_semantics=("parallel",),
            vmem_limit_bytes=_vmem_limit(weight_bytes + act_bytes)),
        name="stage_a_t" if transposed else "stage_a_n",
    )(x2, w["norm_ffn1"], w["ffn1_gate"], w["ffn1_up"], w["ffn1_down"], w["norm_mix"], w["w_in"],
      w["group_mat"], w["q_gain"], w["k_gain"])
    return outs


def _stage_d_kernel(x1_ref, o_ref, r_ref, woa_ref, wor_ref, g2_ref, wg_ref, wu_ref, wd_ref,
                    y_ref, acc_ref):
    x2 = (x1_ref[...]
          + jnp.dot(o_ref[...], woa_ref[...], preferred_element_type=F32)
          + jnp.dot(r_ref[...], wor_ref[...], preferred_element_type=F32))
    xn = _rms(x2, g2_ref[...]).astype(BF16)
    _swiglu_into(acc_ref, xn, wg_ref, wu_ref, wd_ref)
    y_ref[...] = x2 + 0.5 * acc_ref[...]


def _stage_d(x1, o, r, w):
    N, D = x1.shape
    a = o.shape[1]
    lw = r.shape[1]
    tm = min(TOKEN_TILE, N)
    assert N % tm == 0
    d_ff = w["ffn2_gate"].shape[1]
    tok = lambda width: pl.BlockSpec((tm, width), lambda i: (i, 0))
    weight_bytes = 2 * (3 * D * d_ff + D * D)
    act_bytes = 4 * tm * (2 * 2 * D + 2 * (a + lw) + 2 * D + 4 * FF_CHUNK)
    return pl.pallas_call(
        _stage_d_kernel,
        grid=(N // tm,),
        in_specs=[
            tok(D), tok(a), tok(lw),
            _const_spec((a, D)), _const_spec((lw, D)), _const_spec((1, D)),
            _const_spec(w["ffn2_gate"].shape), _const_spec(w["ffn2_up"].shape),
            _const_spec(w["ffn2_down"].shape),
        ],
        out_specs=tok(D),
        out_shape=jax.ShapeDtypeStruct((N, D), F32),
        scratch_shapes=[pltpu.VMEM((tm, D), F32)],
        compiler_params=pltpu.CompilerParams(
            dimension_semantics=("parallel",),
            vmem_limit_bytes=_vmem_limit(weight_bytes + act_bytes)),
        name="stage_d",
    )(x1, o, r, w["w_out_attn"], w["w_out_lru"], w["norm_ffn2"],
      w["ffn2_gate"], w["ffn2_up"], w["ffn2_down"])


def _t5_bucket(rel):
    n = NUM_BUCKETS // 2
    max_exact = n // 2
    ret = jnp.where(rel > 0, n, 0)
    rel = jnp.abs(rel)
    relf = jnp.maximum(rel, 1).astype(jnp.float32)
    large = max_exact + (jnp.log(relf / max_exact) / math.log(MAX_DISTANCE / max_exact)
                         * (n - max_exact)).astype(jnp.int32)
    large = jnp.minimum(large, n - 1)
    return ret + jnp.where(rel < max_exact, rel, large)


FAR_BUCKET = NUM_BUCKETS // 2 - 1


def _bucket_tile(q_pos, k_pos, keys_first):
    rel = k_pos[None, :] - q_pos[:, None]
    visible = (k_pos[None, :] // CHUNK) <= (q_pos[:, None] // CHUNK)
    idx = jnp.where(visible, _t5_bucket(rel), -1)
    return idx.T if keys_first else idx


def _bias_kernel(table_ref, idx_ref, out_ref, *, scale):
    h = pl.program_id(0)
    idx = idx_ref[0]
    acc = jnp.zeros(idx.shape, F32)
    for b in range(NUM_BUCKETS):
        acc = jnp.where(idx == b, table_ref[b, h], acc)
    out_ref[0, 0] = jnp.where(idx < 0, NEG_INF, (acc - table_ref[FAR_BUCKET, h]) * scale)


def _bias_tiles(rel_bias, idx, scale=1.0):
    n, R, C = idx.shape
    H = rel_bias.shape[1]
    return pl.pallas_call(
        functools.partial(_bias_kernel, scale=scale),
        grid=(H, n),
        in_specs=[pl.BlockSpec(memory_space=pltpu.SMEM),
                  pl.BlockSpec((1, R, C), lambda h, i: (i, 0, 0))],
        out_specs=pl.BlockSpec((1, 1, R, C), lambda h, i: (h, i, 0, 0)),
        out_shape=jax.ShapeDtypeStruct((H, n, R, C), F32),
        name="bias_tiles",
    )(rel_bias, idx)


def _lambda(lq1_ref, lk1_ref, lq2_ref, lk2_ref):
    s1 = jnp.sum(lq1_ref[...] * lk1_ref[...], axis=-1, keepdims=True)
    s2 = jnp.sum(lq2_ref[...] * lk2_ref[...], axis=-1, keepdims=True)
    return jnp.exp(s1) - jnp.exp(s2) + LAM_INIT


def _split_components(qt):
    first = lax.broadcasted_iota(jnp.int32, qt.shape, 0) < HEAD_DIM
    zero = jnp.zeros_like(qt)
    return jnp.concatenate([jnp.where(first, qt, zero), jnp.where(first, zero, qt)], axis=1)


def _attn_finish(o, sub_ref):
    o = o * lax.rsqrt(jnp.mean(o * o, axis=0, keepdims=True) + EPS)
    return o.T * sub_ref[...] * (1.0 - LAM_INIT)


def _prompt_attn_fixed_kernel(qT_ref, k_ref, vT_ref, bias_ref, lq1_ref, lk1_ref, lq2_ref, lk2_ref,
                              sub_ref, o_ref, m_ref, l_ref, acc_ref, *, tile, far_rows):
    G = qT_ref.shape[3]
    T = tile
    sub = T // G
    nq = qT_ref.shape[1] // sub
    lam = _lambda(lq1_ref, lk1_ref, lq2_ref, lk2_ref)

    def lane_tiles(ref, first, count):
        return jnp.concatenate([ref[0, first + j] for j in range(count)], axis=1)

    def probs(first_group, count, q2, shift, bias=None):
        rows = pl.ds(pl.multiple_of(first_group * G, G), count * G)
        s = jnp.dot(k_ref[0, rows, :], q2, preferred_element_type=F32)
        if bias is not None:
            s = s + jnp.concatenate([bias, bias], axis=1)
        p = jnp.exp2(s - shift)
        return (jnp.sum(p, axis=0, keepdims=True),
                jnp.dot(lane_tiles(vT_ref, first_group, count), p.astype(BF16),
                        preferred_element_type=F32))

    def far_groups(first_group, count, q2):
        shift = m_ref[...]
        per = min(far_rows // G, count)
        parts = [probs(first_group + j, per, q2, shift) for j in range(0, count, per)]
        l_ref[...] += functools.reduce(lambda x, y: x + y, [p[0] for p in parts])
        acc_ref[...] += functools.reduce(lambda x, y: x + y, [p[1] for p in parts])

    def q_tile(qi, carry):
        qt = lane_tiles(qT_ref, qi * sub, sub)
        q2 = _split_components(qt)

        own = k_ref[0, pl.ds(pl.multiple_of(qi * T, T), T), :].astype(F32).T * qt.astype(F32)
        shift = jnp.concatenate([jnp.sum(own[:HEAD_DIM], axis=0, keepdims=True),
                                 jnp.sum(own[HEAD_DIM:], axis=0, keepdims=True)], axis=1)

        start = jnp.maximum(qi - 1, 0)
        bias = jnp.concatenate([bias_ref[0, jnp.where(qi > 0, 1, 0)],
                                bias_ref[0, jnp.where(qi > 0, 0, 2)]], axis=0)
        l_near, acc_near = probs(start * sub, 2 * sub, q2, shift, bias)
        m_ref[...] = shift
        l_ref[...] = l_near
        acc_ref[...] = acc_near

        n_far = jnp.maximum(qi - 1, 0)

        def far(i, c):
            far_groups(i * 2 * sub, 2 * sub, q2)
            return c

        lax.fori_loop(0, n_far // 2, far, 0)

        @pl.when(n_far % 2 == 1)
        def _():
            far_groups((n_far - 1) * sub, sub, q2)

        on = acc_ref[...] * (1.0 / l_ref[...])
        o = on[:, :T] - lam * on[:, T:]
        rows = pl.ds(pl.multiple_of(qi * T, T), T)
        o_ref[0, rows, :] = _attn_finish(o, sub_ref).astype(o_ref.dtype)
        return carry

    lax.fori_loop(0, nq, q_tile, 0)


def _prompt_attn_online_kernel(qT_ref, k_ref, vT_ref, bias_ref, lq1_ref, lk1_ref, lq2_ref, lk2_ref,
                               sub_ref, o_ref, m_ref, l_ref, acc_ref, *, tile):
    T = tile
    sub = T // qT_ref.shape[3]
    nq = qT_ref.shape[1] // sub
    lam = _lambda(lq1_ref, lk1_ref, lq2_ref, lk2_ref)

    def lane_tiles(ref, i):
        return jnp.concatenate([ref[0, i * sub + j] for j in range(sub)], axis=1)

    def scores(ki, q2):
        kt = k_ref[0, pl.ds(pl.multiple_of(ki * T, T), T), :]
        return jnp.dot(kt, q2, preferred_element_type=F32)

    def pv(p, ki):
        return jnp.dot(lane_tiles(vT_ref, ki), p.astype(BF16), preferred_element_type=F32)

    def update(s, ki):
        m_prev = m_ref[...]
        m_new = jnp.maximum(m_prev, jnp.max(s, axis=0, keepdims=True))
        alpha = jnp.exp2(m_prev - m_new)
        p = jnp.exp2(s - m_new)
        l_ref[...] = alpha * l_ref[...] + jnp.sum(p, axis=0, keepdims=True)
        acc_ref[...] = alpha * acc_ref[...] + pv(p, ki)
        m_ref[...] = m_new

    def q_tile(qi, carry):
        q2 = _split_components(lane_tiles(qT_ref, qi))

        bias = bias_ref[0, 0]
        s = scores(qi, q2) + jnp.concatenate([bias, bias], axis=1)
        m = jnp.max(s, axis=0, keepdims=True)
        p = jnp.exp2(s - m)
        m_ref[...] = m
        l_ref[...] = jnp.sum(p, axis=0, keepdims=True)
        acc_ref[...] = pv(p, qi)
        kp = jnp.maximum(qi - 1, 0)
        bias = bias_ref[0, jnp.where(qi > 0, 1, 2)]
        s = scores(kp, q2) + jnp.concatenate([bias, bias], axis=1)
        n_far = qi - 1

        update(s, kp)

        def far(ki, s_cur):
            s_next = scores(jnp.minimum(ki + 1, jnp.maximum(n_far - 1, 0)), q2)
            update(s_cur, ki)
            return s_next

        lax.fori_loop(0, n_far, far, scores(0, q2))

        on = acc_ref[...] * (1.0 / l_ref[...])
        o = on[:, :T] - lam * on[:, T:]
        rows = pl.ds(pl.multiple_of(qi * T, T), T)
        o_ref[0, rows, :] = _attn_finish(o, sub_ref).astype(o_ref.dtype)
        return carry

    lax.fori_loop(0, nq, q_tile, 0)


def _near_bias(rel_bias, tile, offsets):
    pos = jnp.arange(tile, dtype=jnp.int32)
    base = tile * max(o for o in offsets if o is not None)
    idx = [jnp.full((tile, tile), -1, jnp.int32) if o is None
           else _bucket_tile(pos + base, pos + base - o * tile, True) for o in offsets]
    return _bias_tiles(rel_bias, jnp.stack(idx), LOG2E)


def _prompt_attention(qT, kb, vT, rel_bias, w, *, fixed_shift):
    T = ATTN_TILE
    B, S = kb.shape[:2]
    G = qT.shape[-1]
    A = kb.shape[-1]
    H = A // V_DIM
    assert S % T == 0 and G % CHUNK == 0 and G >= MAX_DISTANCE and T == 2 * G
    lam_spec = pl.BlockSpec((1, HEAD_DIM), lambda b, h: (0, 0))
    tile_spec = pl.BlockSpec((1, S // G, V_DIM, G), lambda b, h: (b, 0, h, 0))
    seq_spec = pl.BlockSpec((1, S, V_DIM), lambda b, h: (b, 0, h))
    bias = _near_bias(rel_bias, T, (0, 1, None))
    tmp_bytes = 4 * 8 * T * 2 * T
    if fixed_shift:
        kern = functools.partial(_prompt_attn_fixed_kernel, tile=T, far_rows=FAR_ROWS)
    else:
        kern = functools.partial(_prompt_attn_online_kernel, tile=T)
    blk_bytes = 2 * 2 * (3 * S * V_DIM) + 2 * 2 * S * V_DIM + 2 * 4 * bias[0].size
    return pl.pallas_call(
        kern,
        grid=(B, H),
        in_specs=[tile_spec, seq_spec, tile_spec,
                  pl.BlockSpec((1,) + bias.shape[1:], lambda b, h: (h, 0, 0, 0)),
                  lam_spec, lam_spec, lam_spec, lam_spec,
                  pl.BlockSpec((1, V_DIM), lambda b, h: (0, 0))],
        out_specs=seq_spec,
        out_shape=jax.ShapeDtypeStruct((B, S, A), BF16),
        scratch_shapes=[pltpu.VMEM((1, 2 * T), F32), pltpu.VMEM((1, 2 * T), F32),
                        pltpu.VMEM((V_DIM, 2 * T), F32)],
        compiler_params=pltpu.CompilerParams(
            dimension_semantics=("parallel", "parallel"),
            vmem_limit_bytes=_vmem_limit(blk_bytes + tmp_bytes + 4 * V_DIM * 2 * T)),
        name="prompt_attn_fixed" if fixed_shift else "prompt_attn_online",
    )(qT, kb, vT, bias, w["lambda_q1"], w["lambda_k1"], w["lambda_q2"], w["lambda_k2"], w["subln"])


def _sample_attn_kernel(q_ref, kn_ref, vn_ref, ck_ref, cv_ref, bc_ref, bn_ref,
                        lq1_ref, lk1_ref, lq2_ref, lk2_ref, sub_ref, o_ref, *, near):
    Q = q_ref.shape[1]
    P = ck_ref.shape[2]
    H = q_ref.shape[2] // V_DIM
    lam = _lambda(lq1_ref, lk1_ref, lq2_ref, lk2_ref)
    first = lax.broadcasted_iota(jnp.int32, (Q, V_DIM), 1) < HEAD_DIM
    nt = (((1,), (1,)), ((), ()))
    far = P - near
    for h in range(H):
        cols = slice(h * V_DIM, (h + 1) * V_DIM)
        qh = q_ref[0, :, cols]
        zero = jnp.zeros_like(qh)
        q2 = jnp.concatenate([jnp.where(first, qh, zero), jnp.where(first, zero, qh)], axis=0)
        k_far = ck_ref[0, cols, :far].astype(BF16)
        k_near = ck_ref[0, cols, far:].astype(BF16)
        bc = bc_ref[h]
        bn = bn_ref[h]
        s_far = jnp.dot(q2, k_far, preferred_element_type=F32)
        s_near = (jnp.dot(q2, k_near, preferred_element_type=F32)
                  + jnp.concatenate([bc, bc], axis=0))
        s_new = (lax.dot_general(q2, kn_ref[0, :, cols], nt, preferred_element_type=F32)
                 + jnp.concatenate([bn, bn], axis=0))
        m = jnp.maximum(jnp.maximum(jnp.max(s_far, axis=-1, keepdims=True),
                                    jnp.max(s_near, axis=-1, keepdims=True)),
                        jnp.max(s_new, axis=-1, keepdims=True))
        p_far = jnp.exp(s_far - m)
        p_near = jnp.exp(s_near - m)
        p_new = jnp.exp(s_new - m)
        l = (jnp.sum(p_far, axis=-1, keepdims=True) + jnp.sum(p_near, axis=-1, keepdims=True)
             + jnp.sum(p_new, axis=-1, keepdims=True))
        v_far = cv_ref[0, pl.ds(h, far, stride=H), :].astype(BF16)
        v_near = cv_ref[0, pl.ds(far * H + h, near, stride=H), :].astype(BF16)
        acc = (jnp.dot(p_far.astype(BF16), v_far, preferred_element_type=F32)
               + jnp.dot(p_near.astype(BF16), v_near, preferred_element_type=F32)
               + jnp.dot(p_new.astype(BF16), vn_ref[0, :, cols], preferred_element_type=F32))
        on = acc * (1.0 / l)
        o = on[:Q] - lam * on[Q:]
        o = o * lax.rsqrt(jnp.mean(o * o, axis=-1, keepdims=True) + EPS)
        o_ref[0, :, cols] = (o * sub_ref[...] * (1.0 - LAM_INIT)).astype(o_ref.dtype)


def _sample_attention(qb, kb, vb, cache_k, cache_v, bias_c, bias_n, w):
    B, Q, A = qb.shape
    P = cache_k.shape[2]
    near = bias_c.shape[-1]
    new_spec = pl.BlockSpec((1, Q, A), lambda b: (b, 0, 0))
    ck_spec = pl.BlockSpec((1, A, P), lambda b: (b, 0, 0))
    cv_spec = pl.BlockSpec((1,) + cache_v.shape[1:], lambda b: (b, 0, 0))
    lam_spec = pl.BlockSpec((1, HEAD_DIM), lambda b: (0, 0))
    blk_bytes = 2 * 2 * 4 * P * A + 2 * 4 * 2 * Q * A
    tmp_bytes = 4 * 2 * Q * (3 * P + 2 * P) + 2 * 2 * P * V_DIM
    return pl.pallas_call(
        functools.partial(_sample_attn_kernel, near=near),
        grid=(B,),
        in_specs=[new_spec, new_spec, new_spec, ck_spec, cv_spec,
                  pl.BlockSpec(bias_c.shape, lambda b: (0, 0, 0)),
                  pl.BlockSpec(bias_n.shape, lambda b: (0, 0, 0)),
                  lam_spec, lam_spec, lam_spec, lam_spec,
                  pl.BlockSpec((1, V_DIM), lambda b: (0, 0))],
        out_specs=new_spec,
        out_shape=jax.ShapeDtypeStruct((B, Q, A), BF16),
        compiler_params=pltpu.CompilerParams(
            dimension_semantics=("parallel",),
            vmem_limit_bytes=_vmem_limit(blk_bytes + tmp_bytes)),
        name="sample_attn",
    )(qb, kb, vb, cache_k, cache_v, bias_c, bias_n,
      w["lambda_q1"], w["lambda_k1"], w["lambda_q2"], w["lambda_k2"], w["subln"])


def _gelu_tanh(x):
    return 0.5 * x * (1.0 + jnp.tanh(math.sqrt(2.0 / math.pi) * (x + 0.044715 * (x * x * x))))


def _lru_kernel(xr_ref, xg_ref, h0_ref, c0_ref, cw_ref, cb_ref, wgate_ref, bgate_ref, L_ref, gn_ref,
                r_ref, hl_ref, cn_ref, xbuf_ref, a_ref, b_ref, h_ref, hc_ref):
    t = pl.program_id(1)
    ts, W = xr_ref.shape[1:]
    tail = CONV_WIDTH - 1
    base = SUBLANES

    @pl.when(t == 0)
    def _():
        xbuf_ref[base - tail:base, :] = c0_ref[0]
        hc_ref[...] = jnp.broadcast_to(h0_ref[0], hc_ref.shape)

    xbuf_ref[base:base + ts, :] = xr_ref[0]
    xc = cb_ref[...]
    for j in range(CONV_WIDTH):
        xc = xc + xbuf_ref[base - tail + j:base - tail + j + ts, :] * cw_ref[j:j + 1, :]
    xbuf_ref[base - tail:base, :] = xbuf_ref[base + ts - tail:base + ts, :]

    gates = jnp.dot(xc.astype(BF16), wgate_ref[...], preferred_element_type=F32) + bgate_ref[...]
    r = _sigmoid(gates[:, :W])
    i = _sigmoid(gates[:, W:])
    z = -L_ref[...]
    softplus = jnp.maximum(z, 0.0) + jnp.log1p(jnp.exp(-jnp.abs(z)))
    log_a = -LRU_C * r * softplus
    a_ref[...] = jnp.exp(log_a)
    th = jnp.tanh(log_a)
    b_ref[...] = jnp.sqrt(-2.0 * th / (1.0 - th)) * (i * xc)

    row = lax.broadcasted_iota(jnp.int32, (SUBLANES, W), 0)

    def group(g, hprev):
        rows = pl.ds(pl.multiple_of(g * SUBLANES, SUBLANES), SUBLANES)
        A = a_ref[rows, :]
        Bv = b_ref[rows, :]
        d = 1
        while d < SUBLANES:
            keep = row >= d
            A_sh = jnp.where(keep, pltpu.roll(A, d, 0), 1.0)
            B_sh = jnp.where(keep, pltpu.roll(Bv, d, 0), 0.0)
            Bv = A * B_sh + Bv
            A = A * A_sh
            d *= 2
        Hg = A * hprev + Bv
        h_ref[rows, :] = Hg
        return jnp.broadcast_to(Hg[SUBLANES - 1:SUBLANES, :], hprev.shape)

    hc_ref[...] = lax.fori_loop(0, ts // SUBLANES, group, hc_ref[...])

    out = h_ref[...] * _gelu_tanh(xg_ref[0])
    r_ref[0] = _rms(out, gn_ref[...]).astype(r_ref.dtype)

    @pl.when(t == pl.num_programs(1) - 1)
    def _():
        hl_ref[0] = hc_ref[0:1, :]
        cn_ref[0] = xbuf_ref[base - tail:base, :]


def _lru(xr, xg, h0, conv0, w):
    B, S, W = xr.shape
    ts = min(LRU_TILE, S)
    assert S % ts == 0 and ts % SUBLANES == 0 and S >= CONV_WIDTH - 1
    tail = CONV_WIDTH - 1
    seq = pl.BlockSpec((1, ts, W), lambda b, t: (b, t, 0))
    per_b = lambda r: pl.BlockSpec((1, r, W), lambda b, t: (b, 0, 0))
    const = lambda shape: pl.BlockSpec(shape, lambda b, t: (0, 0))
    return pl.pallas_call(
        _lru_kernel,
        grid=(B, S // ts),
        in_specs=[seq, seq, per_b(1), per_b(tail),
                  const((CONV_WIDTH, W)), const((1, W)), const((W, 2 * W)), const((1, 2 * W)),
                  const((1, W)), const((1, W))],
        out_specs=[seq, per_b(1), per_b(tail)],
        out_shape=[jax.ShapeDtypeStruct((B, S, W), BF16),
                   jax.ShapeDtypeStruct((B, 1, W), F32),
                   jax.ShapeDtypeStruct((B, tail, W), F32)],
        scratch_shapes=[pltpu.VMEM((ts + SUBLANES, W), F32), pltpu.VMEM((ts, W), F32),
                        pltpu.VMEM((ts, W), F32), pltpu.VMEM((ts, W), F32),
                        pltpu.VMEM((SUBLANES, W), F32)],
        compiler_params=pltpu.CompilerParams(dimension_semantics=("parallel", "arbitrary")),
        name="rg_lru",
    )(xr, xg, h0.reshape(B, 1, W), conv0, w["conv_w"], w["conv_b"], w["w_gate"], w["b_gate"],
      w["lru_L"], w["lru_out_norm"])


def _block_diag(wb):
    n, c, d = wb.shape
    eye = jnp.eye(n, dtype=wb.dtype)
    return (eye[:, None, :, None] * wb[:, :, None, :]).reshape(n * c, n * d)


def _prepare_weights(l, p):
    D, d_ff = p["ffn1_gate"].shape[1:]
    assert d_ff % FF_CHUNK == 0
    lw = p["conv_w"].shape[-1]
    in_width = p["w_in"].shape[-1]
    a = (in_width - 2 * lw) // 3
    assert a % V_DIM == 0

    group = np.arange(a) // HEAD_DIM
    w = dict(attn_width=a, lru_width=lw)
    for name in ("ffn1_gate", "ffn1_up", "ffn1_down", "ffn2_gate", "ffn2_up", "ffn2_down"):
        w[name] = p[name][l].astype(BF16)
    for name in ("norm_ffn1", "norm_mix", "norm_ffn2", "lru_out_norm", "conv_b", "lru_L", "subln",
                 "lambda_q1", "lambda_k1", "lambda_q2", "lambda_k2"):
        w[name] = p[name][l][None, :]
    w["w_in"] = p["w_in"][l].astype(BF16)
    w["group_mat"] = jnp.asarray((group[:, None] == group[None, :]) / HEAD_DIM, BF16)
    w["q_gain"] = jnp.tile(p["q_norm"][l], a // HEAD_DIM)[None, :]
    w["k_gain"] = jnp.tile(p["k_norm"][l], a // HEAD_DIM)[None, :]
    w["w_out_attn"] = p["w_out"][l][:a].astype(BF16)
    w["w_out_lru"] = p["w_out"][l][a:].astype(BF16)
    w["conv_w"] = p["conv_w"][l]
    w["w_gate"] = jnp.concatenate([_block_diag(p["gate_a_w"][l]), _block_diag(p["gate_x_w"][l])],
                                  axis=1).astype(BF16)
    w["b_gate"] = jnp.concatenate([p["gate_a_b"][l], p["gate_x_b"][l]])[None, :]
    return w


def _layer_prompt(x, w, rel_bias):
    B, S, D = x.shape
    T = ATTN_TILE
    a, lw = w["attn_width"], w["lru_width"]
    H = a // V_DIM
    x1, kT, v, xr, xg, qT, kb, vT = _stage_a(x, w, transposed=True)
    k = kT.reshape(B, H, 2, HEAD_DIM, S).transpose(0, 4, 1, 2, 3)
    nl = S // LAYOUT_TILE
    score_bound = (HEAD_DIM ** 0.5 * LOG2E * (1 + 2.0 ** -7)
                   * jnp.max(jnp.abs(w["q_gain"])) * jnp.max(jnp.abs(w["k_gain"])))
    bias_bound = 2 * LOG2E * jnp.max(jnp.abs(rel_bias))
    attend = lambda fixed: functools.partial(_prompt_attention, w=w, fixed_shift=fixed)
    o = lax.cond(2 * (score_bound + bias_bound) <= SHIFT_RANGE, attend(True), attend(False),
                 qT.reshape(B, nl, a, LAYOUT_TILE), kb.reshape(B, S, a),
                 vT.reshape(B, nl, a, LAYOUT_TILE), rel_bias)
    r, h_last, conv_new = _lru(xr.reshape(B, S, lw), xg.reshape(B, S, lw),
                               jnp.zeros((B, lw), F32), jnp.zeros((B, CONV_WIDTH - 1, lw), F32), w)
    y = _stage_d(x1, o.reshape(B * S, a), r.reshape(B * S, lw), w)
    return (y.reshape(B, S, D), k, v.reshape(B, S, H, V_DIM), h_last.reshape(B, lw), conv_new)


def _layer_sample(x, w, rel_bias, k_past, v_past, h0, conv0):
    B, Q, D = x.shape
    P = k_past.shape[1]
    a, lw = w["attn_width"], w["lru_width"]
    H = a // V_DIM
    near = LANES
    assert P % CHUNK == 0 and Q <= CHUNK and near >= MAX_DISTANCE and P > near
    x1, k, v, xr, xg, qb, kb, vb = _stage_a(x, w, transposed=False)
    q_pos = P + jnp.arange(Q, dtype=jnp.int32)
    idx_c = _bucket_tile(q_pos, jnp.arange(P - near, P, dtype=jnp.int32), False)
    idx_n = _bucket_tile(q_pos, q_pos, False)
    bias_c = _bias_tiles(rel_bias, idx_c[None])[:, 0]
    bias_n = _bias_tiles(rel_bias, idx_n[None])[:, 0]
    o = _sample_attention(qb.reshape(B, Q, a), kb.reshape(B, Q, a), vb.reshape(B, Q, a),
                          k_past.transpose(0, 2, 3, 4, 1).reshape(B, a, P),
                          v_past.reshape(B, P * H, V_DIM),
                          bias_c, bias_n, w)
    r, h_last, conv_new = _lru(xr.reshape(B, Q, lw), xg.reshape(B, Q, lw), h0, conv0, w)
    y = _stage_d(x1, o.reshape(B * Q, a), r.reshape(B * Q, lw), w)
    return (y.reshape(B, Q, D), k.reshape(B, Q, H, 2, HEAD_DIM), v.reshape(B, Q, H, V_DIM),
            h_last.reshape(B, lw), conv_new)


def kernel(x_prompt, x_sample, cache_k, cache_v, state_lru, state_conv, rel_bias, norm_ffn1, ffn1_gate, ffn1_up, ffn1_down, norm_mix, w_in, q_norm, k_norm, lambda_q1, lambda_k1, lambda_q2, lambda_k2, subln, conv_w, conv_b, gate_a_w, gate_a_b, gate_x_w, gate_x_b, lru_L, lru_out_norm, w_out, norm_ffn2, ffn2_gate, ffn2_up, ffn2_down):
    p = dict(norm_ffn1=norm_ffn1, ffn1_gate=ffn1_gate, ffn1_up=ffn1_up, ffn1_down=ffn1_down,
             norm_mix=norm_mix, w_in=w_in, q_norm=q_norm, k_norm=k_norm,
             lambda_q1=lambda_q1, lambda_k1=lambda_k1, lambda_q2=lambda_q2, lambda_k2=lambda_k2,
             subln=subln, conv_w=conv_w, conv_b=conv_b, gate_a_w=gate_a_w, gate_a_b=gate_a_b,
             gate_x_w=gate_x_w, gate_x_b=gate_x_b, lru_L=lru_L, lru_out_norm=lru_out_norm,
             w_out=w_out, norm_ffn2=norm_ffn2, ffn2_gate=ffn2_gate, ffn2_up=ffn2_up,
             ffn2_down=ffn2_down)
    depth = cache_k.shape[0]
    assert depth == 1, "the lambda initial value is specialised to a single layer"
    w = _prepare_weights(0, p)
    yp, kp, vp, hp, cp = _layer_prompt(x_prompt, w, rel_bias)
    ys, kn, vn, hn, cn = _layer_sample(x_sample, w, rel_bias, cache_k[0], cache_v[0],
                                       state_lru[0], state_conv[0])
    stack = lambda t: t[None]
    return (yp, ys, stack(kp), stack(vp), stack(hp), stack(cp),
            stack(kn), stack(vn), stack(hn), stack(cn))
```

```python
import functools
import math

import numpy as np
import jax
import jax.numpy as jnp
from jax import lax
from jax.experimental import pallas as pl
from jax.experimental.pallas import tpu as pltpu

F32 = jnp.float32
BF16 = jnp.bfloat16

HEAD_DIM = 64
V_DIM = 2 * HEAD_DIM
CHUNK = 64
NUM_BUCKETS = 32
MAX_DISTANCE = 128
CONV_WIDTH = 4
LRU_BLOCKS = 8
LRU_C = 8.0
EPS = 1e-6
NEG_INF = -1e30
LAM_INIT = 0.8 - 0.6 * math.exp(-0.3 * 0)

LANES = 128
SUBLANES = 8
MXU_DIM = 256
VMEM_BYTES = 64 * 1024 * 1024

FF_CHUNK = MXU_DIM
ATTN_TILE = 512
LAYOUT_TILE = 256
FAR_ROWS = 512
FAR_TILES = 4
LOG2E = 1.4426950408889634
SHIFT_RANGE = 100.0
TOKEN_TILE = 512
ROW_GROUPS = 1
LRU_TILE = 256


def _vmem_limit(nbytes):
    return int(min(VMEM_BYTES - 4 * 1024 * 1024, max(nbytes, 16 * 1024 * 1024)))


def _rms(x, g):
    ms = jnp.mean(x * x, axis=-1, keepdims=True)
    return x * lax.rsqrt(ms + EPS) * g


def _sigmoid(x):
    return 1.0 / (1.0 + jnp.exp(-x))


def _const_spec(shape):
    n = len(shape)
    return pl.BlockSpec(shape, lambda *_: (0,) * n, pipeline_mode=pl.Buffered(1))


def _swiglu_into(acc_ref, xn, wg_ref, wu_ref, wd_ref):
    d_ff = wg_ref.shape[1]
    for c in range(0, d_ff, FF_CHUNK):
        cols = slice(c, c + FF_CHUNK)
        g = jnp.dot(xn, wg_ref[:, cols], preferred_element_type=F32)
        u = jnp.dot(xn, wu_ref[:, cols], preferred_element_type=F32)
        h = (g * _sigmoid(g) * u).astype(BF16)
        part = jnp.dot(h, wd_ref[cols, :], preferred_element_type=F32)
        if c == 0:
            acc_ref[...] = part
        else:
            acc_ref[...] += part


def _stage_a_kernel(x_ref, g1_ref, wg_ref, wu_ref, wd_ref, gm_ref, win_ref, gmat_ref, qg_ref, kg_ref,
                    x1_ref, k_ref, v_ref, xr_ref, xg_ref, qb_ref, kb_ref, vb_ref, acc_ref,
                    *, attn_width, lru_width, transposed, tile, q_scale):
    x = x_ref[...]
    xn = _rms(x, g1_ref[...]).astype(BF16)
    _swiglu_into(acc_ref, xn, wg_ref, wu_ref, wd_ref)
    x1 = x + 0.5 * acc_ref[...]
    x1_ref[...] = x1
    hn = _rms(x1, gm_ref[...]).astype(BF16)
    proj = jnp.dot(hn, win_ref[...], preferred_element_type=F32)
    a = attn_width
    q = proj[:, :a]
    k = proj[:, a:2 * a]
    v = proj[:, 2 * a:3 * a]
    xr_ref[...] = proj[:, 3 * a:3 * a + lru_width]
    xg_ref[...] = proj[:, 3 * a + lru_width:]

    def group_norm(t, g):
        ms = jnp.dot((t * t).astype(BF16), gmat_ref[...], preferred_element_type=F32)
        return t * lax.rsqrt(ms + EPS) * g

    qn = group_norm(q, qg_ref[...]) * q_scale
    kn = group_norm(k, kg_ref[...])
    n_heads = a // V_DIM
    for h in range(n_heads):
        v_ref[pl.ds(h, x.shape[0], stride=n_heads), :] = v[:, h * V_DIM:(h + 1) * V_DIM]
    kb_ref[...] = kn.astype(BF16)
    if transposed:
        k_ref[0] = kn.T
        for j in range(x.shape[0] // tile):
            rows = slice(j * tile, (j + 1) * tile)
            qb_ref[0, j] = qn[rows, :].T.astype(BF16)
            vb_ref[0, j] = v[rows, :].T.astype(BF16)
    else:
        k_ref[...] = kn
        qb_ref[...] = qn.astype(BF16)
        vb_ref[...] = v.astype(BF16)


def _stage_a(x, w, *, transposed):
    B, S, D = x.shape
    a = w["attn_width"]
    lw = w["lru_width"]
    tm = TOKEN_TILE
    T = LAYOUT_TILE
    q_scale = HEAD_DIM ** -0.5 * (LOG2E if transposed else 1.0)
    assert (B * S) % tm == 0 and (not transposed or (tm % T == 0 and S % tm == 0))
    nt = S // tm
    x2 = x.reshape(B * S, D)
    d_ff = w["ffn1_gate"].shape[1]
    tok = lambda width: pl.BlockSpec((tm, width), lambda i: (i, 0))
    out_shape = [
        jax.ShapeDtypeStruct((B * S, D), F32),
        jax.ShapeDtypeStruct((B * S, a), F32),
        jax.ShapeDtypeStruct((B * S, a), F32),
        jax.ShapeDtypeStruct((B * S, lw), F32),
        jax.ShapeDtypeStruct((B * S, lw), F32),
    ]
    out_specs = [tok(D), tok(a), tok(a), tok(lw), tok(lw)]
    H = a // V_DIM
    out_shape[2] = jax.ShapeDtypeStruct((B * S * H, V_DIM), F32)
    out_specs[2] = pl.BlockSpec((tm * H, V_DIM), lambda i: (i, 0))
    if transposed:
        out_shape[1] = jax.ShapeDtypeStruct((B, a, S), F32)
        out_specs[1] = pl.BlockSpec((1, a, tm), lambda i: (i // nt, 0, i % nt))
        tshape = jax.ShapeDtypeStruct((B * nt, tm // T, a, T), BF16)
        tspec = pl.BlockSpec((1, tm // T, a, T), lambda i: (i, 0, 0, 0))
        out_shape += [tshape, jax.ShapeDtypeStruct((B * S, a), BF16), tshape]
        out_specs += [tspec, tok(a), tspec]
    else:
        bshape = jax.ShapeDtypeStruct((B * S, a), BF16)
        out_shape += [bshape, bshape, bshape]
        out_specs += [tok(a), tok(a), tok(a)]
    weight_bytes = 2 * (3 * D * d_ff + D * (3 * a + 2 * lw) + a * a)
    act_bytes = 4 * tm * (2 * 2 * D + 2 * 2 * (2 * a + 2 * lw) + D + 2 * (3 * a + 2 * lw) + 4 * FF_CHUNK)
    kern = functools.partial(_stage_a_kernel, attn_width=a, lru_width=lw, transposed=transposed,
                             tile=T, q_scale=q_scale)
    outs = pl.pallas_call(
        kern,
        grid=(B * S // tm,),
        in_specs=[
            tok(D),
            _const_spec((1, D)),
            _const_spec(w["ffn1_gate"].shape),
            _const_spec(w["ffn1_up"].shape),
            _const_spec(w["ffn1_down"].shape),
            _const_spec((1, D)),
            _const_spec(w["w_in"].shape),
            _const_spec((a, a)),
            _const_spec((1, a)),
            _const_spec((1, a)),
        ],
        out_specs=out_specs,
        out_shape=out_shape,
        scratch_shapes=[pltpu.VMEM((tm, D), F32)],
        compiler_params=pltpu.CompilerParams(
            dimension_semantics=("parallel",),
            vmem_limit_bytes=_vmem_limit(weight_bytes + act_bytes)),
        name="stage_a_t" if transposed else "stage_a_n",
    )(x2, w["norm_ffn1"], w["ffn1_gate"], w["ffn1_up"], w["ffn1_down"], w["norm_mix"], w["w_in"],
      w["group_mat"], w["q_gain"], w["k_gain"])
    return outs


def _stage_d_kernel(x1_ref, o_ref, r_ref, woa_ref, wor_ref, g2_ref, wg_ref, wu_ref, wd_ref,
                    y_ref, acc_ref, *, sub_rows):
    for r0 in range(0, x1_ref.shape[0], sub_rows):
        rows = slice(r0, r0 + sub_rows)
        x2 = (x1_ref[rows, :]
              + jnp.dot(o_ref[rows, :], woa_ref[...], preferred_element_type=F32)
              + jnp.dot(r_ref[rows, :], wor_ref[...], preferred_element_type=F32))
        xn = _rms(x2, g2_ref[...]).astype(BF16)
        acc = acc_ref.at[rows]
        _swiglu_into(acc, xn, wg_ref, wu_ref, wd_ref)
        y_ref[rows, :] = x2 + 0.5 * acc[...]


def _stage_d(x1, o, r, w):
    N, D = x1.shape
    a = o.shape[1]
    lw = r.shape[1]
    tm = min(TOKEN_TILE * ROW_GROUPS, N)
    assert N % tm == 0 and tm % TOKEN_TILE == 0
    d_ff = w["ffn2_gate"].shape[1]
    tok = lambda width: pl.BlockSpec((tm, width), lambda i: (i, 0))
    weight_bytes = 2 * (3 * D * d_ff + D * D)
    act_bytes = 4 * tm * (2 * 2 * D + 2 * (a + lw) + 2 * D + 4 * FF_CHUNK)
    return pl.pallas_call(
        functools.partial(_stage_d_kernel, sub_rows=TOKEN_TILE),
        grid=(N // tm,),
        in_specs=[
            tok(D), tok(a), tok(lw),
            _const_spec((a, D)), _const_spec((lw, D)), _const_spec((1, D)),
            _const_spec(w["ffn2_gate"].shape), _const_spec(w["ffn2_up"].shape),
            _const_spec(w["ffn2_down"].shape),
        ],
        out_specs=tok(D),
        out_shape=jax.ShapeDtypeStruct((N, D), F32),
        scratch_shapes=[pltpu.VMEM((tm, D), F32)],
        compiler_params=pltpu.CompilerParams(
            dimension_semantics=("parallel",),
            vmem_limit_bytes=_vmem_limit(weight_bytes + act_bytes)),
        name="stage_d",
    )(x1, o, r, w["w_out_attn"], w["w_out_lru"], w["norm_ffn2"],
      w["ffn2_gate"], w["ffn2_up"], w["ffn2_down"])


def _t5_bucket(rel):
    n = NUM_BUCKETS // 2
    max_exact = n // 2
    ret = jnp.where(rel > 0, n, 0)
    rel = jnp.abs(rel)
    relf = jnp.maximum(rel, 1).astype(jnp.float32)
    large = max_exact + (jnp.log(relf / max_exact) / math.log(MAX_DISTANCE / max_exact)
                         * (n - max_exact)).astype(jnp.int32)
    large = jnp.minimum(large, n - 1)
    return ret + jnp.where(rel < max_exact, rel, large)


FAR_BUCKET = NUM_BUCKETS // 2 - 1


def _bucket_tile(q_pos, k_pos, keys_first):
    rel = k_pos[None, :] - q_pos[:, None]
    visible = (k_pos[None, :] // CHUNK) <= (q_pos[:, None] // CHUNK)
    idx = jnp.where(visible, _t5_bucket(rel), -1)
    return idx.T if keys_first else idx


def _bias_kernel(table_ref, idx_ref, out_ref, *, scale):
    h = pl.program_id(0)
    idx = idx_ref[0]
    acc = jnp.zeros(idx.shape, F32)
    for b in range(NUM_BUCKETS):
        acc = jnp.where(idx == b, table_ref[b, h], acc)
    out_ref[0, 0] = jnp.where(idx < 0, NEG_INF, (acc - table_ref[FAR_BUCKET, h]) * scale)


def _bias_tiles(rel_bias, idx, scale=1.0):
    n, R, C = idx.shape
    H = rel_bias.shape[1]
    return pl.pallas_call(
        functools.partial(_bias_kernel, scale=scale),
        grid=(H, n),
        in_specs=[pl.BlockSpec(memory_space=pltpu.SMEM),
                  pl.BlockSpec((1, R, C), lambda h, i: (i, 0, 0))],
        out_specs=pl.BlockSpec((1, 1, R, C), lambda h, i: (h, i, 0, 0)),
        out_shape=jax.ShapeDtypeStruct((H, n, R, C), F32),
        name="bias_tiles",
    )(rel_bias, idx)


def _lambda(lq1_ref, lk1_ref, lq2_ref, lk2_ref):
    s1 = jnp.sum(lq1_ref[...] * lk1_ref[...], axis=-1, keepdims=True)
    s2 = jnp.sum(lq2_ref[...] * lk2_ref[...], axis=-1, keepdims=True)
    return jnp.exp(s1) - jnp.exp(s2) + LAM_INIT


def _split_components(qt):
    first = lax.broadcasted_iota(jnp.int32, qt.shape, 0) < HEAD_DIM
    zero = jnp.zeros_like(qt)
    return jnp.concatenate([jnp.where(first, qt, zero), jnp.where(first, zero, qt)], axis=1)


def _attn_finish(o, sub_ref):
    o = o * lax.rsqrt(jnp.mean(o * o, axis=0, keepdims=True) + EPS)
    return o.T * sub_ref[...] * (1.0 - LAM_INIT)


def _prompt_attn_fixed_kernel(qT_ref, k_ref, vT_ref, bias_ref, lq1_ref, lk1_ref, lq2_ref, lk2_ref,
                              sub_ref, o_ref, m_ref, l_ref, acc_ref, *, tile, far_rows):
    G = qT_ref.shape[3]
    T = tile
    sub = T // G
    nq = qT_ref.shape[1] // sub
    lam = _lambda(lq1_ref, lk1_ref, lq2_ref, lk2_ref)

    def lane_tiles(ref, first, count):
        return jnp.concatenate([ref[0, first + j] for j in range(count)], axis=1)

    def probs(first_group, count, q2, shift, bias=None):
        rows = pl.ds(pl.multiple_of(first_group * G, G), count * G)
        s = jnp.dot(k_ref[0, rows, :], q2, preferred_element_type=F32)
        if bias is not None:
            s = s + jnp.concatenate([bias, bias], axis=1)
        p = jnp.exp2(s - shift)
        return (jnp.sum(p, axis=0, keepdims=True),
                jnp.dot(lane_tiles(vT_ref, first_group, count), p.astype(BF16),
                        preferred_element_type=F32))

    def far_groups(first_group, count, q2):
        shift = m_ref[...]
        per = min(far_rows // G, count)
        parts = [probs(first_group + j, per, q2, shift) for j in range(0, count, per)]
        l_ref[...] += functools.reduce(lambda x, y: x + y, [p[0] for p in parts])
        acc_ref[...] += functools.reduce(lambda x, y: x + y, [p[1] for p in parts])

    def q_tile(qi, carry):
        qt = lane_tiles(qT_ref, qi * sub, sub)
        q2 = _split_components(qt)

        own = k_ref[0, pl.ds(pl.multiple_of(qi * T, T), T), :].astype(F32).T * qt.astype(F32)
        shift = jnp.concatenate([jnp.sum(own[:HEAD_DIM], axis=0, keepdims=True),
                                 jnp.sum(own[HEAD_DIM:], axis=0, keepdims=True)], axis=1)

        start = jnp.maximum(qi - 1, 0)
        bias = jnp.concatenate([bias_ref[0, jnp.where(qi > 0, 1, 0)],
                                bias_ref[0, jnp.where(qi > 0, 0, 2)]], axis=0)
        l_near, acc_near = probs(start * sub, 2 * sub, q2, shift, bias)
        m_ref[...] = shift
        l_ref[...] = l_near
        acc_ref[...] = acc_near

        n_far = jnp.maximum(qi - 1, 0)

        def far(i, c):
            far_groups(i * FAR_TILES * sub, FAR_TILES * sub, q2)
            return c

        lax.fori_loop(0, n_far // FAR_TILES, far, 0)
        done = n_far - n_far % FAR_TILES
        chunk = FAR_TILES // 2
        while chunk >= 1:
            take = (n_far - done) >= chunk

            @pl.when(take)
            def _(done=done, chunk=chunk):
                far_groups(done * sub, chunk * sub, q2)

            done = done + jnp.where(take, chunk, 0)
            chunk //= 2

        on = acc_ref[...] * (1.0 / l_ref[...])
        o = on[:, :T] - lam * on[:, T:]
        rows = pl.ds(pl.multiple_of(qi * T, T), T)
        o_ref[0, rows, :] = _attn_finish(o, sub_ref).astype(o_ref.dtype)
        return carry

    lax.fori_loop(0, nq, q_tile, 0)


def _prompt_attn_online_kernel(qT_ref, k_ref, vT_ref, bias_ref, lq1_ref, lk1_ref, lq2_ref, lk2_ref,
                               sub_ref, o_ref, m_ref, l_ref, acc_ref, *, tile):
    T = tile
    sub = T // qT_ref.shape[3]
    nq = qT_ref.shape[1] // sub
    lam = _lambda(lq1_ref, lk1_ref, lq2_ref, lk2_ref)

    def lane_tiles(ref, i):
        return jnp.concatenate([ref[0, i * sub + j] for j in range(sub)], axis=1)

    def scores(ki, q2):
        kt = k_ref[0, pl.ds(pl.multiple_of(ki * T, T), T), :]
        return jnp.dot(kt, q2, preferred_element_type=F32)

    def pv(p, ki):
        return jnp.dot(lane_tiles(vT_ref, ki), p.astype(BF16), preferred_element_type=F32)

    def update(s, ki):
        m_prev = m_ref[...]
        m_new = jnp.maximum(m_prev, jnp.max(s, axis=0, keepdims=True))
        alpha = jnp.exp2(m_prev - m_new)
        p = jnp.exp2(s - m_new)
        l_ref[...] = alpha * l_ref[...] + jnp.sum(p, axis=0, keepdims=True)
        acc_ref[...] = alpha * acc_ref[...] + pv(p, ki)
        m_ref[...] = m_new

    def q_tile(qi, carry):
        q2 = _split_components(lane_tiles(qT_ref, qi))

        bias = bias_ref[0, 0]
        s = scores(qi, q2) + jnp.concatenate([bias, bias], axis=1)
        m = jnp.max(s, axis=0, keepdims=True)
        p = jnp.exp2(s - m)
        m_ref[...] = m
        l_ref[...] = jnp.sum(p, axis=0, keepdims=True)
        acc_ref[...] = pv(p, qi)
        kp = jnp.maximum(qi - 1, 0)
        bias = bias_ref[0, jnp.where(qi > 0, 1, 2)]
        s = scores(kp, q2) + jnp.concatenate([bias, bias], axis=1)
        n_far = qi - 1

        update(s, kp)

        def far(ki, s_cur):
            s_next = scores(jnp.minimum(ki + 1, jnp.maximum(n_far - 1, 0)), q2)
            update(s_cur, ki)
            return s_next

        lax.fori_loop(0, n_far, far, scores(0, q2))

        on = acc_ref[...] * (1.0 / l_ref[...])
        o = on[:, :T] - lam * on[:, T:]
        rows = pl.ds(pl.multiple_of(qi * T, T), T)
        o_ref[0, rows, :] = _attn_finish(o, sub_ref).astype(o_ref.dtype)
        return carry

    lax.fori_loop(0, nq, q_tile, 0)


def _near_bias(rel_bias, tile, offsets):
    pos = jnp.arange(tile, dtype=jnp.int32)
    base = tile * max(o for o in offsets if o is not None)
    idx = [jnp.full((tile, tile), -1, jnp.int32) if o is None
           else _bucket_tile(pos + base, pos + base - o * tile, True) for o in offsets]
    return _bias_tiles(rel_bias, jnp.stack(idx), LOG2E)


def _prompt_attention(qT, kb, vT, rel_bias, w, *, fixed_shift):
    T = ATTN_TILE
    B, S = kb.shape[:2]
    G = qT.shape[-1]
    A = kb.shape[-1]
    H = A // V_DIM
    assert S % T == 0 and G % CHUNK == 0 and G >= MAX_DISTANCE and T == 2 * G
    lam_spec = pl.BlockSpec((1, HEAD_DIM), lambda b, h: (0, 0))
    tile_spec = pl.BlockSpec((1, S // G, V_DIM, G), lambda b, h: (b, 0, h, 0))
    seq_spec = pl.BlockSpec((1, S, V_DIM), lambda b, h: (b, 0, h))
    bias = _near_bias(rel_bias, T, (0, 1, None))
    tmp_bytes = 4 * 8 * T * 2 * T
    if fixed_shift:
        kern = functools.partial(_prompt_attn_fixed_kernel, tile=T, far_rows=FAR_ROWS)
    else:
        kern = functools.partial(_prompt_attn_online_kernel, tile=T)
    blk_bytes = 2 * 2 * (3 * S * V_DIM) + 2 * 2 * S * V_DIM + 2 * 4 * bias[0].size
    return pl.pallas_call(
        kern,
        grid=(B, H),
        in_specs=[tile_spec, seq_spec, tile_spec,
                  pl.BlockSpec((1,) + bias.shape[1:], lambda b, h: (h, 0, 0, 0)),
                  lam_spec, lam_spec, lam_spec, lam_spec,
                  pl.BlockSpec((1, V_DIM), lambda b, h: (0, 0))],
        out_specs=seq_spec,
        out_shape=jax.ShapeDtypeStruct((B, S, A), BF16),
        scratch_shapes=[pltpu.VMEM((1, 2 * T), F32), pltpu.VMEM((1, 2 * T), F32),
                        pltpu.VMEM((V_DIM, 2 * T), F32)],
        compiler_params=pltpu.CompilerParams(
            dimension_semantics=("parallel", "parallel"),
            vmem_limit_bytes=_vmem_limit(blk_bytes + tmp_bytes + 4 * V_DIM * 2 * T)),
        name="prompt_attn_fixed" if fixed_shift else "prompt_attn_online",
    )(qT, kb, vT, bias, w["lambda_q1"], w["lambda_k1"], w["lambda_q2"], w["lambda_k2"], w["subln"])


def _sample_attn_kernel(q_ref, kn_ref, vn_ref, ck_ref, cv_ref, bc_ref, bn_ref,
                        lq1_ref, lk1_ref, lq2_ref, lk2_ref, sub_ref, o_ref, *, near):
    Q = q_ref.shape[1]
    P = ck_ref.shape[2]
    H = q_ref.shape[2] // V_DIM
    lam = _lambda(lq1_ref, lk1_ref, lq2_ref, lk2_ref)
    first = lax.broadcasted_iota(jnp.int32, (Q, V_DIM), 1) < HEAD_DIM
    nt = (((1,), (1,)), ((), ()))
    far = P - near
    for h in range(H):
        cols = slice(h * V_DIM, (h + 1) * V_DIM)
        qh = q_ref[0, :, cols]
        zero = jnp.zeros_like(qh)
        q2 = jnp.concatenate([jnp.where(first, qh, zero), jnp.where(first, zero, qh)], axis=0)
        k_far = ck_ref[0, cols, :far].astype(BF16)
        k_near = ck_ref[0, cols, far:].astype(BF16)
        bc = bc_ref[h]
        bn = bn_ref[h]
        s_far = jnp.dot(q2, k_far, preferred_element_type=F32)
        s_near = (jnp.dot(q2, k_near, preferred_element_type=F32)
                  + jnp.concatenate([bc, bc], axis=0))
        s_new = (lax.dot_general(q2, kn_ref[0, :, cols], nt, preferred_element_type=F32)
                 + jnp.concatenate([bn, bn], axis=0))
        m = jnp.maximum(jnp.maximum(jnp.max(s_far, axis=-1, keepdims=True),
                                    jnp.max(s_near, axis=-1, keepdims=True)),
                        jnp.max(s_new, axis=-1, keepdims=True))
        p_far = jnp.exp(s_far - m)
        p_near = jnp.exp(s_near - m)
        p_new = jnp.exp(s_new - m)
        l = (jnp.sum(p_far, axis=-1, keepdims=True) + jnp.sum(p_near, axis=-1, keepdims=True)
             + jnp.sum(p_new, axis=-1, keepdims=True))
        v_far = cv_ref[0, pl.ds(h, far, stride=H), :].astype(BF16)
        v_near = cv_ref[0, pl.ds(far * H + h, near, stride=H), :].astype(BF16)
        acc = (jnp.dot(p_far.astype(BF16), v_far, preferred_element_type=F32)
               + jnp.dot(p_near.astype(BF16), v_near, preferred_element_type=F32)
               + jnp.dot(p_new.astype(BF16), vn_ref[0, :, cols], preferred_element_type=F32))
        on = acc * (1.0 / l)
        o = on[:Q] - lam * on[Q:]
        o = o * lax.rsqrt(jnp.mean(o * o, axis=-1, keepdims=True) + EPS)
        o_ref[0, :, cols] = (o * sub_ref[...] * (1.0 - LAM_INIT)).astype(o_ref.dtype)


def _sample_attention(qb, kb, vb, cache_k, cache_v, bias_c, bias_n, w):
    B, Q, A = qb.shape
    P = cache_k.shape[2]
    near = bias_c.shape[-1]
    new_spec = pl.BlockSpec((1, Q, A), lambda b: (b, 0, 0))
    ck_spec = pl.BlockSpec((1, A, P), lambda b: (b, 0, 0))
    cv_spec = pl.BlockSpec((1,) + cache_v.shape[1:], lambda b: (b, 0, 0))
    lam_spec = pl.BlockSpec((1, HEAD_DIM), lambda b: (0, 0))
    blk_bytes = 2 * 2 * 4 * P * A + 2 * 4 * 2 * Q * A
    tmp_bytes = 4 * 2 * Q * (3 * P + 2 * P) + 2 * 2 * P * V_DIM
    return pl.pallas_call(
        functools.partial(_sample_attn_kernel, near=near),
        grid=(B,),
        in_specs=[new_spec, new_spec, new_spec, ck_spec, cv_spec,
                  pl.BlockSpec(bias_c.shape, lambda b: (0, 0, 0)),
                  pl.BlockSpec(bias_n.shape, lambda b: (0, 0, 0)),
                  lam_spec, lam_spec, lam_spec, lam_spec,
                  pl.BlockSpec((1, V_DIM), lambda b: (0, 0))],
        out_specs=new_spec,
        out_shape=jax.ShapeDtypeStruct((B, Q, A), BF16),
        compiler_params=pltpu.CompilerParams(
            dimension_semantics=("parallel",),
            vmem_limit_bytes=_vmem_limit(blk_bytes + tmp_bytes)),
        name="sample_attn",
    )(qb, kb, vb, cache_k, cache_v, bias_c, bias_n,
      w["lambda_q1"], w["lambda_k1"], w["lambda_q2"], w["lambda_k2"], w["subln"])


def _gelu_tanh(x):
    return 0.5 * x * (1.0 + jnp.tanh(math.sqrt(2.0 / math.pi) * (x + 0.044715 * (x * x * x))))


def _lru_kernel(xr_ref, xg_ref, h0_ref, c0_ref, cw_ref, cb_ref, wgate_ref, bgate_ref, L_ref, gn_ref,
                r_ref, hl_ref, cn_ref, xbuf_ref, a_ref, b_ref, h_ref, hc_ref):
    t = pl.program_id(1)
    ts, W = xr_ref.shape[1:]
    tail = CONV_WIDTH - 1
    base = SUBLANES

    @pl.when(t == 0)
    def _():
        xbuf_ref[base - tail:base, :] = c0_ref[0]
        hc_ref[...] = jnp.broadcast_to(h0_ref[0], hc_ref.shape)

    xbuf_ref[base:base + ts, :] = xr_ref[0]
    xc = cb_ref[...]
    for j in range(CONV_WIDTH):
        xc = xc + xbuf_ref[base - tail + j:base - tail + j + ts, :] * cw_ref[j:j + 1, :]
    xbuf_ref[base - tail:base, :] = xbuf_ref[base + ts - tail:base + ts, :]

    gates = jnp.dot(xc.astype(BF16), wgate_ref[...], preferred_element_type=F32) + bgate_ref[...]
    r = _sigmoid(gates[:, :W])
    i = _sigmoid(gates[:, W:])
    z = -L_ref[...]
    softplus = jnp.maximum(z, 0.0) + jnp.log1p(jnp.exp(-jnp.abs(z)))
    log_a = -LRU_C * r * softplus
    a_ref[...] = jnp.exp(log_a)
    th = jnp.tanh(log_a)
    b_ref[...] = jnp.sqrt(-2.0 * th / (1.0 - th)) * (i * xc)

    row = lax.broadcasted_iota(jnp.int32, (SUBLANES, W), 0)

    def group(g, hprev):
        rows = pl.ds(pl.multiple_of(g * SUBLANES, SUBLANES), SUBLANES)
        A = a_ref[rows, :]
        Bv = b_ref[rows, :]
        d = 1
        while d < SUBLANES:
            keep = row >= d
            A_sh = jnp.where(keep, pltpu.roll(A, d, 0), 1.0)
            B_sh = jnp.where(keep, pltpu.roll(Bv, d, 0), 0.0)
            Bv = A * B_sh + Bv
            A = A * A_sh
            d *= 2
        Hg = A * hprev + Bv
        h_ref[rows, :] = Hg
        return jnp.broadcast_to(Hg[SUBLANES - 1:SUBLANES, :], hprev.shape)

    hc_ref[...] = lax.fori_loop(0, ts // SUBLANES, group, hc_ref[...])

    out = h_ref[...] * _gelu_tanh(xg_ref[0])
    r_ref[0] = _rms(out, gn_ref[...]).astype(r_ref.dtype)

    @pl.when(t == pl.num_programs(1) - 1)
    def _():
        hl_ref[0] = hc_ref[0:1, :]
        cn_ref[0] = xbuf_ref[base - tail:base, :]


def _lru(xr, xg, h0, conv0, w):
    B, S, W = xr.shape
    ts = min(LRU_TILE, S)
    assert S % ts == 0 and ts % SUBLANES == 0 and S >= CONV_WIDTH - 1
    tail = CONV_WIDTH - 1
    seq = pl.BlockSpec((1, ts, W), lambda b, t: (b, t, 0))
    per_b = lambda r: pl.BlockSpec((1, r, W), lambda b, t: (b, 0, 0))
    const = lambda shape: pl.BlockSpec(shape, lambda b, t: (0, 0))
    return pl.pallas_call(
        _lru_kernel,
        grid=(B, S // ts),
        in_specs=[seq, seq, per_b(1), per_b(tail),
                  const((CONV_WIDTH, W)), const((1, W)), const((W, 2 * W)), const((1, 2 * W)),
                  const((1, W)), const((1, W))],
        out_specs=[seq, per_b(1), per_b(tail)],
        out_shape=[jax.ShapeDtypeStruct((B, S, W), BF16),
                   jax.ShapeDtypeStruct((B, 1, W), F32),
                   jax.ShapeDtypeStruct((B, tail, W), F32)],
        scratch_shapes=[pltpu.VMEM((ts + SUBLANES, W), F32), pltpu.VMEM((ts, W), F32),
                        pltpu.VMEM((ts, W), F32), pltpu.VMEM((ts, W), F32),
                        pltpu.VMEM((SUBLANES, W), F32)],
        compiler_params=pltpu.CompilerParams(dimension_semantics=("parallel", "arbitrary")),
        name="rg_lru",
    )(xr, xg, h0.reshape(B, 1, W), conv0, w["conv_w"], w["conv_b"], w["w_gate"], w["b_gate"],
      w["lru_L"], w["lru_out_norm"])


def _block_diag(wb):
    n, c, d = wb.shape
    eye = jnp.eye(n, dtype=wb.dtype)
    return (eye[:, None, :, None] * wb[:, :, None, :]).reshape(n * c, n * d)


def _prepare_weights(l, p):
    D, d_ff = p["ffn1_gate"].shape[1:]
    assert d_ff % FF_CHUNK == 0
    lw = p["conv_w"].shape[-1]
    in_width = p["w_in"].shape[-1]
    a = (in_width - 2 * lw) // 3
    assert a % V_DIM == 0

    group = np.arange(a) // HEAD_DIM
    w = dict(attn_width=a, lru_width=lw)
    for name in ("ffn1_gate", "ffn1_up", "ffn1_down", "ffn2_gate", "ffn2_up", "ffn2_down"):
        w[name] = p[name][l].astype(BF16)
    for name in ("norm_ffn1", "norm_mix", "norm_ffn2", "lru_out_norm", "conv_b", "lru_L", "subln",
                 "lambda_q1", "lambda_k1", "lambda_q2", "lambda_k2"):
        w[name] = p[name][l][None, :]
    w["w_in"] = p["w_in"][l].astype(BF16)
    w["group_mat"] = jnp.asarray((group[:, None] == group[None, :]) / HEAD_DIM, BF16)
    w["q_gain"] = jnp.tile(p["q_norm"][l], a // HEAD_DIM)[None, :]
    w["k_gain"] = jnp.tile(p["k_norm"][l], a // HEAD_DIM)[None, :]
    w["w_out_attn"] = p["w_out"][l][:a].astype(BF16)
    w["w_out_lru"] = p["w_out"][l][a:].astype(BF16)
    w["conv_w"] = p["conv_w"][l]
    w["w_gate"] = jnp.concatenate([_block_diag(p["gate_a_w"][l]), _block_diag(p["gate_x_w"][l])],
                                  axis=1).astype(BF16)
    w["b_gate"] = jnp.concatenate([p["gate_a_b"][l], p["gate_x_b"][l]])[None, :]
    return w


def _layer_prompt(x, w, rel_bias):
    B, S, D = x.shape
    T = ATTN_TILE
    a, lw = w["attn_width"], w["lru_width"]
    H = a // V_DIM
    x1, kT, v, xr, xg, qT, kb, vT = _stage_a(x, w, transposed=True)
    k = kT.reshape(B, H, 2, HEAD_DIM, S).transpose(0, 4, 1, 2, 3)
    nl = S // LAYOUT_TILE
    score_bound = (HEAD_DIM ** 0.5 * LOG2E * (1 + 2.0 ** -7)
                   * jnp.max(jnp.abs(w["q_gain"])) * jnp.max(jnp.abs(w["k_gain"])))
    bias_bound = 2 * LOG2E * jnp.max(jnp.abs(rel_bias))
    attend = lambda fixed: functools.partial(_prompt_attention, w=w, fixed_shift=fixed)
    o = lax.cond(2 * (score_bound + bias_bound) <= SHIFT_RANGE, attend(True), attend(False),
                 qT.reshape(B, nl, a, LAYOUT_TILE), kb.reshape(B, S, a),
                 vT.reshape(B, nl, a, LAYOUT_TILE), rel_bias)
    r, h_last, conv_new = _lru(xr.reshape(B, S, lw), xg.reshape(B, S, lw),
                               jnp.zeros((B, lw), F32), jnp.zeros((B, CONV_WIDTH - 1, lw), F32), w)
    y = _stage_d(x1, o.reshape(B * S, a), r.reshape(B * S, lw), w)
    return (y.reshape(B, S, D), k, v.reshape(B, S, H, V_DIM), h_last.reshape(B, lw), conv_new)


def _layer_sample(x, w, rel_bias, k_past, v_past, h0, conv0):
    B, Q, D = x.shape
    P = k_past.shape[1]
    a, lw = w["attn_width"], w["lru_width"]
    H = a // V_DIM
    near = LANES
    assert P % CHUNK == 0 and Q <= CHUNK and near >= MAX_DISTANCE and P > near
    x1, k, v, xr, xg, qb, kb, vb = _stage_a(x, w, transposed=False)
    q_pos = P + jnp.arange(Q, dtype=jnp.int32)
    idx_c = _bucket_tile(q_pos, jnp.arange(P - near, P, dtype=jnp.int32), False)
    idx_n = _bucket_tile(q_pos, q_pos, False)
    bias_c = _bias_tiles(rel_bias, idx_c[None])[:, 0]
    bias_n = _bias_tiles(rel_bias, idx_n[None])[:, 0]
    o = _sample_attention(qb.reshape(B, Q, a), kb.reshape(B, Q, a), vb.reshape(B, Q, a),
                          k_past.transpose(0, 2, 3, 4, 1).reshape(B, a, P),
                          v_past.reshape(B, P * H, V_DIM),
                          bias_c, bias_n, w)
    r, h_last, conv_new = _lru(xr.reshape(B, Q, lw), xg.reshape(B, Q, lw), h0, conv0, w)
    y = _stage_d(x1, o.reshape(B * Q, a), r.reshape(B * Q, lw), w)
    return (y.reshape(B, Q, D), k.reshape(B, Q, H, 2, HEAD_DIM), v.reshape(B, Q, H, V_DIM),
            h_last.reshape(B, lw), conv_new)


def kernel(x_prompt, x_sample, cache_k, cache_v, state_lru, state_conv, rel_bias, norm_ffn1, ffn1_gate, ffn1_up, ffn1_down, norm_mix, w_in, q_norm, k_norm, lambda_q1, lambda_k1, lambda_q2, lambda_k2, subln, conv_w, conv_b, gate_a_w, gate_a_b, gate_x_w, gate_x_b, lru_L, lru_out_norm, w_out, norm_ffn2, ffn2_gate, ffn2_up, ffn2_down):
    p = dict(norm_ffn1=norm_ffn1, ffn1_gate=ffn1_gate, ffn1_up=ffn1_up, ffn1_down=ffn1_down,
             norm_mix=norm_mix, w_in=w_in, q_norm=q_norm, k_norm=k_norm,
             lambda_q1=lambda_q1, lambda_k1=lambda_k1, lambda_q2=lambda_q2, lambda_k2=lambda_k2,
             subln=subln, conv_w=conv_w, conv_b=conv_b, gate_a_w=gate_a_w, gate_a_b=gate_a_b,
             gate_x_w=gate_x_w, gate_x_b=gate_x_b, lru_L=lru_L, lru_out_norm=lru_out_norm,
             w_out=w_out, norm_ffn2=norm_ffn2, ffn2_gate=ffn2_gate, ffn2_up=ffn2_up,
             ffn2_down=ffn2_down)
    depth = cache_k.shape[0]
    assert depth == 1, "the lambda initial value is specialised to a single layer"
    w = _prepare_weights(0, p)
    yp, kp, vp, hp, cp = _layer_prompt(x_prompt, w, rel_bias)
    ys, kn, vn, hn, cn = _layer_sample(x_sample, w, rel_bias, cache_k[0], cache_v[0],
                                       state_lru[0], state_conv[0])
    stack = lambda t: t[None]
    return (yp, ys, stack(kp), stack(vp), stack(hp), stack(cp),
            stack(kn), stack(vn), stack(hn), stack(cn))
```

```python
import functools
import math

import numpy as np
import jax
import jax.numpy as jnp
from jax import lax
from jax.experimental import pallas as pl
from jax.experimental.pallas import tpu as pltpu

F32 = jnp.float32
BF16 = jnp.bfloat16

HEAD_DIM = 64
V_DIM = 2 * HEAD_DIM
CHUNK = 64
NUM_BUCKETS = 32
MAX_DISTANCE = 128
CONV_WIDTH = 4
LRU_BLOCKS = 8
LRU_C = 8.0
EPS = 1e-6
NEG_INF = -1e30
LAM_INIT = 0.8 - 0.6 * math.exp(-0.3 * 0)

LANES = 128
SUBLANES = 8
MXU_DIM = 256
VMEM_BYTES = 64 * 1024 * 1024

FF_CHUNK = MXU_DIM
ATTN_TILE = 512
LAYOUT_TILE = 256
FAR_ROWS = 512
FAR_TILES = 4
LOG2E = 1.4426950408889634
SCORE_RANGE = 60.0
TOKEN_TILE = 512
ROW_GROUPS = 1
LRU_TILE = 512


def _vmem_limit(nbytes):
    return int(min(VMEM_BYTES - 4 * 1024 * 1024, max(nbytes, 16 * 1024 * 1024)))


def _rms(x, g):
    ms = jnp.mean(x * x, axis=-1, keepdims=True)
    return x * lax.rsqrt(ms + EPS) * g


def _sigmoid(x):
    return 0.5 * jnp.tanh(0.5 * x) + 0.5


def _const_spec(shape):
    n = len(shape)
    return pl.BlockSpec(shape, lambda *_: (0,) * n, pipeline_mode=pl.Buffered(1))


def _swiglu_into(acc_ref, xn, wg_ref, wu_ref, wd_ref):
    d_ff = wg_ref.shape[1]
    for c in range(0, d_ff, FF_CHUNK):
        cols = slice(c, c + FF_CHUNK)
        g = jnp.dot(xn, wg_ref[:, cols], preferred_element_type=F32)
        u = jnp.dot(xn, wu_ref[:, cols], preferred_element_type=F32)
        h = (g * _sigmoid(g) * u).astype(BF16)
        part = jnp.dot(h, wd_ref[cols, :], preferred_element_type=F32)
        if c == 0:
            acc_ref[...] = part
        else:
            acc_ref[...] += part


def _stage_a_kernel(x_ref, g1_ref, wg_ref, wu_ref, wd_ref, gm_ref, win_ref, gmat_ref, qg_ref, kg_ref,
                    x1_ref, k_ref, v_ref, xr_ref, xg_ref, qb_ref, kb_ref, vb_ref, acc_ref,
                    *, attn_width, lru_width, transposed, tile, q_scale):
    x = x_ref[...]
    xn = _rms(x, g1_ref[...]).astype(BF16)
    _swiglu_into(acc_ref, xn, wg_ref, wu_ref, wd_ref)
    x1 = x + 0.5 * acc_ref[...]
    x1_ref[...] = x1
    hn = _rms(x1, gm_ref[...]).astype(BF16)
    proj = jnp.dot(hn, win_ref[...], preferred_element_type=F32)
    a = attn_width
    q = proj[:, :a]
    k = proj[:, a:2 * a]
    v = proj[:, 2 * a:3 * a]
    xr_ref[...] = proj[:, 3 * a:3 * a + lru_width]
    xg_ref[...] = proj[:, 3 * a + lru_width:]

    def group_norm(t, g):
        ms = jnp.dot((t * t).astype(BF16), gmat_ref[...], preferred_element_type=F32)
        return t * lax.rsqrt(ms + EPS) * g

    qn = group_norm(q, qg_ref[...]) * q_scale
    kn = group_norm(k, kg_ref[...])
    n_heads = a // V_DIM
    for h in range(n_heads):
        v_ref[pl.ds(h, x.shape[0], stride=n_heads), :] = v[:, h * V_DIM:(h + 1) * V_DIM]
    kb_ref[...] = kn.astype(BF16)
    if transposed:
        k_ref[0] = kn.T
        for j in range(x.shape[0] // tile):
            rows = slice(j * tile, (j + 1) * tile)
            qb_ref[0, j] = qn[rows, :].T.astype(BF16)
            vb_ref[0, j] = v[rows, :].T.astype(BF16)
    else:
        k_ref[...] = kn
        qb_ref[...] = qn.astype(BF16)
        vb_ref[...] = v.astype(BF16)


def _stage_a(x, w, *, transposed):
    B, S, D = x.shape
    a = w["attn_width"]
    lw = w["lru_width"]
    tm = TOKEN_TILE
    T = LAYOUT_TILE
    q_scale = HEAD_DIM ** -0.5 * (LOG2E if transposed else 1.0)
    assert (B * S) % tm == 0 and (not transposed or (tm % T == 0 and S % tm == 0))
    nt = S // tm
    x2 = x.reshape(B * S, D)
    d_ff = w["ffn1_gate"].shape[1]
    tok = lambda width: pl.BlockSpec((tm, width), lambda i: (i, 0))
    out_shape = [
        jax.ShapeDtypeStruct((B * S, D), F32),
        jax.ShapeDtypeStruct((B * S, a), F32),
        jax.ShapeDtypeStruct((B * S, a), F32),
        jax.ShapeDtypeStruct((B * S, lw), F32),
        jax.ShapeDtypeStruct((B * S, lw), F32),
    ]
    out_specs = [tok(D), tok(a), tok(a), tok(lw), tok(lw)]
    H = a // V_DIM
    out_shape[2] = jax.ShapeDtypeStruct((B * S * H, V_DIM), F32)
    out_specs[2] = pl.BlockSpec((tm * H, V_DIM), lambda i: (i, 0))
    if transposed:
        out_shape[1] = jax.ShapeDtypeStruct((B, a, S), F32)
        out_specs[1] = pl.BlockSpec((1, a, tm), lambda i: (i // nt, 0, i % nt))
        tshape = jax.ShapeDtypeStruct((B * nt, tm // T, a, T), BF16)
        tspec = pl.BlockSpec((1, tm // T, a, T), lambda i: (i, 0, 0, 0))
        out_shape += [tshape, jax.ShapeDtypeStruct((B * S, a), BF16), tshape]
        out_specs += [tspec, tok(a), tspec]
    else:
        bshape = jax.ShapeDtypeStruct((B * S, a), BF16)
        out_shape += [bshape, bshape, bshape]
        out_specs += [tok(a), tok(a), tok(a)]
    weight_bytes = 2 * (3 * D * d_ff + D * (3 * a + 2 * lw) + a * a)
    act_bytes = 4 * tm * (2 * 2 * D + 2 * 2 * (2 * a + 2 * lw) + D + 2 * (3 * a + 2 * lw) + 4 * FF_CHUNK)
    kern = functools.partial(_stage_a_kernel, attn_width=a, lru_width=lw, transposed=transposed,
                             tile=T, q_scale=q_scale)
    outs = pl.pallas_call(
        kern,
        grid=(B * S // tm,),
        in_specs=[
            tok(D),
            _const_spec((1, D)),
            _const_spec(w["ffn1_gate"].shape),
            _const_spec(w["ffn1_up"].shape),
            _const_spec(w["ffn1_down"].shape),
            _const_spec((1, D)),
            _const_spec(w["w_in"].shape),
            _const_spec((a, a)),
            _const_spec((1, a)),
            _const_spec((1, a)),
        ],
        out_specs=out_specs,
        out_shape=out_shape,
        scratch_shapes=[pltpu.VMEM((tm, D), F32)],
        compiler_params=pltpu.CompilerParams(
            dimension_semantics=("parallel",),
            vmem_limit_bytes=_vmem_limit(weight_bytes + act_bytes)),
        name="stage_a_t" if transposed else "stage_a_n",
    )(x2, w["norm_ffn1"], w["ffn1_gate"], w["ffn1_up"], w["ffn1_down"], w["norm_mix"], w["w_in"],
      w["group_mat"], w["q_gain"], w["k_gain"])
    return outs


def _stage_d_kernel(x1_ref, o_ref, r_ref, woa_ref, wor_ref, g2_ref, wg_ref, wu_ref, wd_ref,
                    y_ref, acc_ref, *, sub_rows):
    for r0 in range(0, x1_ref.shape[0], sub_rows):
        rows = slice(r0, r0 + sub_rows)
        x2 = (x1_ref[rows, :]
              + jnp.dot(o_ref[rows, :], woa_ref[...], preferred_element_type=F32)
              + jnp.dot(r_ref[rows, :], wor_ref[...], preferred_element_type=F32))
        xn = _rms(x2, g2_ref[...]).astype(BF16)
        acc = acc_ref.at[rows]
        _swiglu_into(acc, xn, wg_ref, wu_ref, wd_ref)
        y_ref[rows, :] = x2 + 0.5 * acc[...]


def _stage_d(x1, o, r, w):
    N, D = x1.shape
    a = o.shape[1]
    lw = r.shape[1]
    tm = min(TOKEN_TILE * ROW_GROUPS, N)
    assert N % tm == 0 and tm % TOKEN_TILE == 0
    d_ff = w["ffn2_gate"].shape[1]
    tok = lambda width: pl.BlockSpec((tm, width), lambda i: (i, 0))
    weight_bytes = 2 * (3 * D * d_ff + D * D)
    act_bytes = 4 * tm * (2 * 2 * D + 2 * (a + lw) + 2 * D + 4 * FF_CHUNK)
    return pl.pallas_call(
        functools.partial(_stage_d_kernel, sub_rows=TOKEN_TILE),
        grid=(N // tm,),
        in_specs=[
            tok(D), tok(a), tok(lw),
            _const_spec((a, D)), _const_spec((lw, D)), _const_spec((1, D)),
            _const_spec(w["ffn2_gate"].shape), _const_spec(w["ffn2_up"].shape),
            _const_spec(w["ffn2_down"].shape),
        ],
        out_specs=tok(D),
        out_shape=jax.ShapeDtypeStruct((N, D), F32),
        scratch_shapes=[pltpu.VMEM((tm, D), F32)],
        compiler_params=pltpu.CompilerParams(
            dimension_semantics=("parallel",),
            vmem_limit_bytes=_vmem_limit(weight_bytes + act_bytes)),
        name="stage_d",
    )(x1, o, r, w["w_out_attn"], w["w_out_lru"], w["norm_ffn2"],
      w["ffn2_gate"], w["ffn2_up"], w["ffn2_down"])


def _t5_bucket(rel):
    n = NUM_BUCKETS // 2
    max_exact = n // 2
    ret = jnp.where(rel > 0, n, 0)
    rel = jnp.abs(rel)
    relf = jnp.maximum(rel, 1).astype(jnp.float32)
    large = max_exact + (jnp.log(relf / max_exact) / math.log(MAX_DISTANCE / max_exact)
                         * (n - max_exact)).astype(jnp.int32)
    large = jnp.minimum(large, n - 1)
    return ret + jnp.where(rel < max_exact, rel, large)


FAR_BUCKET = NUM_BUCKETS // 2 - 1


def _bucket_tile(q_pos, k_pos, keys_first):
    rel = k_pos[None, :] - q_pos[:, None]
    visible = (k_pos[None, :] // CHUNK) <= (q_pos[:, None] // CHUNK)
    idx = jnp.where(visible, _t5_bucket(rel), -1)
    return idx.T if keys_first else idx


def _bias_kernel(table_ref, idx_ref, out_ref, *, scale):
    h = pl.program_id(0)
    idx = idx_ref[0]
    acc = jnp.zeros(idx.shape, F32)
    for b in range(NUM_BUCKETS):
        acc = jnp.where(idx == b, table_ref[b, h], acc)
    out_ref[0, 0] = jnp.where(idx < 0, NEG_INF, (acc - table_ref[FAR_BUCKET, h]) * scale)


def _bias_tiles(rel_bias, idx, scale=1.0):
    n, R, C = idx.shape
    H = rel_bias.shape[1]
    return pl.pallas_call(
        functools.partial(_bias_kernel, scale=scale),
        grid=(H, n),
        in_specs=[pl.BlockSpec(memory_space=pltpu.SMEM),
                  pl.BlockSpec((1, R, C), lambda h, i: (i, 0, 0))],
        out_specs=pl.BlockSpec((1, 1, R, C), lambda h, i: (h, i, 0, 0)),
        out_shape=jax.ShapeDtypeStruct((H, n, R, C), F32),
        name="bias_tiles",
    )(rel_bias, idx)


def _lambda(lq1_ref, lk1_ref, lq2_ref, lk2_ref):
    s1 = jnp.sum(lq1_ref[...] * lk1_ref[...], axis=-1, keepdims=True)
    s2 = jnp.sum(lq2_ref[...] * lk2_ref[...], axis=-1, keepdims=True)
    return jnp.exp(s1) - jnp.exp(s2) + LAM_INIT


def _split_components(qt):
    first = lax.broadcasted_iota(jnp.int32, qt.shape, 0) < HEAD_DIM
    zero = jnp.zeros_like(qt)
    return jnp.concatenate([jnp.where(first, qt, zero), jnp.where(first, zero, qt)], axis=1)


def _attn_finish(o, sub_ref):
    o = o * lax.rsqrt(jnp.mean(o * o, axis=0, keepdims=True) + EPS)
    return o.T * sub_ref[...] * (1.0 - LAM_INIT)


def _prompt_attn_fixed_kernel(qT_ref, k_ref, vT_ref, bias_ref, lq1_ref, lk1_ref, lq2_ref, lk2_ref,
                              sub_ref, o_ref, l_ref, acc_ref, *, tile, far_rows):
    G = qT_ref.shape[3]
    T = tile
    sub = T // G
    nq = qT_ref.shape[1] // sub
    lam = _lambda(lq1_ref, lk1_ref, lq2_ref, lk2_ref)

    def lane_tiles(ref, first, count):
        return jnp.concatenate([ref[0, first + j] for j in range(count)], axis=1)

    def probs(first_group, count, q2, bias=None):
        rows = pl.ds(pl.multiple_of(first_group * G, G), count * G)
        s = jnp.dot(k_ref[0, rows, :], q2, preferred_element_type=F32)
        if bias is not None:
            s = s + jnp.concatenate([bias, bias], axis=1)
        p = jnp.exp2(s)
        return (jnp.sum(p, axis=0, keepdims=True),
                jnp.dot(lane_tiles(vT_ref, first_group, count), p.astype(BF16),
                        preferred_element_type=F32))

    def far_groups(first_group, count, q2):
        per = min(far_rows // G, count)
        parts = [probs(first_group + j, per, q2) for j in range(0, count, per)]
        l_ref[...] += functools.reduce(lambda x, y: x + y, [p[0] for p in parts])
        acc_ref[...] += functools.reduce(lambda x, y: x + y, [p[1] for p in parts])

    def q_tile(qi, carry):
        q2 = _split_components(lane_tiles(qT_ref, qi * sub, sub))

        start = jnp.maximum(qi - 1, 0)
        bias = jnp.concatenate([bias_ref[0, jnp.where(qi > 0, 1, 0)],
                                bias_ref[0, jnp.where(qi > 0, 0, 2)]], axis=0)
        l_near, acc_near = probs(start * sub, 2 * sub, q2, bias)
        l_ref[...] = l_near
        acc_ref[...] = acc_near

        n_far = jnp.maximum(qi - 1, 0)

        def far(i, c):
            far_groups(i * FAR_TILES * sub, FAR_TILES * sub, q2)
            return c

        lax.fori_loop(0, n_far // FAR_TILES, far, 0)
        done = n_far - n_far % FAR_TILES
        chunk = FAR_TILES // 2
        while chunk >= 1:
            take = (n_far - done) >= chunk

            @pl.when(take)
            def _(done=done, chunk=chunk):
                far_groups(done * sub, chunk * sub, q2)

            done = done + jnp.where(take, chunk, 0)
            chunk //= 2

        on = acc_ref[...] * (1.0 / l_ref[...])
        o = on[:, :T] - lam * on[:, T:]
        rows = pl.ds(pl.multiple_of(qi * T, T), T)
        o_ref[0, rows, :] = _attn_finish(o, sub_ref).astype(o_ref.dtype)
        return carry

    lax.fori_loop(0, nq, q_tile, 0)


def _prompt_attn_online_kernel(qT_ref, k_ref, vT_ref, bias_ref, lq1_ref, lk1_ref, lq2_ref, lk2_ref,
                               sub_ref, o_ref, m_ref, l_ref, acc_ref, *, tile):
    T = tile
    sub = T // qT_ref.shape[3]
    nq = qT_ref.shape[1] // sub
    lam = _lambda(lq1_ref, lk1_ref, lq2_ref, lk2_ref)

    def lane_tiles(ref, i):
        return jnp.concatenate([ref[0, i * sub + j] for j in range(sub)], axis=1)

    def scores(ki, q2):
        kt = k_ref[0, pl.ds(pl.multiple_of(ki * T, T), T), :]
        return jnp.dot(kt, q2, preferred_element_type=F32)

    def pv(p, ki):
        return jnp.dot(lane_tiles(vT_ref, ki), p.astype(BF16), preferred_element_type=F32)

    def update(s, ki):
        m_prev = m_ref[...]
        m_new = jnp.maximum(m_prev, jnp.max(s, axis=0, keepdims=True))
        alpha = jnp.exp2(m_prev - m_new)
        p = jnp.exp2(s - m_new)
        l_ref[...] = alpha * l_ref[...] + jnp.sum(p, axis=0, keepdims=True)
        acc_ref[...] = alpha * acc_ref[...] + pv(p, ki)
        m_ref[...] = m_new

    def q_tile(qi, carry):
        q2 = _split_components(lane_tiles(qT_ref, qi))

        bias = bias_ref[0, 0]
        s = scores(qi, q2) + jnp.concatenate([bias, bias], axis=1)
        m = jnp.max(s, axis=0, keepdims=True)
        p = jnp.exp2(s - m)
        m_ref[...] = m
        l_ref[...] = jnp.sum(p, axis=0, keepdims=True)
        acc_ref[...] = pv(p, qi)
        kp = jnp.maximum(qi - 1, 0)
        bias = bias_ref[0, jnp.where(qi > 0, 1, 2)]
        s = scores(kp, q2) + jnp.concatenate([bias, bias], axis=1)
        n_far = qi - 1

        update(s, kp)

        def far(ki, s_cur):
            s_next = scores(jnp.minimum(ki + 1, jnp.maximum(n_far - 1, 0)), q2)
            update(s_cur, ki)
            return s_next

        lax.fori_loop(0, n_far, far, scores(0, q2))

        on = acc_ref[...] * (1.0 / l_ref[...])
        o = on[:, :T] - lam * on[:, T:]
        rows = pl.ds(pl.multiple_of(qi * T, T), T)
        o_ref[0, rows, :] = _attn_finish(o, sub_ref).astype(o_ref.dtype)
        return carry

    lax.fori_loop(0, nq, q_tile, 0)


def _near_bias(rel_bias, tile, offsets):
    pos = jnp.arange(tile, dtype=jnp.int32)
    base = tile * max(o for o in offsets if o is not None)
    idx = [jnp.full((tile, tile), -1, jnp.int32) if o is None
           else _bucket_tile(pos + base, pos + base - o * tile, True) for o in offsets]
    return _bias_tiles(rel_bias, jnp.stack(idx), LOG2E)


def _prompt_attention(qT, kb, vT, rel_bias, w, *, fixed_shift):
    T = ATTN_TILE
    B, S = kb.shape[:2]
    G = qT.shape[-1]
    A = kb.shape[-1]
    H = A // V_DIM
    assert S % T == 0 and G % CHUNK == 0 and G >= MAX_DISTANCE and T == 2 * G
    lam_spec = pl.BlockSpec((1, HEAD_DIM), lambda b, h: (0, 0))
    tile_spec = pl.BlockSpec((1, S // G, V_DIM, G), lambda b, h: (b, 0, h, 0))
    seq_spec = pl.BlockSpec((1, S, V_DIM), lambda b, h: (b, 0, h))
    bias = _near_bias(rel_bias, T, (0, 1, None))
    tmp_bytes = 4 * 8 * T * 2 * T
    if fixed_shift:
        kern = functools.partial(_prompt_attn_fixed_kernel, tile=T, far_rows=FAR_ROWS)
        stats = []
    else:
        kern = functools.partial(_prompt_attn_online_kernel, tile=T)
        stats = [pltpu.VMEM((1, 2 * T), F32)]
    blk_bytes = 2 * 2 * (3 * S * V_DIM) + 2 * 2 * S * V_DIM + 2 * 4 * bias[0].size
    return pl.pallas_call(
        kern,
        grid=(B, H),
        in_specs=[tile_spec, seq_spec, tile_spec,
                  pl.BlockSpec((1,) + bias.shape[1:], lambda b, h: (h, 0, 0, 0)),
                  lam_spec, lam_spec, lam_spec, lam_spec,
                  pl.BlockSpec((1, V_DIM), lambda b, h: (0, 0))],
        out_specs=seq_spec,
        out_shape=jax.ShapeDtypeStruct((B, S, A), BF16),
        scratch_shapes=stats + [pltpu.VMEM((1, 2 * T), F32), pltpu.VMEM((V_DIM, 2 * T), F32)],
        compiler_params=pltpu.CompilerParams(
            dimension_semantics=("parallel", "parallel"),
            vmem_limit_bytes=_vmem_limit(blk_bytes + tmp_bytes + 4 * V_DIM * 2 * T)),
        name="prompt_attn_fixed" if fixed_shift else "prompt_attn_online",
    )(qT, kb, vT, bias, w["lambda_q1"], w["lambda_k1"], w["lambda_q2"], w["lambda_k2"], w["subln"])


def _sample_attn_kernel(q_ref, kn_ref, vn_ref, ck_ref, cv_ref, bc_ref, bn_ref,
                        lq1_ref, lk1_ref, lq2_ref, lk2_ref, sub_ref, o_ref, *, near):
    Q = q_ref.shape[1]
    P = ck_ref.shape[2]
    H = q_ref.shape[2] // V_DIM
    lam = _lambda(lq1_ref, lk1_ref, lq2_ref, lk2_ref)
    first = lax.broadcasted_iota(jnp.int32, (Q, V_DIM), 1) < HEAD_DIM
    nt = (((1,), (1,)), ((), ()))
    far = P - near
    for h in range(H):
        cols = slice(h * V_DIM, (h + 1) * V_DIM)
        qh = q_ref[0, :, cols]
        zero = jnp.zeros_like(qh)
        q2 = jnp.concatenate([jnp.where(first, qh, zero), jnp.where(first, zero, qh)], axis=0)
        k_far = ck_ref[0, cols, :far].astype(BF16)
        k_near = ck_ref[0, cols, far:].astype(BF16)
        bc = bc_ref[h]
        bn = bn_ref[h]
        s_far = jnp.dot(q2, k_far, preferred_element_type=F32)
        s_near = (jnp.dot(q2, k_near, preferred_element_type=F32)
                  + jnp.concatenate([bc, bc], axis=0))
        s_new = (lax.dot_general(q2, kn_ref[0, :, cols], nt, preferred_element_type=F32)
                 + jnp.concatenate([bn, bn], axis=0))
        m = jnp.maximum(jnp.maximum(jnp.max(s_far, axis=-1, keepdims=True),
                                    jnp.max(s_near, axis=-1, keepdims=True)),
                        jnp.max(s_new, axis=-1, keepdims=True))
        p_far = jnp.exp(s_far - m)
        p_near = jnp.exp(s_near - m)
        p_new = jnp.exp(s_new - m)
        l = (jnp.sum(p_far, axis=-1, keepdims=True) + jnp.sum(p_near, axis=-1, keepdims=True)
             + jnp.sum(p_new, axis=-1, keepdims=True))
        v_far = cv_ref[0, pl.ds(h, far, stride=H), :].astype(BF16)
        v_near = cv_ref[0, pl.ds(far * H + h, near, stride=H), :].astype(BF16)
        acc = (jnp.dot(p_far.astype(BF16), v_far, preferred_element_type=F32)
               + jnp.dot(p_near.astype(BF16), v_near, preferred_element_type=F32)
               + jnp.dot(p_new.astype(BF16), vn_ref[0, :, cols], preferred_element_type=F32))
        on = acc * (1.0 / l)
        o = on[:Q] - lam * on[Q:]
        o = o * lax.rsqrt(jnp.mean(o * o, axis=-1, keepdims=True) + EPS)
        o_ref[0, :, cols] = (o * sub_ref[...] * (1.0 - LAM_INIT)).astype(o_ref.dtype)


def _sample_attention(qb, kb, vb, cache_k, cache_v, bias_c, bias_n, w):
    B, Q, A = qb.shape
    P = cache_k.shape[2]
    near = bias_c.shape[-1]
    new_spec = pl.BlockSpec((1, Q, A), lambda b: (b, 0, 0))
    ck_spec = pl.BlockSpec((1, A, P), lambda b: (b, 0, 0))
    cv_spec = pl.BlockSpec((1,) + cache_v.shape[1:], lambda b: (b, 0, 0))
    lam_spec = pl.BlockSpec((1, HEAD_DIM), lambda b: (0, 0))
    blk_bytes = 2 * 2 * 4 * P * A + 2 * 4 * 2 * Q * A
    tmp_bytes = 4 * 2 * Q * (3 * P + 2 * P) + 2 * 2 * P * V_DIM
    return pl.pallas_call(
        functools.partial(_sample_attn_kernel, near=near),
        grid=(B,),
        in_specs=[new_spec, new_spec, new_spec, ck_spec, cv_spec,
                  pl.BlockSpec(bias_c.shape, lambda b: (0, 0, 0)),
                  pl.BlockSpec(bias_n.shape, lambda b: (0, 0, 0)),
                  lam_spec, lam_spec, lam_spec, lam_spec,
                  pl.BlockSpec((1, V_DIM), lambda b: (0, 0))],
        out_specs=new_spec,
        out_shape=jax.ShapeDtypeStruct((B, Q, A), BF16),
        compiler_params=pltpu.CompilerParams(
            dimension_semantics=("parallel",),
            vmem_limit_bytes=_vmem_limit(blk_bytes + tmp_bytes)),
        name="sample_attn",
    )(qb, kb, vb, cache_k, cache_v, bias_c, bias_n,
      w["lambda_q1"], w["lambda_k1"], w["lambda_q2"], w["lambda_k2"], w["subln"])


def _gelu_tanh(x):
    return 0.5 * x * (1.0 + jnp.tanh(math.sqrt(2.0 / math.pi) * (x + 0.044715 * (x * x * x))))


def _lru_kernel(xr_ref, xg_ref, h0_ref, c0_ref, cw_ref, cb_ref, wgate_ref, bgate_ref, L_ref, gn_ref,
                perm_ref, r_ref, hl_ref, cn_ref, tail_ref, hc_ref):
    t = pl.program_id(1)
    ts, W = xr_ref.shape[1:]
    n = ts // SUBLANES
    tail = CONV_WIDTH - 1
    row = lax.broadcasted_iota(jnp.int32, (SUBLANES, W), 0)

    @pl.when(t == 0)
    def _():
        tail_ref[SUBLANES - tail:, :] = c0_ref[0]
        hc_ref[...] = jnp.broadcast_to(h0_ref[0], hc_ref.shape)

    def regroup(x):
        hi = x.astype(BF16)
        lo = (x - hi.astype(F32)).astype(BF16)
        return (jnp.dot(perm_ref[0], hi, preferred_element_type=F32)
                + jnp.dot(perm_ref[0], lo, preferred_element_type=F32))

    def blocks(v):
        return [v[j * SUBLANES:(j + 1) * SUBLANES] for j in range(n)]

    def shift_subsequences(block, entering):
        return jnp.where(row == 0, entering, pltpu.roll(block, 1, 0))

    xp = blocks(regroup(xr_ref[0]))
    lead = [shift_subsequences(xp[n - k], tail_ref[SUBLANES - k:SUBLANES - k + 1, :])
            for k in range(tail, 0, -1)]
    ext = lead + xp
    xc = cb_ref[...]
    for j in range(CONV_WIDTH):
        xc = xc + jnp.concatenate(ext[j:j + n], axis=0) * cw_ref[j:j + 1, :]
    tail_ref[SUBLANES - tail:, :] = xr_ref[0, ts - tail:, :]

    gates = jnp.dot(xc.astype(BF16), wgate_ref[...], preferred_element_type=F32) + bgate_ref[...]
    r = _sigmoid(gates[:, :W])
    i = _sigmoid(gates[:, W:])
    z = -L_ref[...]
    softplus = jnp.maximum(z, 0.0) + jnp.log1p(jnp.exp(-jnp.abs(z)))
    log_a = -LRU_C * r * softplus
    a = blocks(jnp.exp(log_a))
    th = jnp.tanh(log_a)
    b = blocks(jnp.sqrt(-2.0 * th / (1.0 - th)) * (i * xc))

    cum_a, cum_b = [a[0]], [b[0]]
    for j in range(1, n):
        cum_a.append(a[j] * cum_a[-1])
        cum_b.append(a[j] * cum_b[-1] + b[j])
    end_a, end_b = cum_a[-1], cum_b[-1]
    d = 1
    while d < SUBLANES:
        keep = row >= d
        a_sh = jnp.where(keep, pltpu.roll(end_a, d, 0), 1.0)
        b_sh = jnp.where(keep, pltpu.roll(end_b, d, 0), 0.0)
        end_b = end_a * b_sh + end_b
        end_a = end_a * a_sh
        d *= 2
    h_prev = hc_ref[0:1, :]
    h_end = end_a * h_prev + end_b
    h_in = shift_subsequences(h_end, h_prev)
    h = jnp.concatenate([cum_a[j] * h_in + cum_b[j] for j in range(n)], axis=0)
    hc_ref[...] = jnp.broadcast_to(h_end[SUBLANES - 1:, :], hc_ref.shape)

    gate = jnp.dot(perm_ref[0], _gelu_tanh(xg_ref[0]).astype(BF16), preferred_element_type=F32)
    out = h * gate
    normed = _rms(out, gn_ref[...]).astype(BF16)
    r_ref[0] = jnp.dot(perm_ref[1], normed, preferred_element_type=F32).astype(r_ref.dtype)

    @pl.when(t == pl.num_programs(1) - 1)
    def _():
        hl_ref[0] = hc_ref[0:1, :]
        cn_ref[0] = xr_ref[0, ts - tail:, :]


def _lru(xr, xg, h0, conv0, w):
    B, S, W = xr.shape
    ts = min(LRU_TILE, S)
    assert S % ts == 0 and ts % SUBLANES == 0 and S >= CONV_WIDTH - 1
    tail = CONV_WIDTH - 1
    seq = pl.BlockSpec((1, ts, W), lambda b, t: (b, t, 0))
    per_b = lambda r: pl.BlockSpec((1, r, W), lambda b, t: (b, 0, 0))
    const = lambda shape: pl.BlockSpec(shape, lambda b, t: (0, 0))
    n = ts // SUBLANES
    time = np.arange(ts)
    perm = np.zeros((2, ts, ts), np.float32)
    perm[0, SUBLANES * (time % n) + time // n, time] = 1.0
    perm[1] = perm[0].T
    return pl.pallas_call(
        _lru_kernel,
        grid=(B, S // ts),
        in_specs=[seq, seq, per_b(1), per_b(tail),
                  const((CONV_WIDTH, W)), const((1, W)), const((W, 2 * W)), const((1, 2 * W)),
                  const((1, W)), const((1, W)),
                  pl.BlockSpec((2, ts, ts), lambda b, t: (0, 0, 0))],
        out_specs=[seq, per_b(1), per_b(tail)],
        out_shape=[jax.ShapeDtypeStruct((B, S, W), BF16),
                   jax.ShapeDtypeStruct((B, 1, W), F32),
                   jax.ShapeDtypeStruct((B, tail, W), F32)],
        scratch_shapes=[pltpu.VMEM((SUBLANES, W), F32), pltpu.VMEM((SUBLANES, W), F32)],
        compiler_params=pltpu.CompilerParams(dimension_semantics=("parallel", "arbitrary")),
        name="rg_lru",
    )(xr, xg, h0.reshape(B, 1, W), conv0, w["conv_w"], w["conv_b"], w["w_gate"], w["b_gate"],
      w["lru_L"], w["lru_out_norm"], jnp.asarray(perm, BF16))


def _block_diag(wb):
    n, c, d = wb.shape
    eye = jnp.eye(n, dtype=wb.dtype)
    return (eye[:, None, :, None] * wb[:, :, None, :]).reshape(n * c, n * d)


def _prepare_weights(l, p):
    D, d_ff = p["ffn1_gate"].shape[1:]
    assert d_ff % FF_CHUNK == 0
    lw = p["conv_w"].shape[-1]
    in_width = p["w_in"].shape[-1]
    a = (in_width - 2 * lw) // 3
    assert a % V_DIM == 0

    group = np.arange(a) // HEAD_DIM
    w = dict(attn_width=a, lru_width=lw)
    for name in ("ffn1_gate", "ffn1_up", "ffn1_down", "ffn2_gate", "ffn2_up", "ffn2_down"):
        w[name] = p[name][l].astype(BF16)
    for name in ("norm_ffn1", "norm_mix", "norm_ffn2", "lru_out_norm", "conv_b", "lru_L", "subln",
                 "lambda_q1", "lambda_k1", "lambda_q2", "lambda_k2"):
        w[name] = p[name][l][None, :]
    w["w_in"] = p["w_in"][l].astype(BF16)
    w["group_mat"] = jnp.asarray((group[:, None] == group[None, :]) / HEAD_DIM, BF16)
    w["q_gain"] = jnp.tile(p["q_norm"][l], a // HEAD_DIM)[None, :]
    w["k_gain"] = jnp.tile(p["k_norm"][l], a // HEAD_DIM)[None, :]
    w["w_out_attn"] = p["w_out"][l][:a].astype(BF16)
    w["w_out_lru"] = p["w_out"][l][a:].astype(BF16)
    w["conv_w"] = p["conv_w"][l]
    w["w_gate"] = jnp.concatenate([_block_diag(p["gate_a_w"][l]), _block_diag(p["gate_x_w"][l])],
                                  axis=1).astype(BF16)
    w["b_gate"] = jnp.concatenate([p["gate_a_b"][l], p["gate_x_b"][l]])[None, :]
    return w


def _layer_prompt(x, w, rel_bias):
    B, S, D = x.shape
    T = ATTN_TILE
    a, lw = w["attn_width"], w["lru_width"]
    H = a // V_DIM
    x1, kT, v, xr, xg, qT, kb, vT = _stage_a(x, w, transposed=True)
    k = kT.reshape(B, H, 2, HEAD_DIM, S).transpose(0, 4, 1, 2, 3)
    nl = S // LAYOUT_TILE
    score_bound = (HEAD_DIM ** 0.5 * LOG2E * (1 + 2.0 ** -7)
                   * jnp.max(jnp.abs(w["q_gain"])) * jnp.max(jnp.abs(w["k_gain"])))
    bias_bound = 2 * LOG2E * jnp.max(jnp.abs(rel_bias))
    attend = lambda fixed: functools.partial(_prompt_attention, w=w, fixed_shift=fixed)
    o = lax.cond(score_bound + bias_bound <= SCORE_RANGE, attend(True), attend(False),
                 qT.reshape(B, nl, a, LAYOUT_TILE), kb.reshape(B, S, a),
                 vT.reshape(B, nl, a, LAYOUT_TILE), rel_bias)
    r, h_last, conv_new = _lru(xr.reshape(B, S, lw), xg.reshape(B, S, lw),
                               jnp.zeros((B, lw), F32), jnp.zeros((B, CONV_WIDTH - 1, lw), F32), w)
    y = _stage_d(x1, o.reshape(B * S, a), r.reshape(B * S, lw), w)
    return (y.reshape(B, S, D), k, v.reshape(B, S, H, V_DIM), h_last.reshape(B, lw), conv_new)


def _layer_sample(x, w, rel_bias, k_past, v_past, h0, conv0):
    B, Q, D = x.shape
    P = k_past.shape[1]
    a, lw = w["attn_width"], w["lru_width"]
    H = a // V_DIM
    near = LANES
    assert P % CHUNK == 0 and Q <= CHUNK and near >= MAX_DISTANCE and P > near
    x1, k, v, xr, xg, qb, kb, vb = _stage_a(x, w, transposed=False)
    q_pos = P + jnp.arange(Q, dtype=jnp.int32)
    idx_c = _bucket_tile(q_pos, jnp.arange(P - near, P, dtype=jnp.int32), False)
    idx_n = _bucket_tile(q_pos, q_pos, False)
    bias_c = _bias_tiles(rel_bias, idx_c[None])[:, 0]
    bias_n = _bias_tiles(rel_bias, idx_n[None])[:, 0]
    o = _sample_attention(qb.reshape(B, Q, a), kb.reshape(B, Q, a), vb.reshape(B, Q, a),
                          k_past.transpose(0, 2, 3, 4, 1).reshape(B, a, P),
                          v_past.reshape(B, P * H, V_DIM),
                          bias_c, bias_n, w)
    r, h_last, conv_new = _lru(xr.reshape(B, Q, lw), xg.reshape(B, Q, lw), h0, conv0, w)
    y = _stage_d(x1, o.reshape(B * Q, a), r.reshape(B * Q, lw), w)
    return (y.reshape(B, Q, D), k.reshape(B, Q, H, 2, HEAD_DIM), v.reshape(B, Q, H, V_DIM),
            h_last.reshape(B, lw), conv_new)


def kernel(x_prompt, x_sample, cache_k, cache_v, state_lru, state_conv, rel_bias, norm_ffn1, ffn1_gate, ffn1_up, ffn1_down, norm_mix, w_in, q_norm, k_norm, lambda_q1, lambda_k1, lambda_q2, lambda_k2, subln, conv_w, conv_b, gate_a_w, gate_a_b, gate_x_w, gate_x_b, lru_L, lru_out_norm, w_out, norm_ffn2, ffn2_gate, ffn2_up, ffn2_down):
    p = dict(norm_ffn1=norm_ffn1, ffn1_gate=ffn1_gate, ffn1_up=ffn1_up, ffn1_down=ffn1_down,
             norm_mix=norm_mix, w_in=w_in, q_norm=q_norm, k_norm=k_norm,
             lambda_q1=lambda_q1, lambda_k1=lambda_k1, lambda_q2=lambda_q2, lambda_k2=lambda_k2,
             subln=subln, conv_w=conv_w, conv_b=conv_b, gate_a_w=gate_a_w, gate_a_b=gate_a_b,
             gate_x_w=gate_x_w, gate_x_b=gate_x_b, lru_L=lru_L, lru_out_norm=lru_out_norm,
             w_out=w_out, norm_ffn2=norm_ffn2, ffn2_gate=ffn2_gate, ffn2_up=ffn2_up,
             ffn2_down=ffn2_down)
    depth = cache_k.shape[0]
    assert depth == 1, "the lambda initial value is specialised to a single layer"
    w = _prepare_weights(0, p)
    yp, kp, vp, hp, cp = _layer_prompt(x_prompt, w, rel_bias)
    ys, kn, vn, hn, cn = _layer_sample(x_sample, w, rel_bias, cache_k[0], cache_v[0],
                                       state_lru[0], state_conv[0])
    stack = lambda t: t[None]
    return (yp, ys, stack(kp), stack(vp), stack(hp), stack(cp),
            stack(kn), stack(vn), stack(hn), stack(cn))
```

```python
import functools
import math

import numpy as np
import jax
import jax.numpy as jnp
from jax import lax
from jax.experimental import pallas as pl
from jax.experimental.pallas import tpu as pltpu

F32 = jnp.float32
BF16 = jnp.bfloat16

HEAD_DIM = 64
V_DIM = 2 * HEAD_DIM
CHUNK = 64
NUM_BUCKETS = 32
MAX_DISTANCE = 128
CONV_WIDTH = 4
LRU_BLOCKS = 8
LRU_C = 8.0
EPS = 1e-6
NEG_INF = -1e30
LAM_INIT = 0.8 - 0.6 * math.exp(-0.3 * 0)

LANES = 128
SUBLANES = 8
MXU_DIM = 256
VMEM_BYTES = 64 * 1024 * 1024

FF_CHUNK = MXU_DIM
ATTN_TILE = 512
LAYOUT_TILE = 256
FAR_ROWS = 1024
FAR_TILES = 4
LOG2E = 1.4426950408889634
SCORE_RANGE = 60.0
TOKEN_TILE = 512
ROW_GROUPS = 1
LRU_TILE = 512


def _vmem_limit(nbytes):
    return int(min(VMEM_BYTES - 4 * 1024 * 1024, max(nbytes, 16 * 1024 * 1024)))


def _rms(x, g):
    ms = jnp.mean(x * x, axis=-1, keepdims=True)
    return x * lax.rsqrt(ms + EPS) * g


def _sigmoid(x):
    return 0.5 * jnp.tanh(0.5 * x) + 0.5


def _const_spec(shape):
    n = len(shape)
    return pl.BlockSpec(shape, lambda *_: (0,) * n, pipeline_mode=pl.Buffered(1))


def _swiglu_into(acc_ref, xn, wg_ref, wu_ref, wd_ref):
    d_ff = wg_ref.shape[1]
    for c in range(0, d_ff, FF_CHUNK):
        cols = slice(c, c + FF_CHUNK)
        g = jnp.dot(xn, wg_ref[:, cols], preferred_element_type=F32)
        u = jnp.dot(xn, wu_ref[:, cols], preferred_element_type=F32)
        h = (g * _sigmoid(g) * u).astype(BF16)
        part = jnp.dot(h, wd_ref[cols, :], preferred_element_type=F32)
        if c == 0:
            acc_ref[...] = part
        else:
            acc_ref[...] += part


def _stage_a_kernel(x_ref, g1_ref, wg_ref, wu_ref, wd_ref, gm_ref, win_ref, gmat_ref, qg_ref, kg_ref,
                    x1_ref, k_ref, v_ref, xr_ref, xg_ref, qb_ref, kb_ref, vb_ref, acc_ref,
                    *, attn_width, lru_width, transposed, tile, q_scale):
    x = x_ref[...]
    xn = _rms(x, g1_ref[...]).astype(BF16)
    _swiglu_into(acc_ref, xn, wg_ref, wu_ref, wd_ref)
    x1 = x + 0.5 * acc_ref[...]
    x1_ref[...] = x1
    hn = _rms(x1, gm_ref[...]).astype(BF16)
    proj = jnp.dot(hn, win_ref[...], preferred_element_type=F32)
    a = attn_width
    q = proj[:, :a]
    k = proj[:, a:2 * a]
    v = proj[:, 2 * a:3 * a]
    xr_ref[...] = proj[:, 3 * a:3 * a + lru_width]
    xg_ref[...] = proj[:, 3 * a + lru_width:]

    def group_norm(t, g):
        ms = jnp.dot((t * t).astype(BF16), gmat_ref[...], preferred_element_type=F32)
        return t * lax.rsqrt(ms + EPS) * g

    qn = group_norm(q, qg_ref[...]) * q_scale
    kn = group_norm(k, kg_ref[...])
    n_heads = a // V_DIM
    for h in range(n_heads):
        v_ref[pl.ds(h, x.shape[0], stride=n_heads), :] = v[:, h * V_DIM:(h + 1) * V_DIM]
    kb_ref[...] = kn.astype(BF16)
    if transposed:
        k_ref[0] = kn.T
        for j in range(x.shape[0] // tile):
            rows = slice(j * tile, (j + 1) * tile)
            qb_ref[0, j] = qn[rows, :].T.astype(BF16)
            vb_ref[0, j] = v[rows, :].T.astype(BF16)
    else:
        k_ref[...] = kn
        qb_ref[...] = qn.astype(BF16)
        vb_ref[...] = v.astype(BF16)


def _stage_a(x, w, *, transposed):
    B, S, D = x.shape
    a = w["attn_width"]
    lw = w["lru_width"]
    tm = TOKEN_TILE
    T = LAYOUT_TILE
    q_scale = HEAD_DIM ** -0.5 * (LOG2E if transposed else 1.0)
    assert (B * S) % tm == 0 and (not transposed or (tm % T == 0 and S % tm == 0))
    nt = S // tm
    x2 = x.reshape(B * S, D)
    d_ff = w["ffn1_gate"].shape[1]
    tok = lambda width: pl.BlockSpec((tm, width), lambda i: (i, 0))
    out_shape = [
        jax.ShapeDtypeStruct((B * S, D), F32),
        jax.ShapeDtypeStruct((B * S, a), F32),
        jax.ShapeDtypeStruct((B * S, a), F32),
        jax.ShapeDtypeStruct((B * S, lw), F32),
        jax.ShapeDtypeStruct((B * S, lw), F32),
    ]
    out_specs = [tok(D), tok(a), tok(a), tok(lw), tok(lw)]
    H = a // V_DIM
    out_shape[2] = jax.ShapeDtypeStruct((B * S * H, V_DIM), F32)
    out_specs[2] = pl.BlockSpec((tm * H, V_DIM), lambda i: (i, 0))
    if transposed:
        out_shape[1] = jax.ShapeDtypeStruct((B, a, S), F32)
        out_specs[1] = pl.BlockSpec((1, a, tm), lambda i: (i // nt, 0, i % nt))
        tshape = jax.ShapeDtypeStruct((B * nt, tm // T, a, T), BF16)
        tspec = pl.BlockSpec((1, tm // T, a, T), lambda i: (i, 0, 0, 0))
        out_shape += [tshape, jax.ShapeDtypeStruct((B * S, a), BF16), tshape]
        out_specs += [tspec, tok(a), tspec]
    else:
        bshape = jax.ShapeDtypeStruct((B * S, a), BF16)
        out_shape += [bshape, bshape, bshape]
        out_specs += [tok(a), tok(a), tok(a)]
    weight_bytes = 2 * (3 * D * d_ff + D * (3 * a + 2 * lw) + a * a)
    act_bytes = 4 * tm * (2 * 2 * D + 2 * 2 * (2 * a + 2 * lw) + D + 2 * (3 * a + 2 * lw) + 4 * FF_CHUNK)
    kern = functools.partial(_stage_a_kernel, attn_width=a, lru_width=lw, transposed=transposed,
                             tile=T, q_scale=q_scale)
    outs = pl.pallas_call(
        kern,
        grid=(B * S // tm,),
        in_specs=[
            tok(D),
            _const_spec((1, D)),
            _const_spec(w["ffn1_gate"].shape),
            _const_spec(w["ffn1_up"].shape),
            _const_spec(w["ffn1_down"].shape),
            _const_spec((1, D)),
            _const_spec(w["w_in"].shape),
            _const_spec((a, a)),
            _const_spec((1, a)),
            _const_spec((1, a)),
        ],
        out_specs=out_specs,
        out_shape=out_shape,
        scratch_shapes=[pltpu.VMEM((tm, D), F32)],
        compiler_params=pltpu.CompilerParams(
            dimension_semantics=("parallel",),
            vmem_limit_bytes=_vmem_limit(weight_bytes + act_bytes)),
        name="stage_a_t" if transposed else "stage_a_n",
    )(x2, w["norm_ffn1"], w["ffn1_gate"], w["ffn1_up"], w["ffn1_down"], w["norm_mix"], w["w_in"],
      w["group_mat"], w["q_gain"], w["k_gain"])
    return outs


def _stage_d_kernel(x1_ref, o_ref, r_ref, woa_ref, wor_ref, g2_ref, wg_ref, wu_ref, wd_ref,
                    y_ref, acc_ref, *, sub_rows):
    for r0 in range(0, x1_ref.shape[0], sub_rows):
        rows = slice(r0, r0 + sub_rows)
        x2 = (x1_ref[rows, :]
              + jnp.dot(o_ref[rows, :], woa_ref[...], preferred_element_type=F32)
              + jnp.dot(r_ref[rows, :], wor_ref[...], preferred_element_type=F32))
        xn = _rms(x2, g2_ref[...]).astype(BF16)
        acc = acc_ref.at[rows]
        _swiglu_into(acc, xn, wg_ref, wu_ref, wd_ref)
        y_ref[rows, :] = x2 + 0.5 * acc[...]


def _stage_d(x1, o, r, w):
    N, D = x1.shape
    a = o.shape[1]
    lw = r.shape[1]
    tm = min(TOKEN_TILE * ROW_GROUPS, N)
    assert N % tm == 0 and tm % TOKEN_TILE == 0
    d_ff = w["ffn2_gate"].shape[1]
    tok = lambda width: pl.BlockSpec((tm, width), lambda i: (i, 0))
    weight_bytes = 2 * (3 * D * d_ff + D * D)
    act_bytes = 4 * tm * (2 * 2 * D + 2 * (a + lw) + 2 * D + 4 * FF_CHUNK)
    return pl.pallas_call(
        functools.partial(_stage_d_kernel, sub_rows=TOKEN_TILE),
        grid=(N // tm,),
        in_specs=[
            tok(D), tok(a), tok(lw),
            _const_spec((a, D)), _const_spec((lw, D)), _const_spec((1, D)),
            _const_spec(w["ffn2_gate"].shape), _const_spec(w["ffn2_up"].shape),
            _const_spec(w["ffn2_down"].shape),
        ],
        out_specs=tok(D),
        out_shape=jax.ShapeDtypeStruct((N, D), F32),
        scratch_shapes=[pltpu.VMEM((tm, D), F32)],
        compiler_params=pltpu.CompilerParams(
            dimension_semantics=("parallel",),
            vmem_limit_bytes=_vmem_limit(weight_bytes + act_bytes)),
        name="stage_d",
    )(x1, o, r, w["w_out_attn"], w["w_out_lru"], w["norm_ffn2"],
      w["ffn2_gate"], w["ffn2_up"], w["ffn2_down"])


def _t5_bucket(rel):
    n = NUM_BUCKETS // 2
    max_exact = n // 2
    ret = jnp.where(rel > 0, n, 0)
    rel = jnp.abs(rel)
    relf = jnp.maximum(rel, 1).astype(jnp.float32)
    large = max_exact + (jnp.log(relf / max_exact) / math.log(MAX_DISTANCE / max_exact)
                         * (n - max_exact)).astype(jnp.int32)
    large = jnp.minimum(large, n - 1)
    return ret + jnp.where(rel < max_exact, rel, large)


FAR_BUCKET = NUM_BUCKETS // 2 - 1


def _bucket_tile(q_pos, k_pos, keys_first):
    rel = k_pos[None, :] - q_pos[:, None]
    visible = (k_pos[None, :] // CHUNK) <= (q_pos[:, None] // CHUNK)
    idx = jnp.where(visible, _t5_bucket(rel), -1)
    return idx.T if keys_first else idx


def _bias_kernel(table_ref, idx_ref, out_ref, *, scale):
    h = pl.program_id(0)
    idx = idx_ref[0]
    acc = jnp.zeros(idx.shape, F32)
    for b in range(NUM_BUCKETS):
        acc = jnp.where(idx == b, table_ref[b, h], acc)
    out_ref[0, 0] = jnp.where(idx < 0, NEG_INF, (acc - table_ref[FAR_BUCKET, h]) * scale)


def _bias_tiles(rel_bias, idx, scale=1.0):
    n, R, C = idx.shape
    H = rel_bias.shape[1]
    return pl.pallas_call(
        functools.partial(_bias_kernel, scale=scale),
        grid=(H, n),
        in_specs=[pl.BlockSpec(memory_space=pltpu.SMEM),
                  pl.BlockSpec((1, R, C), lambda h, i: (i, 0, 0))],
        out_specs=pl.BlockSpec((1, 1, R, C), lambda h, i: (h, i, 0, 0)),
        out_shape=jax.ShapeDtypeStruct((H, n, R, C), F32),
        name="bias_tiles",
    )(rel_bias, idx)


def _lambda(lq1_ref, lk1_ref, lq2_ref, lk2_ref):
    s1 = jnp.sum(lq1_ref[...] * lk1_ref[...], axis=-1, keepdims=True)
    s2 = jnp.sum(lq2_ref[...] * lk2_ref[...], axis=-1, keepdims=True)
    return jnp.exp(s1) - jnp.exp(s2) + LAM_INIT


def _split_components(qt):
    first = lax.broadcasted_iota(jnp.int32, qt.shape, 0) < HEAD_DIM
    zero = jnp.zeros_like(qt)
    return jnp.concatenate([jnp.where(first, qt, zero), jnp.where(first, zero, qt)], axis=1)


def _attn_finish(o, sub_ref):
    o = o * lax.rsqrt(jnp.mean(o * o, axis=0, keepdims=True) + EPS)
    return o.T * sub_ref[...] * (1.0 - LAM_INIT)


def _prompt_attn_fixed_kernel(qT_ref, k_ref, vT_ref, bias_ref, lq1_ref, lk1_ref, lq2_ref, lk2_ref,
                              sub_ref, o_ref, l_ref, acc_ref, *, tile, far_rows):
    G = qT_ref.shape[3]
    T = tile
    sub = T // G
    nq = qT_ref.shape[1] // sub
    lam = _lambda(lq1_ref, lk1_ref, lq2_ref, lk2_ref)

    def lane_tiles(ref, first, count):
        return jnp.concatenate([ref[0, first + j] for j in range(count)], axis=1)

    def probs(first_group, count, q2, bias=None):
        rows = pl.ds(pl.multiple_of(first_group * G, G), count * G)
        s = jnp.dot(k_ref[0, rows, :], q2, preferred_element_type=F32)
        if bias is not None:
            s = s + jnp.concatenate([bias, bias], axis=1)
        p = jnp.exp2(s)
        return (jnp.sum(p, axis=0, keepdims=True),
                jnp.dot(lane_tiles(vT_ref, first_group, count), p.astype(BF16),
                        preferred_element_type=F32))

    def far_groups(first_group, count, q2):
        per = min(far_rows // G, count)
        parts = [probs(first_group + j, per, q2) for j in range(0, count, per)]
        l_ref[...] += functools.reduce(lambda x, y: x + y, [p[0] for p in parts])
        acc_ref[...] += functools.reduce(lambda x, y: x + y, [p[1] for p in parts])

    def q_tile(qi, carry):
        q2 = _split_components(lane_tiles(qT_ref, qi * sub, sub))

        start = jnp.maximum(qi - 1, 0)
        bias = jnp.concatenate([bias_ref[0, jnp.where(qi > 0, 1, 0)],
                                jnp.where(qi > 0, bias_ref[0, 0], NEG_INF)], axis=0)
        l_near, acc_near = probs(start * sub, 2 * sub, q2, bias)
        l_ref[...] = l_near
        acc_ref[...] = acc_near

        n_far = jnp.maximum(qi - 1, 0)

        def far(i, c):
            far_groups(i * FAR_TILES * sub, FAR_TILES * sub, q2)
            return c

        lax.fori_loop(0, n_far // FAR_TILES, far, 0)
        done = n_far - n_far % FAR_TILES
        chunk = FAR_TILES // 2
        while chunk >= 1:
            take = (n_far - done) >= chunk

            @pl.when(take)
            def _(done=done, chunk=chunk):
                far_groups(done * sub, chunk * sub, q2)

            done = done + jnp.where(take, chunk, 0)
            chunk //= 2

        on = acc_ref[...] * (1.0 / l_ref[...])
        o = on[:, :T] - lam * on[:, T:]
        rows = pl.ds(pl.multiple_of(qi * T, T), T)
        o_ref[0, rows, :] = _attn_finish(o, sub_ref).astype(o_ref.dtype)
        return carry

    lax.fori_loop(0, nq, q_tile, 0)


def _prompt_attn_online_kernel(qT_ref, k_ref, vT_ref, bias_ref, lq1_ref, lk1_ref, lq2_ref, lk2_ref,
                               sub_ref, o_ref, m_ref, l_ref, acc_ref, *, tile):
    T = tile
    sub = T // qT_ref.shape[3]
    nq = qT_ref.shape[1] // sub
    lam = _lambda(lq1_ref, lk1_ref, lq2_ref, lk2_ref)

    def lane_tiles(ref, i):
        return jnp.concatenate([ref[0, i * sub + j] for j in range(sub)], axis=1)

    def scores(ki, q2):
        kt = k_ref[0, pl.ds(pl.multiple_of(ki * T, T), T), :]
        return jnp.dot(kt, q2, preferred_element_type=F32)

    def pv(p, ki):
        return jnp.dot(lane_tiles(vT_ref, ki), p.astype(BF16), preferred_element_type=F32)

    def update(s, ki):
        m_prev = m_ref[...]
        m_new = jnp.maximum(m_prev, jnp.max(s, axis=0, keepdims=True))
        alpha = jnp.exp2(m_prev - m_new)
        p = jnp.exp2(s - m_new)
        l_ref[...] = alpha * l_ref[...] + jnp.sum(p, axis=0, keepdims=True)
        acc_ref[...] = alpha * acc_ref[...] + pv(p, ki)
        m_ref[...] = m_new

    def q_tile(qi, carry):
        q2 = _split_components(lane_tiles(qT_ref, qi))

        bias = bias_ref[0, 0]
        s = scores(qi, q2) + jnp.concatenate([bias, bias], axis=1)
        m = jnp.max(s, axis=0, keepdims=True)
        p = jnp.exp2(s - m)
        m_ref[...] = m
        l_ref[...] = jnp.sum(p, axis=0, keepdims=True)
        acc_ref[...] = pv(p, qi)
        kp = jnp.maximum(qi - 1, 0)
        bias = jnp.where(qi > 0, bias_ref[0, 1], NEG_INF)
        s = scores(kp, q2) + jnp.concatenate([bias, bias], axis=1)
        n_far = qi - 1

        update(s, kp)

        def far(ki, s_cur):
            s_next = scores(jnp.minimum(ki + 1, jnp.maximum(n_far - 1, 0)), q2)
            update(s_cur, ki)
            return s_next

        lax.fori_loop(0, n_far, far, scores(0, q2))

        on = acc_ref[...] * (1.0 / l_ref[...])
        o = on[:, :T] - lam * on[:, T:]
        rows = pl.ds(pl.multiple_of(qi * T, T), T)
        o_ref[0, rows, :] = _attn_finish(o, sub_ref).astype(o_ref.dtype)
        return carry

    lax.fori_loop(0, nq, q_tile, 0)


def _near_bias(rel_bias, tile, offsets):
    pos = jnp.arange(tile, dtype=jnp.int32)
    base = tile * max(offsets)
    idx = [_bucket_tile(pos + base, pos + base - o * tile, True) for o in offsets]
    return _bias_tiles(rel_bias, jnp.stack(idx), LOG2E)


def _prompt_attention(qT, kb, vT, rel_bias, w, *, fixed_shift):
    T = ATTN_TILE
    B, S = kb.shape[:2]
    G = qT.shape[-1]
    A = kb.shape[-1]
    H = A // V_DIM
    assert S % T == 0 and G % CHUNK == 0 and G >= MAX_DISTANCE and T == 2 * G
    lam_spec = pl.BlockSpec((1, HEAD_DIM), lambda b, h: (0, 0))
    tile_spec = pl.BlockSpec((1, S // G, V_DIM, G), lambda b, h: (b, 0, h, 0))
    seq_spec = pl.BlockSpec((1, S, V_DIM), lambda b, h: (b, 0, h))
    bias = _near_bias(rel_bias, T, (0, 1))
    tmp_bytes = 4 * 8 * T * 2 * T
    if fixed_shift:
        kern = functools.partial(_prompt_attn_fixed_kernel, tile=T, far_rows=FAR_ROWS)
        stats = []
    else:
        kern = functools.partial(_prompt_attn_online_kernel, tile=T)
        stats = [pltpu.VMEM((1, 2 * T), F32)]
    blk_bytes = 2 * 2 * (3 * S * V_DIM) + 2 * 2 * S * V_DIM + 2 * 4 * bias[0].size
    return pl.pallas_call(
        kern,
        grid=(B, H),
        in_specs=[tile_spec, seq_spec, tile_spec,
                  pl.BlockSpec((1,) + bias.shape[1:], lambda b, h: (h, 0, 0, 0)),
                  lam_spec, lam_spec, lam_spec, lam_spec,
                  pl.BlockSpec((1, V_DIM), lambda b, h: (0, 0))],
        out_specs=seq_spec,
        out_shape=jax.ShapeDtypeStruct((B, S, A), BF16),
        scratch_shapes=stats + [pltpu.VMEM((1, 2 * T), F32), pltpu.VMEM((V_DIM, 2 * T), F32)],
        compiler_params=pltpu.CompilerParams(
            dimension_semantics=("parallel", "parallel"),
            vmem_limit_bytes=_vmem_limit(blk_bytes + tmp_bytes + 4 * V_DIM * 2 * T)),
        name="prompt_attn_fixed" if fixed_shift else "prompt_attn_online",
    )(qT, kb, vT, bias, w["lambda_q1"], w["lambda_k1"], w["lambda_q2"], w["lambda_k2"], w["subln"])


def _sample_attn_kernel(q_ref, kn_ref, vn_ref, ck_ref, cv_ref, bc_ref, bn_ref,
                        lq1_ref, lk1_ref, lq2_ref, lk2_ref, sub_ref, o_ref, *, near, shift_free):
    Q = q_ref.shape[1]
    P = ck_ref.shape[2]
    H = q_ref.shape[2] // V_DIM
    lam = _lambda(lq1_ref, lk1_ref, lq2_ref, lk2_ref)
    first = lax.broadcasted_iota(jnp.int32, (Q, V_DIM), 1) < HEAD_DIM
    nt = (((1,), (1,)), ((), ()))
    far = P - near
    for h in range(H):
        cols = slice(h * V_DIM, (h + 1) * V_DIM)
        qh = q_ref[0, :, cols]
        zero = jnp.zeros_like(qh)
        q2 = jnp.concatenate([jnp.where(first, qh, zero), jnp.where(first, zero, qh)], axis=0)
        k_far = ck_ref[0, cols, :far].astype(BF16)
        k_near = ck_ref[0, cols, far:].astype(BF16)
        bc = bc_ref[h]
        bn = bn_ref[h]
        s_far = jnp.dot(q2, k_far, preferred_element_type=F32)
        s_near = (jnp.dot(q2, k_near, preferred_element_type=F32)
                  + jnp.concatenate([bc, bc], axis=0))
        s_new = (lax.dot_general(q2, kn_ref[0, :, cols], nt, preferred_element_type=F32)
                 + jnp.concatenate([bn, bn], axis=0))
        if not shift_free:
            m = jnp.maximum(jnp.maximum(jnp.max(s_far, axis=-1, keepdims=True),
                                        jnp.max(s_near, axis=-1, keepdims=True)),
                            jnp.max(s_new, axis=-1, keepdims=True))
            s_far, s_near, s_new = s_far - m, s_near - m, s_new - m
        p_far = jnp.exp(s_far)
        p_near = jnp.exp(s_near)
        p_new = jnp.exp(s_new)
        l = (jnp.sum(p_far, axis=-1, keepdims=True) + jnp.sum(p_near, axis=-1, keepdims=True)
             + jnp.sum(p_new, axis=-1, keepdims=True))
        v_far = cv_ref[0, pl.ds(h, far, stride=H), :].astype(BF16)
        v_near = cv_ref[0, pl.ds(far * H + h, near, stride=H), :].astype(BF16)
        acc = (jnp.dot(p_far.astype(BF16), v_far, preferred_element_type=F32)
               + jnp.dot(p_near.astype(BF16), v_near, preferred_element_type=F32)
               + jnp.dot(p_new.astype(BF16), vn_ref[0, :, cols], preferred_element_type=F32))
        on = acc * (1.0 / l)
        o = on[:Q] - lam * on[Q:]
        o = o * lax.rsqrt(jnp.mean(o * o, axis=-1, keepdims=True) + EPS)
        o_ref[0, :, cols] = (o * sub_ref[...] * (1.0 - LAM_INIT)).astype(o_ref.dtype)


def _sample_attention(qb, kb, vb, cache_k, cache_v, bias_c, bias_n, w, *, shift_free):
    B, Q, A = qb.shape
    P = cache_k.shape[2]
    near = bias_c.shape[-1]
    new_spec = pl.BlockSpec((1, Q, A), lambda b: (b, 0, 0))
    ck_spec = pl.BlockSpec((1, A, P), lambda b: (b, 0, 0))
    cv_spec = pl.BlockSpec((1,) + cache_v.shape[1:], lambda b: (b, 0, 0))
    lam_spec = pl.BlockSpec((1, HEAD_DIM), lambda b: (0, 0))
    blk_bytes = 2 * 2 * 4 * P * A + 2 * 4 * 2 * Q * A
    tmp_bytes = 4 * 2 * Q * (3 * P + 2 * P) + 2 * 2 * P * V_DIM
    return pl.pallas_call(
        functools.partial(_sample_attn_kernel, near=near, shift_free=shift_free),
        grid=(B,),
        in_specs=[new_spec, new_spec, new_spec, ck_spec, cv_spec,
                  pl.BlockSpec(bias_c.shape, lambda b: (0, 0, 0)),
                  pl.BlockSpec(bias_n.shape, lambda b: (0, 0, 0)),
                  lam_spec, lam_spec, lam_spec, lam_spec,
                  pl.BlockSpec((1, V_DIM), lambda b: (0, 0))],
        out_specs=new_spec,
        out_shape=jax.ShapeDtypeStruct((B, Q, A), BF16),
        compiler_params=pltpu.CompilerParams(
            dimension_semantics=("parallel",),
            vmem_limit_bytes=_vmem_limit(blk_bytes + tmp_bytes)),
        name="sample_attn_free" if shift_free else "sample_attn_rowmax",
    )(qb, kb, vb, cache_k, cache_v, bias_c, bias_n,
      w["lambda_q1"], w["lambda_k1"], w["lambda_q2"], w["lambda_k2"], w["subln"])


def _gelu_tanh(x):
    return 0.5 * x * (1.0 + jnp.tanh(math.sqrt(2.0 / math.pi) * (x + 0.044715 * (x * x * x))))


def _lru_kernel(xr_ref, xg_ref, h0_ref, c0_ref, cw_ref, cb_ref, wgate_ref, bgate_ref, L_ref, gn_ref,
                perm_ref, r_ref, hl_ref, cn_ref, tail_ref, hc_ref):
    t = pl.program_id(1)
    ts, W = xr_ref.shape[1:]
    n = ts // SUBLANES
    tail = CONV_WIDTH - 1
    row = lax.broadcasted_iota(jnp.int32, (SUBLANES, W), 0)

    @pl.when(t == 0)
    def _():
        tail_ref[SUBLANES - tail:, :] = c0_ref[0]
        hc_ref[...] = jnp.broadcast_to(h0_ref[0], hc_ref.shape)

    def regroup(x):
        hi = x.astype(BF16)
        lo = (x - hi.astype(F32)).astype(BF16)
        return (jnp.dot(perm_ref[0], hi, preferred_element_type=F32)
                + jnp.dot(perm_ref[0], lo, preferred_element_type=F32))

    def blocks(v):
        return [v[j * SUBLANES:(j + 1) * SUBLANES] for j in range(n)]

    def shift_subsequences(block, entering):
        return jnp.where(row == 0, entering, pltpu.roll(block, 1, 0))

    xp = blocks(regroup(xr_ref[0]))
    lead = [shift_subsequences(xp[n - k], tail_ref[SUBLANES - k:SUBLANES - k + 1, :])
            for k in range(tail, 0, -1)]
    ext = lead + xp
    xc = cb_ref[...]
    for j in range(CONV_WIDTH):
        xc = xc + jnp.concatenate(ext[j:j + n], axis=0) * cw_ref[j:j + 1, :]
    tail_ref[SUBLANES - tail:, :] = xr_ref[0, ts - tail:, :]

    gates = jnp.dot(xc.astype(BF16), wgate_ref[...], preferred_element_type=F32) + bgate_ref[...]
    r = _sigmoid(gates[:, :W])
    i = _sigmoid(gates[:, W:])
    z = -L_ref[...]
    softplus = jnp.maximum(z, 0.0) + jnp.log1p(jnp.exp(-jnp.abs(z)))
    log_a = -LRU_C * r * softplus
    a = blocks(jnp.exp(log_a))
    th = jnp.tanh(log_a)
    u = -2.0 * th / (1.0 - th)
    b = blocks(jnp.where(u > 0.0, u * lax.rsqrt(u), 0.0) * (i * xc))

    end_a, end_b = a[0], b[0]
    for j in range(1, n):
        end_b = a[j] * end_b + b[j]
        end_a = a[j] * end_a
    d = 1
    while d < SUBLANES:
        keep = row >= d
        a_sh = jnp.where(keep, pltpu.roll(end_a, d, 0), 1.0)
        b_sh = jnp.where(keep, pltpu.roll(end_b, d, 0), 0.0)
        end_b = end_a * b_sh + end_b
        end_a = end_a * a_sh
        d *= 2
    h_prev = hc_ref[0:1, :]
    h_end = end_a * h_prev + end_b
    state = shift_subsequences(h_end, h_prev)
    states = []
    for j in range(n):
        state = a[j] * state + b[j]
        states.append(state)
    h = jnp.concatenate(states, axis=0)
    hc_ref[...] = jnp.broadcast_to(h_end[SUBLANES - 1:, :], hc_ref.shape)

    gate = jnp.dot(perm_ref[0], _gelu_tanh(xg_ref[0]).astype(BF16), preferred_element_type=F32)
    out = h * gate
    normed = _rms(out, gn_ref[...]).astype(BF16)
    r_ref[0] = jnp.dot(perm_ref[1], normed, preferred_element_type=F32).astype(r_ref.dtype)

    @pl.when(t == pl.num_programs(1) - 1)
    def _():
        hl_ref[0] = hc_ref[0:1, :]
        cn_ref[0] = xr_ref[0, ts - tail:, :]


def _lru(xr, xg, h0, conv0, w):
    B, S, W = xr.shape
    ts = min(LRU_TILE, S)
    assert S % ts == 0 and ts % SUBLANES == 0 and S >= CONV_WIDTH - 1
    tail = CONV_WIDTH - 1
    seq = pl.BlockSpec((1, ts, W), lambda b, t: (b, t, 0))
    per_b = lambda r: pl.BlockSpec((1, r, W), lambda b, t: (b, 0, 0))
    const = lambda shape: pl.BlockSpec(shape, lambda b, t: (0, 0))
    n = ts // SUBLANES
    time = np.arange(ts)
    perm = np.zeros((2, ts, ts), np.float32)
    perm[0, SUBLANES * (time % n) + time // n, time] = 1.0
    perm[1] = perm[0].T
    return pl.pallas_call(
        _lru_kernel,
        grid=(B, S // ts),
        in_specs=[seq, seq, per_b(1), per_b(tail),
                  const((CONV_WIDTH, W)), const((1, W)), const((W, 2 * W)), const((1, 2 * W)),
                  const((1, W)), const((1, W)),
                  pl.BlockSpec((2, ts, ts), lambda b, t: (0, 0, 0))],
        out_specs=[seq, per_b(1), per_b(tail)],
        out_shape=[jax.ShapeDtypeStruct((B, S, W), BF16),
                   jax.ShapeDtypeStruct((B, 1, W), F32),
                   jax.ShapeDtypeStruct((B, tail, W), F32)],
        scratch_shapes=[pltpu.VMEM((SUBLANES, W), F32), pltpu.VMEM((SUBLANES, W), F32)],
        compiler_params=pltpu.CompilerParams(dimension_semantics=("parallel", "arbitrary")),
        name="rg_lru",
    )(xr, xg, h0.reshape(B, 1, W), conv0, w["conv_w"], w["conv_b"], w["w_gate"], w["b_gate"],
      w["lru_L"], w["lru_out_norm"], jnp.asarray(perm, BF16))


def _block_diag(wb):
    n, c, d = wb.shape
    eye = jnp.eye(n, dtype=wb.dtype)
    return (eye[:, None, :, None] * wb[:, :, None, :]).reshape(n * c, n * d)


def _prepare_weights(l, p):
    D, d_ff = p["ffn1_gate"].shape[1:]
    assert d_ff % FF_CHUNK == 0
    lw = p["conv_w"].shape[-1]
    in_width = p["w_in"].shape[-1]
    a = (in_width - 2 * lw) // 3
    assert a % V_DIM == 0

    group = np.arange(a) // HEAD_DIM
    w = dict(attn_width=a, lru_width=lw)
    for name in ("ffn1_gate", "ffn1_up", "ffn1_down", "ffn2_gate", "ffn2_up", "ffn2_down"):
        w[name] = p[name][l].astype(BF16)
    for name in ("norm_ffn1", "norm_mix", "norm_ffn2", "lru_out_norm", "conv_b", "lru_L", "subln",
                 "lambda_q1", "lambda_k1", "lambda_q2", "lambda_k2"):
        w[name] = p[name][l][None, :]
    w["w_in"] = p["w_in"][l].astype(BF16)
    w["group_mat"] = jnp.asarray((group[:, None] == group[None, :]) / HEAD_DIM, BF16)
    w["q_gain"] = jnp.tile(p["q_norm"][l], a // HEAD_DIM)[None, :]
    w["k_gain"] = jnp.tile(p["k_norm"][l], a // HEAD_DIM)[None, :]
    w["w_out_attn"] = p["w_out"][l][:a].astype(BF16)
    w["w_out_lru"] = p["w_out"][l][a:].astype(BF16)
    w["conv_w"] = p["conv_w"][l]
    w["w_gate"] = jnp.concatenate([_block_diag(p["gate_a_w"][l]), _block_diag(p["gate_x_w"][l])],
                                  axis=1).astype(BF16)
    w["b_gate"] = jnp.concatenate([p["gate_a_b"][l], p["gate_x_b"][l]])[None, :]
    return w


def _scores_bounded(w, rel_bias):
    score_bound = (HEAD_DIM ** 0.5 * LOG2E * (1 + 2.0 ** -7)
                   * jnp.max(jnp.abs(w["q_gain"])) * jnp.max(jnp.abs(w["k_gain"])))
    bias_bound = 2 * LOG2E * jnp.max(jnp.abs(rel_bias))
    return score_bound + bias_bound <= SCORE_RANGE


def _layer_prompt(x, w, rel_bias, bounded):
    B, S, D = x.shape
    T = ATTN_TILE
    a, lw = w["attn_width"], w["lru_width"]
    H = a // V_DIM
    x1, kT, v, xr, xg, qT, kb, vT = _stage_a(x, w, transposed=True)
    k = kT.reshape(B, H, 2, HEAD_DIM, S).transpose(0, 4, 1, 2, 3)
    nl = S // LAYOUT_TILE
    attend = lambda fixed: functools.partial(_prompt_attention, w=w, fixed_shift=fixed)
    o = lax.cond(bounded, attend(True), attend(False),
                 qT.reshape(B, nl, a, LAYOUT_TILE), kb.reshape(B, S, a),
                 vT.reshape(B, nl, a, LAYOUT_TILE), rel_bias)
    r, h_last, conv_new = _lru(xr.reshape(B, S, lw), xg.reshape(B, S, lw),
                               jnp.zeros((B, lw), F32), jnp.zeros((B, CONV_WIDTH - 1, lw), F32), w)
    y = _stage_d(x1, o.reshape(B * S, a), r.reshape(B * S, lw), w)
    return (y.reshape(B, S, D), k, v.reshape(B, S, H, V_DIM), h_last.reshape(B, lw), conv_new)


def _layer_sample(x, w, rel_bias, bounded, k_past, v_past, h0, conv0):
    B, Q, D = x.shape
    P = k_past.shape[1]
    a, lw = w["attn_width"], w["lru_width"]
    H = a // V_DIM
    near = LANES
    assert P % CHUNK == 0 and Q <= CHUNK and near >= MAX_DISTANCE and P > near
    x1, k, v, xr, xg, qb, kb, vb = _stage_a(x, w, transposed=False)
    q_pos = P + jnp.arange(Q, dtype=jnp.int32)
    idx_c = _bucket_tile(q_pos, jnp.arange(P - near, P, dtype=jnp.int32), False)
    idx_n = _bucket_tile(q_pos, q_pos, False)
    bias_c = _bias_tiles(rel_bias, idx_c[None])[:, 0]
    bias_n = _bias_tiles(rel_bias, idx_n[None])[:, 0]
    attend = lambda free: functools.partial(_sample_attention, w=w, shift_free=free)
    o = lax.cond(bounded, attend(True), attend(False),
                 qb.reshape(B, Q, a), kb.reshape(B, Q, a), vb.reshape(B, Q, a),
                 k_past.transpose(0, 2, 3, 4, 1).reshape(B, a, P),
                 v_past.reshape(B, P * H, V_DIM), bias_c, bias_n)
    r, h_last, conv_new = _lru(xr.reshape(B, Q, lw), xg.reshape(B, Q, lw), h0, conv0, w)
    y = _stage_d(x1, o.reshape(B * Q, a), r.reshape(B * Q, lw), w)
    return (y.reshape(B, Q, D), k.reshape(B, Q, H, 2, HEAD_DIM), v.reshape(B, Q, H, V_DIM),
            h_last.reshape(B, lw), conv_new)


def kernel(x_prompt, x_sample, cache_k, cache_v, state_lru, state_conv, rel_bias, norm_ffn1, ffn1_gate, ffn1_up, ffn1_down, norm_mix, w_in, q_norm, k_norm, lambda_q1, lambda_k1, lambda_q2, lambda_k2, subln, conv_w, conv_b, gate_a_w, gate_a_b, gate_x_w, gate_x_b, lru_L, lru_out_norm, w_out, norm_ffn2, ffn2_gate, ffn2_up, ffn2_down):
    p = dict(norm_ffn1=norm_ffn1, ffn1_gate=ffn1_gate, ffn1_up=ffn1_up, ffn1_down=ffn1_down,
             norm_mix=norm_mix, w_in=w_in, q_norm=q_norm, k_norm=k_norm,
             lambda_q1=lambda_q1, lambda_k1=lambda_k1, lambda_q2=lambda_q2, lambda_k2=lambda_k2,
             subln=subln, conv_w=conv_w, conv_b=conv_b, gate_a_w=gate_a_w, gate_a_b=gate_a_b,
             gate_x_w=gate_x_w, gate_x_b=gate_x_b, lru_L=lru_L, lru_out_norm=lru_out_norm,
             w_out=w_out, norm_ffn2=norm_ffn2, ffn2_gate=ffn2_gate, ffn2_up=ffn2_up,
             ffn2_down=ffn2_down)
    depth = cache_k.shape[0]
    assert depth == 1, "the lambda initial value is specialised to a single layer"
    w = _prepare_weights(0, p)
    bounded = _scores_bounded(w, rel_bias)
    yp, kp, vp, hp, cp = _layer_prompt(x_prompt, w, rel_bias, bounded)
    ys, kn, vn, hn, cn = _layer_sample(x_sample, w, rel_bias, bounded, cache_k[0], cache_v[0],
                                       state_lru[0], state_conv[0])
    stack = lambda t: t[None]
    return (yp, ys, stack(kp), stack(vp), stack(hp), stack(cp),
            stack(kn), stack(vn), stack(hn), stack(cn))
```

```python
import functools
import math

import numpy as np
import jax
import jax.numpy as jnp
from jax import lax
from jax.experimental import pallas as pl
from jax.experimental.pallas import tpu as pltpu

F32 = jnp.float32
BF16 = jnp.bfloat16

HEAD_DIM = 64
V_DIM = 2 * HEAD_DIM
CHUNK = 64
NUM_BUCKETS = 32
MAX_DISTANCE = 128
CONV_WIDTH = 4
LRU_BLOCKS = 8
LRU_C = 8.0
EPS = 1e-6
NEG_INF = -1e30
LAM_INIT = 0.8 - 0.6 * math.exp(-0.3 * 0)

LANES = 128
SUBLANES = 8
MXU_DIM = 256
VMEM_BYTES = 64 * 1024 * 1024

FF_CHUNK = MXU_DIM
ATTN_TILE = 512
LAYOUT_TILE = 256
FAR_ROWS = 1024
FAR_TILES = 4
LOG2E = 1.4426950408889634
SCORE_RANGE = 60.0
TOKEN_TILE = 512
LRU_TILE = 1024
LRU_GROUP = 256


def _vmem_limit(nbytes):
    return int(min(VMEM_BYTES - 4 * 1024 * 1024, max(nbytes, 16 * 1024 * 1024)))


def _rms(x, g):
    ms = jnp.mean(x * x, axis=-1, keepdims=True)
    return x * lax.rsqrt(ms + EPS) * g


def _sigmoid(x):
    return 0.5 * jnp.tanh(0.5 * x) + 0.5


def _gelu_tanh(x):
    return 0.5 * x * (1.0 + jnp.tanh(math.sqrt(2.0 / math.pi) * (x + 0.044715 * (x * x * x))))


def _const_spec(shape):
    n = len(shape)
    return pl.BlockSpec(shape, lambda *_: (0,) * n, pipeline_mode=pl.Buffered(1))


def _swiglu_into(acc_ref, xn, wg_ref, wu_ref, wd_ref):
    d_ff = wg_ref.shape[1]
    for c in range(0, d_ff, FF_CHUNK):
        cols = slice(c, c + FF_CHUNK)
        g = jnp.dot(xn, wg_ref[:, cols], preferred_element_type=F32)
        u = jnp.dot(xn, wu_ref[:, cols], preferred_element_type=F32)
        h = (g * _sigmoid(g) * u).astype(BF16)
        part = jnp.dot(h, wd_ref[cols, :], preferred_element_type=F32)
        if c == 0:
            acc_ref[...] = part
        else:
            acc_ref[...] += part


def _stage_a_kernel(x_ref, g1_ref, wg_ref, wu_ref, wd_ref, gm_ref, win_ref, gmat_ref, qg_ref, kg_ref,
                    x1_ref, k_ref, v_ref, xr_ref, xg_ref, qb_ref, kb_ref, vb_ref, acc_ref,
                    *, attn_width, lru_width, transposed, tile, q_scale):
    x = x_ref[...]
    xn = _rms(x, g1_ref[...]).astype(BF16)
    _swiglu_into(acc_ref, xn, wg_ref, wu_ref, wd_ref)
    x1 = x + 0.5 * acc_ref[...]
    x1_ref[...] = x1
    hn = _rms(x1, gm_ref[...]).astype(BF16)
    proj = jnp.dot(hn, win_ref[...], preferred_element_type=F32)
    a = attn_width
    q = proj[:, :a]
    k = proj[:, a:2 * a]
    v = proj[:, 2 * a:3 * a]
    xr_ref[...] = proj[:, 3 * a:3 * a + lru_width]
    xg_ref[...] = _gelu_tanh(proj[:, 3 * a + lru_width:]).astype(BF16)

    def group_norm(t, g):
        ms = jnp.dot((t * t).astype(BF16), gmat_ref[...], preferred_element_type=F32)
        return t * lax.rsqrt(ms + EPS) * g

    qn = group_norm(q, qg_ref[...]) * q_scale
    kn = group_norm(k, kg_ref[...])
    n_heads = a // V_DIM
    for h in range(n_heads):
        v_ref[pl.ds(h, x.shape[0], stride=n_heads), :] = v[:, h * V_DIM:(h + 1) * V_DIM]
    kb_ref[...] = kn.astype(BF16)
    if transposed:
        k_ref[0] = kn.T
        for j in range(x.shape[0] // tile):
            rows = slice(j * tile, (j + 1) * tile)
            qb_ref[0, j] = qn[rows, :].T.astype(BF16)
            vb_ref[0, j] = v[rows, :].T.astype(BF16)
    else:
        k_ref[...] = kn
        qb_ref[...] = qn.astype(BF16)
        vb_ref[...] = v.astype(BF16)


def _stage_a(x, w, *, transposed):
    B, S, D = x.shape
    a = w["attn_width"]
    lw = w["lru_width"]
    tm = TOKEN_TILE
    T = LAYOUT_TILE
    q_scale = HEAD_DIM ** -0.5 * (LOG2E if transposed else 1.0)
    assert (B * S) % tm == 0 and (not transposed or (tm % T == 0 and S % tm == 0))
    nt = S // tm
    x2 = x.reshape(B * S, D)
    d_ff = w["ffn1_gate"].shape[1]
    tok = lambda width: pl.BlockSpec((tm, width), lambda i: (i, 0))
    out_shape = [
        jax.ShapeDtypeStruct((B * S, D), F32),
        jax.ShapeDtypeStruct((B * S, a), F32),
        jax.ShapeDtypeStruct((B * S, a), F32),
        jax.ShapeDtypeStruct((B * S, lw), F32),
        jax.ShapeDtypeStruct((B * S, lw), BF16),
    ]
    out_specs = [tok(D), tok(a), tok(a), tok(lw), tok(lw)]
    H = a // V_DIM
    out_shape[2] = jax.ShapeDtypeStruct((B * S * H, V_DIM), F32)
    out_specs[2] = pl.BlockSpec((tm * H, V_DIM), lambda i: (i, 0))
    if transposed:
        out_shape[1] = jax.ShapeDtypeStruct((B, a, S), F32)
        out_specs[1] = pl.BlockSpec((1, a, tm), lambda i: (i // nt, 0, i % nt))
        tshape = jax.ShapeDtypeStruct((B * nt, tm // T, a, T), BF16)
        tspec = pl.BlockSpec((1, tm // T, a, T), lambda i: (i, 0, 0, 0))
        out_shape += [tshape, jax.ShapeDtypeStruct((B * S, a), BF16), tshape]
        out_specs += [tspec, tok(a), tspec]
    else:
        bshape = jax.ShapeDtypeStruct((B * S, a), BF16)
        out_shape += [bshape, bshape, bshape]
        out_specs += [tok(a), tok(a), tok(a)]
    weight_bytes = 2 * (3 * D * d_ff + D * (3 * a + 2 * lw) + a * a)
    act_bytes = 4 * tm * (2 * 2 * D + 2 * 2 * (2 * a + 2 * lw) + D + 2 * (3 * a + 2 * lw) + 4 * FF_CHUNK)
    kern = functools.partial(_stage_a_kernel, attn_width=a, lru_width=lw, transposed=transposed,
                             tile=T, q_scale=q_scale)
    outs = pl.pallas_call(
        kern,
        grid=(B * S // tm,),
        in_specs=[
            tok(D),
            _const_spec((1, D)),
            _const_spec(w["ffn1_gate"].shape),
            _const_spec(w["ffn1_up"].shape),
            _const_spec(w["ffn1_down"].shape),
            _const_spec((1, D)),
            _const_spec(w["w_in"].shape),
            _const_spec((a, a)),
            _const_spec((1, a)),
            _const_spec((1, a)),
        ],
        out_specs=out_specs,
        out_shape=out_shape,
        scratch_shapes=[pltpu.VMEM((tm, D), F32)],
        compiler_params=pltpu.CompilerParams(
            dimension_semantics=("parallel",),
            vmem_limit_bytes=_vmem_limit(weight_bytes + act_bytes)),
        name="stage_a_t" if transposed else "stage_a_n",
    )(x2, w["norm_ffn1"], w["ffn1_gate"], w["ffn1_up"], w["ffn1_down"], w["norm_mix"], w["w_in"],
      w["group_mat"], w["q_gain"], w["k_gain"])
    return outs


def _stage_d_kernel(x1_ref, o_ref, r_ref, woa_ref, wor_ref, g2_ref, wg_ref, wu_ref, wd_ref,
                    y_ref, acc_ref):
    x2 = (x1_ref[...]
          + jnp.dot(o_ref[...], woa_ref[...], preferred_element_type=F32)
          + jnp.dot(r_ref[...], wor_ref[...], preferred_element_type=F32))
    xn = _rms(x2, g2_ref[...]).astype(BF16)
    _swiglu_into(acc_ref, xn, wg_ref, wu_ref, wd_ref)
    y_ref[...] = x2 + 0.5 * acc_ref[...]


def _stage_d(x1, o, r, w):
    N, D = x1.shape
    a = o.shape[1]
    lw = r.shape[1]
    tm = min(TOKEN_TILE, N)
    assert N % tm == 0
    d_ff = w["ffn2_gate"].shape[1]
    tok = lambda width: pl.BlockSpec((tm, width), lambda i: (i, 0))
    weight_bytes = 2 * (3 * D * d_ff + D * D)
    act_bytes = 4 * tm * (2 * 2 * D + 2 * (a + lw) + 2 * D + 4 * FF_CHUNK)
    return pl.pallas_call(
        _stage_d_kernel,
        grid=(N // tm,),
        in_specs=[
            tok(D), tok(a), tok(lw),
            _const_spec((a, D)), _const_spec((lw, D)), _const_spec((1, D)),
            _const_spec(w["ffn2_gate"].shape), _const_spec(w["ffn2_up"].shape),
            _const_spec(w["ffn2_down"].shape),
        ],
        out_specs=tok(D),
        out_shape=jax.ShapeDtypeStruct((N, D), F32),
        scratch_shapes=[pltpu.VMEM((tm, D), F32)],
        compiler_params=pltpu.CompilerParams(
            dimension_semantics=("parallel",),
            vmem_limit_bytes=_vmem_limit(weight_bytes + act_bytes)),
        name="stage_d",
    )(x1, o, r, w["w_out_attn"], w["w_out_lru"], w["norm_ffn2"],
      w["ffn2_gate"], w["ffn2_up"], w["ffn2_down"])


def _t5_bucket(rel):
    n = NUM_BUCKETS // 2
    max_exact = n // 2
    ret = jnp.where(rel > 0, n, 0)
    rel = jnp.abs(rel)
    relf = jnp.maximum(rel, 1).astype(jnp.float32)
    large = max_exact + (jnp.log(relf / max_exact) / math.log(MAX_DISTANCE / max_exact)
                         * (n - max_exact)).astype(jnp.int32)
    large = jnp.minimum(large, n - 1)
    return ret + jnp.where(rel < max_exact, rel, large)


FAR_BUCKET = NUM_BUCKETS // 2 - 1


def _bucket_tile(q_pos, k_pos, keys_first):
    rel = k_pos[None, :] - q_pos[:, None]
    visible = (k_pos[None, :] // CHUNK) <= (q_pos[:, None] // CHUNK)
    idx = jnp.where(visible, _t5_bucket(rel), -1)
    return idx.T if keys_first else idx


def _bias_kernel(table_ref, idx_ref, out_ref, *, scale):
    h = pl.program_id(0)
    idx = idx_ref[0]
    acc = jnp.zeros(idx.shape, F32)
    for b in range(NUM_BUCKETS):
        acc = jnp.where(idx == b, table_ref[b, h], acc)
    out_ref[0, 0] = jnp.where(idx < 0, NEG_INF, (acc - table_ref[FAR_BUCKET, h]) * scale)


def _bias_tiles(rel_bias, idx, scale=1.0):
    n, R, C = idx.shape
    H = rel_bias.shape[1]
    return pl.pallas_call(
        functools.partial(_bias_kernel, scale=scale),
        grid=(H, n),
        in_specs=[pl.BlockSpec(memory_space=pltpu.SMEM),
                  pl.BlockSpec((1, R, C), lambda h, i: (i, 0, 0))],
        out_specs=pl.BlockSpec((1, 1, R, C), lambda h, i: (h, i, 0, 0)),
        out_shape=jax.ShapeDtypeStruct((H, n, R, C), F32),
        name="bias_tiles",
    )(rel_bias, idx)


def _lambda(lq1_ref, lk1_ref, lq2_ref, lk2_ref):
    s1 = jnp.sum(lq1_ref[...] * lk1_ref[...], axis=-1, keepdims=True)
    s2 = jnp.sum(lq2_ref[...] * lk2_ref[...], axis=-1, keepdims=True)
    return jnp.exp(s1) - jnp.exp(s2) + LAM_INIT


def _split_components(qt):
    first = lax.broadcasted_iota(jnp.int32, qt.shape, 0) < HEAD_DIM
    zero = jnp.zeros_like(qt)
    return jnp.concatenate([jnp.where(first, qt, zero), jnp.where(first, zero, qt)], axis=1)


def _attn_finish(o, sub_ref):
    o = o * lax.rsqrt(jnp.mean(o * o, axis=0, keepdims=True) + EPS)
    return o.T * sub_ref[...] * (1.0 - LAM_INIT)


def _prompt_attn_fixed_kernel(qT_ref, k_ref, vT_ref, bias_ref, lq1_ref, lk1_ref, lq2_ref, lk2_ref,
                              sub_ref, o_ref, l_ref, acc_ref, *, tile, far_rows):
    G = qT_ref.shape[3]
    T = tile
    sub = T // G
    nq = qT_ref.shape[1] // sub
    lam = _lambda(lq1_ref, lk1_ref, lq2_ref, lk2_ref)

    def lane_tiles(ref, first, count):
        return jnp.concatenate([ref[0, first + j] for j in range(count)], axis=1)

    def probs(first_group, count, q2, bias=None):
        rows = pl.ds(pl.multiple_of(first_group * G, G), count * G)
        s = jnp.dot(k_ref[0, rows, :], q2, preferred_element_type=F32)
        if bias is not None:
            s = s + jnp.concatenate([bias, bias], axis=1)
        p = jnp.exp2(s)
        return (jnp.sum(p, axis=0, keepdims=True),
                jnp.dot(lane_tiles(vT_ref, first_group, count), p.astype(BF16),
                        preferred_element_type=F32))

    def far_groups(first_group, count, q2):
        per = min(far_rows // G, count)
        parts = [probs(first_group + j, per, q2) for j in range(0, count, per)]
        l_ref[...] += functools.reduce(lambda x, y: x + y, [p[0] for p in parts])
        acc_ref[...] += functools.reduce(lambda x, y: x + y, [p[1] for p in parts])

    def q_tile(qi, carry):
        q2 = _split_components(lane_tiles(qT_ref, qi * sub, sub))

        start = jnp.maximum(qi - 1, 0)
        bias = jnp.concatenate([bias_ref[0, jnp.where(qi > 0, 1, 0)],
                                jnp.where(qi > 0, bias_ref[0, 0], NEG_INF)], axis=0)
        l_near, acc_near = probs(start * sub, 2 * sub, q2, bias)
        l_ref[...] = l_near
        acc_ref[...] = acc_near

        n_far = jnp.maximum(qi - 1, 0)

        def far(i, c):
            far_groups(i * FAR_TILES * sub, FAR_TILES * sub, q2)
            return c

        lax.fori_loop(0, n_far // FAR_TILES, far, 0)
        done = n_far - n_far % FAR_TILES
        chunk = FAR_TILES // 2
        while chunk >= 1:
            take = (n_far - done) >= chunk

            @pl.when(take)
            def _(done=done, chunk=chunk):
                far_groups(done * sub, chunk * sub, q2)

            done = done + jnp.where(take, chunk, 0)
            chunk //= 2

        on = acc_ref[...] * (1.0 / l_ref[...])
        o = on[:, :T] - lam * on[:, T:]
        rows = pl.ds(pl.multiple_of(qi * T, T), T)
        o_ref[0, rows, :] = _attn_finish(o, sub_ref).astype(o_ref.dtype)
        return carry

    lax.fori_loop(0, nq, q_tile, 0)


def _prompt_attn_online_kernel(qT_ref, k_ref, vT_ref, bias_ref, lq1_ref, lk1_ref, lq2_ref, lk2_ref,
                               sub_ref, o_ref, m_ref, l_ref, acc_ref, *, tile):
    T = tile
    sub = T // qT_ref.shape[3]
    nq = qT_ref.shape[1] // sub
    lam = _lambda(lq1_ref, lk1_ref, lq2_ref, lk2_ref)

    def lane_tiles(ref, i):
        return jnp.concatenate([ref[0, i * sub + j] for j in range(sub)], axis=1)

    def scores(ki, q2):
        kt = k_ref[0, pl.ds(pl.multiple_of(ki * T, T), T), :]
        return jnp.dot(kt, q2, preferred_element_type=F32)

    def pv(p, ki):
        return jnp.dot(lane_tiles(vT_ref, ki), p.astype(BF16), preferred_element_type=F32)

    def update(s, ki):
        m_prev = m_ref[...]
        m_new = jnp.maximum(m_prev, jnp.max(s, axis=0, keepdims=True))
        alpha = jnp.exp2(m_prev - m_new)
        p = jnp.exp2(s - m_new)
        l_ref[...] = alpha * l_ref[...] + jnp.sum(p, axis=0, keepdims=True)
        acc_ref[...] = alpha * acc_ref[...] + pv(p, ki)
        m_ref[...] = m_new

    def q_tile(qi, carry):
        q2 = _split_components(lane_tiles(qT_ref, qi))

        bias = bias_ref[0, 0]
        s = scores(qi, q2) + jnp.concatenate([bias, bias], axis=1)
        m = jnp.max(s, axis=0, keepdims=True)
        p = jnp.exp2(s - m)
        m_ref[...] = m
        l_ref[...] = jnp.sum(p, axis=0, keepdims=True)
        acc_ref[...] = pv(p, qi)
        kp = jnp.maximum(qi - 1, 0)
        bias = jnp.where(qi > 0, bias_ref[0, 1], NEG_INF)
        s = scores(kp, q2) + jnp.concatenate([bias, bias], axis=1)
        n_far = qi - 1

        update(s, kp)

        def far(ki, s_cur):
            s_next = scores(jnp.minimum(ki + 1, jnp.maximum(n_far - 1, 0)), q2)
            update(s_cur, ki)
            return s_next

        lax.fori_loop(0, n_far, far, scores(0, q2))

        on = acc_ref[...] * (1.0 / l_ref[...])
        o = on[:, :T] - lam * on[:, T:]
        rows = pl.ds(pl.multiple_of(qi * T, T), T)
        o_ref[0, rows, :] = _attn_finish(o, sub_ref).astype(o_ref.dtype)
        return carry

    lax.fori_loop(0, nq, q_tile, 0)


def _near_bias(rel_bias, tile, offsets):
    pos = jnp.arange(tile, dtype=jnp.int32)
    base = tile * max(offsets)
    idx = [_bucket_tile(pos + base, pos + base - o * tile, True) for o in offsets]
    return _bias_tiles(rel_bias, jnp.stack(idx), LOG2E)


def _prompt_attention(qT, kb, vT, rel_bias, w, *, fixed_shift):
    T = ATTN_TILE
    B, S = kb.shape[:2]
    G = qT.shape[-1]
    A = kb.shape[-1]
    H = A // V_DIM
    assert S % T == 0 and G % CHUNK == 0 and G >= MAX_DISTANCE and T == 2 * G
    lam_spec = pl.BlockSpec((1, HEAD_DIM), lambda b, h: (0, 0))
    tile_spec = pl.BlockSpec((1, S // G, V_DIM, G), lambda b, h: (b, 0, h, 0))
    seq_spec = pl.BlockSpec((1, S, V_DIM), lambda b, h: (b, 0, h))
    bias = _near_bias(rel_bias, T, (0, 1))
    tmp_bytes = 4 * 8 * T * 2 * T
    if fixed_shift:
        kern = functools.partial(_prompt_attn_fixed_kernel, tile=T, far_rows=FAR_ROWS)
        stats = []
    else:
        kern = functools.partial(_prompt_attn_online_kernel, tile=T)
        stats = [pltpu.VMEM((1, 2 * T), F32)]
    blk_bytes = 2 * 2 * (3 * S * V_DIM) + 2 * 2 * S * V_DIM + 2 * 4 * bias[0].size
    return pl.pallas_call(
        kern,
        grid=(B, H),
        in_specs=[tile_spec, seq_spec, tile_spec,
                  pl.BlockSpec((1,) + bias.shape[1:], lambda b, h: (h, 0, 0, 0)),
                  lam_spec, lam_spec, lam_spec, lam_spec,
                  pl.BlockSpec((1, V_DIM), lambda b, h: (0, 0))],
        out_specs=seq_spec,
        out_shape=jax.ShapeDtypeStruct((B, S, A), BF16),
        scratch_shapes=stats + [pltpu.VMEM((1, 2 * T), F32), pltpu.VMEM((V_DIM, 2 * T), F32)],
        compiler_params=pltpu.CompilerParams(
            dimension_semantics=("parallel", "parallel"),
            vmem_limit_bytes=_vmem_limit(blk_bytes + tmp_bytes + 4 * V_DIM * 2 * T)),
        name="prompt_attn_fixed" if fixed_shift else "prompt_attn_online",
    )(qT, kb, vT, bias, w["lambda_q1"], w["lambda_k1"], w["lambda_q2"], w["lambda_k2"], w["subln"])


def _sample_attn_kernel(q_ref, kn_ref, vn_ref, ck_ref, cv_ref, bc_ref, bn_ref,
                        lq1_ref, lk1_ref, lq2_ref, lk2_ref, sub_ref, o_ref, *, near, shift_free):
    Q = q_ref.shape[1]
    P = ck_ref.shape[2]
    H = q_ref.shape[2] // V_DIM
    lam = _lambda(lq1_ref, lk1_ref, lq2_ref, lk2_ref)
    first = lax.broadcasted_iota(jnp.int32, (Q, V_DIM), 1) < HEAD_DIM
    nt = (((1,), (1,)), ((), ()))
    far = P - near
    for h in range(H):
        cols = slice(h * V_DIM, (h + 1) * V_DIM)
        qh = q_ref[0, :, cols]
        zero = jnp.zeros_like(qh)
        q2 = jnp.concatenate([jnp.where(first, qh, zero), jnp.where(first, zero, qh)], axis=0)
        k_far = ck_ref[0, cols, :far].astype(BF16)
        k_near = ck_ref[0, cols, far:].astype(BF16)
        bc = bc_ref[h]
        bn = bn_ref[h]
        s_far = jnp.dot(q2, k_far, preferred_element_type=F32)
        s_near = (jnp.dot(q2, k_near, preferred_element_type=F32)
                  + jnp.concatenate([bc, bc], axis=0))
        s_new = (lax.dot_general(q2, kn_ref[0, :, cols], nt, preferred_element_type=F32)
                 + jnp.concatenate([bn, bn], axis=0))
        if not shift_free:
            m = jnp.maximum(jnp.maximum(jnp.max(s_far, axis=-1, keepdims=True),
                                        jnp.max(s_near, axis=-1, keepdims=True)),
                            jnp.max(s_new, axis=-1, keepdims=True))
            s_far, s_near, s_new = s_far - m, s_near - m, s_new - m
        p_far = jnp.exp(s_far)
        p_near = jnp.exp(s_near)
        p_new = jnp.exp(s_new)
        l = (jnp.sum(p_far, axis=-1, keepdims=True) + jnp.sum(p_near, axis=-1, keepdims=True)
             + jnp.sum(p_new, axis=-1, keepdims=True))
        v_far = cv_ref[0, pl.ds(h, far, stride=H), :].astype(BF16)
        v_near = cv_ref[0, pl.ds(far * H + h, near, stride=H), :].astype(BF16)
        acc = (jnp.dot(p_far.astype(BF16), v_far, preferred_element_type=F32)
               + jnp.dot(p_near.astype(BF16), v_near, preferred_element_type=F32)
               + jnp.dot(p_new.astype(BF16), vn_ref[0, :, cols], preferred_element_type=F32))
        on = acc * (1.0 / l)
        o = on[:Q] - lam * on[Q:]
        o = o * lax.rsqrt(jnp.mean(o * o, axis=-1, keepdims=True) + EPS)
        o_ref[0, :, cols] = (o * sub_ref[...] * (1.0 - LAM_INIT)).astype(o_ref.dtype)


def _sample_attention(qb, kb, vb, cache_k, cache_v, bias_c, bias_n, w, *, shift_free):
    B, Q, A = qb.shape
    P = cache_k.shape[2]
    near = bias_c.shape[-1]
    new_spec = pl.BlockSpec((1, Q, A), lambda b: (b, 0, 0))
    ck_spec = pl.BlockSpec((1, A, P), lambda b: (b, 0, 0))
    cv_spec = pl.BlockSpec((1,) + cache_v.shape[1:], lambda b: (b, 0, 0))
    lam_spec = pl.BlockSpec((1, HEAD_DIM), lambda b: (0, 0))
    blk_bytes = 2 * 2 * 4 * P * A + 2 * 4 * 2 * Q * A
    tmp_bytes = 4 * 2 * Q * (3 * P + 2 * P) + 2 * 2 * P * V_DIM
    return pl.pallas_call(
        functools.partial(_sample_attn_kernel, near=near, shift_free=shift_free),
        grid=(B,),
        in_specs=[new_spec, new_spec, new_spec, ck_spec, cv_spec,
                  pl.BlockSpec(bias_c.shape, lambda b: (0, 0, 0)),
                  pl.BlockSpec(bias_n.shape, lambda b: (0, 0, 0)),
                  lam_spec, lam_spec, lam_spec, lam_spec,
                  pl.BlockSpec((1, V_DIM), lambda b: (0, 0))],
        out_specs=new_spec,
        out_shape=jax.ShapeDtypeStruct((B, Q, A), BF16),
        compiler_params=pltpu.CompilerParams(
            dimension_semantics=("parallel",),
            vmem_limit_bytes=_vmem_limit(blk_bytes + tmp_bytes)),
        name="sample_attn_free" if shift_free else "sample_attn_rowmax",
    )(qb, kb, vb, cache_k, cache_v, bias_c, bias_n,
      w["lambda_q1"], w["lambda_k1"], w["lambda_q2"], w["lambda_k2"], w["subln"])


def _lru_kernel(xr_ref, gate_ref, h0_ref, c0_ref, cw_ref, cb_ref, wgate_ref, bgate_ref, L_ref, gn_ref,
                perm_ref, r_ref, hl_ref, cn_ref, tail_ref, hc_ref):
    t = pl.program_id(1)
    ts, W = xr_ref.shape[1:]
    gs = perm_ref.shape[1]
    n = gs // SUBLANES
    tail = CONV_WIDTH - 1
    row = lax.broadcasted_iota(jnp.int32, (SUBLANES, W), 0)

    @pl.when(t == 0)
    def _():
        tail_ref[SUBLANES - tail:, :] = c0_ref[0]
        hc_ref[...] = jnp.broadcast_to(h0_ref[0], hc_ref.shape)

    def regroup(x):
        hi = x.astype(BF16)
        lo = (x - hi.astype(F32)).astype(BF16)
        return (jnp.dot(perm_ref[0], hi, preferred_element_type=F32)
                + jnp.dot(perm_ref[0], lo, preferred_element_type=F32))

    def blocks(v):
        return [v[j * SUBLANES:(j + 1) * SUBLANES] for j in range(n)]

    def shift_subsequences(block, entering):
        return jnp.where(row == 0, entering, pltpu.roll(block, 1, 0))

    z = -L_ref[...]
    softplus = jnp.maximum(z, 0.0) + jnp.log1p(jnp.exp(-jnp.abs(z)))

    def coefficients(rows, before):
        x = xr_ref[0, rows, :]
        xp = blocks(regroup(x))
        lead = [shift_subsequences(xp[n - k], before[tail - k:tail - k + 1, :])
                for k in range(tail, 0, -1)]
        ext = lead + xp
        xc = cb_ref[...]
        for j in range(CONV_WIDTH):
            xc = xc + jnp.concatenate(ext[j:j + n], axis=0) * cw_ref[j:j + 1, :]

        gates = (jnp.dot(xc.astype(BF16), wgate_ref[...], preferred_element_type=F32)
                 + bgate_ref[...])
        r = _sigmoid(gates[:, :W])
        i = _sigmoid(gates[:, W:])
        log_a = -LRU_C * r * softplus
        a = blocks(jnp.exp(log_a))
        th = jnp.tanh(log_a)
        u = -2.0 * th / (1.0 - th)
        b = blocks(jnp.where(u > 0.0, u * lax.rsqrt(u), 0.0) * (i * xc))
        return x[gs - tail:, :], a, b

    def recur(rows, a, b, h_prev):
        end_a, end_b = a[0], b[0]
        for j in range(1, n):
            end_b = a[j] * end_b + b[j]
            end_a = a[j] * end_a
        d = 1
        while d < SUBLANES:
            keep = row >= d
            a_sh = jnp.where(keep, pltpu.roll(end_a, d, 0), 1.0)
            b_sh = jnp.where(keep, pltpu.roll(end_b, d, 0), 0.0)
            end_b = end_a * b_sh + end_b
            end_a = end_a * a_sh
            d *= 2
        h_end = end_a * h_prev + end_b
        state = shift_subsequences(h_end, h_prev)
        states = []
        for j in range(n):
            state = a[j] * state + b[j]
            states.append(state)
        h = jnp.concatenate(states, axis=0)

        gate = jnp.dot(perm_ref[0], gate_ref[0, rows, :], preferred_element_type=F32)
        normed = _rms(h * gate, gn_ref[...]).astype(BF16)
        r_ref[0, rows, :] = jnp.dot(perm_ref[1], normed,
                                    preferred_element_type=F32).astype(r_ref.dtype)
        return h_end[SUBLANES - 1:, :]

    before = tail_ref[SUBLANES - tail:, :]
    h_prev = hc_ref[0:1, :]
    groups = []
    for r0 in range(0, ts, gs):
        rows = slice(r0, r0 + gs)
        before, a, b = coefficients(rows, before)
        groups.append((rows, a, b))
    for rows, a, b in groups:
        h_prev = recur(rows, a, b, h_prev)
    tail_ref[SUBLANES - tail:, :] = before
    hc_ref[...] = jnp.broadcast_to(h_prev, hc_ref.shape)

    @pl.when(t == pl.num_programs(1) - 1)
    def _():
        hl_ref[0] = h_prev
        cn_ref[0] = before


def _lru(xr, xg, h0, conv0, w):
    B, S, W = xr.shape
    ts = min(LRU_TILE, S)
    assert S % ts == 0 and ts % SUBLANES == 0 and S >= CONV_WIDTH - 1
    tail = CONV_WIDTH - 1
    seq = pl.BlockSpec((1, ts, W), lambda b, t: (b, t, 0))
    per_b = lambda r: pl.BlockSpec((1, r, W), lambda b, t: (b, 0, 0))
    const = lambda shape: pl.BlockSpec(shape, lambda b, t: (0, 0))
    gs = min(LRU_GROUP, ts)
    assert ts % gs == 0 and gs % SUBLANES == 0 and gs >= SUBLANES * (CONV_WIDTH - 1)
    n = gs // SUBLANES
    time = np.arange(gs)
    perm = np.zeros((2, gs, gs), np.float32)
    perm[0, SUBLANES * (time % n) + time // n, time] = 1.0
    perm[1] = perm[0].T
    return pl.pallas_call(
        _lru_kernel,
        grid=(B, S // ts),
        in_specs=[seq, seq, per_b(1), per_b(tail),
                  const((CONV_WIDTH, W)), const((1, W)), const((W, 2 * W)), const((1, 2 * W)),
                  const((1, W)), const((1, W)),
                  pl.BlockSpec((2, gs, gs), lambda b, t: (0, 0, 0))],
        out_specs=[seq, per_b(1), per_b(tail)],
        out_shape=[jax.ShapeDtypeStruct((B, S, W), BF16),
                   jax.ShapeDtypeStruct((B, 1, W), F32),
                   jax.ShapeDtypeStruct((B, tail, W), F32)],
        scratch_shapes=[pltpu.VMEM((SUBLANES, W), F32), pltpu.VMEM((SUBLANES, W), F32)],
        compiler_params=pltpu.CompilerParams(dimension_semantics=("parallel", "arbitrary")),
        name="rg_lru",
    )(xr, xg, h0.reshape(B, 1, W), conv0, w["conv_w"], w["conv_b"], w["w_gate"], w["b_gate"],
      w["lru_L"], w["lru_out_norm"], jnp.asarray(perm, BF16))


def _block_diag(wb):
    n, c, d = wb.shape
    eye = jnp.eye(n, dtype=wb.dtype)
    return (eye[:, None, :, None] * wb[:, :, None, :]).reshape(n * c, n * d)


def _prepare_weights(l, p):
    D, d_ff = p["ffn1_gate"].shape[1:]
    assert d_ff % FF_CHUNK == 0
    lw = p["conv_w"].shape[-1]
    in_width = p["w_in"].shape[-1]
    a = (in_width - 2 * lw) // 3
    assert a % V_DIM == 0

    group = np.arange(a) // HEAD_DIM
    w = dict(attn_width=a, lru_width=lw)
    for name in ("ffn1_gate", "ffn1_up", "ffn1_down", "ffn2_gate", "ffn2_up", "ffn2_down"):
        w[name] = p[name][l].astype(BF16)
    for name in ("norm_ffn1", "norm_mix", "norm_ffn2", "lru_out_norm", "conv_b", "lru_L", "subln",
                 "lambda_q1", "lambda_k1", "lambda_q2", "lambda_k2"):
        w[name] = p[name][l][None, :]
    w["w_in"] = p["w_in"][l].astype(BF16)
    w["group_mat"] = jnp.asarray((group[:, None] == group[None, :]) / HEAD_DIM, BF16)
    w["q_gain"] = jnp.tile(p["q_norm"][l], a // HEAD_DIM)[None, :]
    w["k_gain"] = jnp.tile(p["k_norm"][l], a // HEAD_DIM)[None, :]
    w["w_out_attn"] = p["w_out"][l][:a].astype(BF16)
    w["w_out_lru"] = p["w_out"][l][a:].astype(BF16)
    w["conv_w"] = p["conv_w"][l]
    w["w_gate"] = jnp.concatenate([_block_diag(p["gate_a_w"][l]), _block_diag(p["gate_x_w"][l])],
                                  axis=1).astype(BF16)
    w["b_gate"] = jnp.concatenate([p["gate_a_b"][l], p["gate_x_b"][l]])[None, :]
    return w


def _scores_bounded(w, rel_bias):
    score_bound = (HEAD_DIM ** 0.5 * LOG2E * (1 + 2.0 ** -7)
                   * jnp.max(jnp.abs(w["q_gain"])) * jnp.max(jnp.abs(w["k_gain"])))
    bias_bound = 2 * LOG2E * jnp.max(jnp.abs(rel_bias))
    return score_bound + bias_bound <= SCORE_RANGE


def _layer_prompt(x, w, rel_bias, bounded):
    B, S, D = x.shape
    T = ATTN_TILE
    a, lw = w["attn_width"], w["lru_width"]
    H = a // V_DIM
    x1, kT, v, xr, xg, qT, kb, vT = _stage_a(x, w, transposed=True)
    k = kT.reshape(B, H, 2, HEAD_DIM, S).transpose(0, 4, 1, 2, 3)
    nl = S // LAYOUT_TILE
    attend = lambda fixed: functools.partial(_prompt_attention, w=w, fixed_shift=fixed)
    o = lax.cond(bounded, attend(True), attend(False),
                 qT.reshape(B, nl, a, LAYOUT_TILE), kb.reshape(B, S, a),
                 vT.reshape(B, nl, a, LAYOUT_TILE), rel_bias)
    r, h_last, conv_new = _lru(xr.reshape(B, S, lw), xg.reshape(B, S, lw),
                               jnp.zeros((B, lw), F32), jnp.zeros((B, CONV_WIDTH - 1, lw), F32), w)
    y = _stage_d(x1, o.reshape(B * S, a), r.reshape(B * S, lw), w)
    return (y.reshape(B, S, D), k, v.reshape(B, S, H, V_DIM), h_last.reshape(B, lw), conv_new)


def _layer_sample(x, w, rel_bias, bounded, k_past, v_past, h0, conv0):
    B, Q, D = x.shape
    P = k_past.shape[1]
    a, lw = w["attn_width"], w["lru_width"]
    H = a // V_DIM
    near = LANES
    assert P % CHUNK == 0 and Q <= CHUNK and near >= MAX_DISTANCE and P > near
    x1, k, v, xr, xg, qb, kb, vb = _stage_a(x, w, transposed=False)
    q_pos = P + jnp.arange(Q, dtype=jnp.int32)
    idx_c = _bucket_tile(q_pos, jnp.arange(P - near, P, dtype=jnp.int32), False)
    idx_n = _bucket_tile(q_pos, q_pos, False)
    bias_c = _bias_tiles(rel_bias, idx_c[None])[:, 0]
    bias_n = _bias_tiles(rel_bias, idx_n[None])[:, 0]
    attend = lambda free: functools.partial(_sample_attention, w=w, shift_free=free)
    o = lax.cond(bounded, attend(True), attend(False),
                 qb.reshape(B, Q, a), kb.reshape(B, Q, a), vb.reshape(B, Q, a),
                 k_past.transpose(0, 2, 3, 4, 1).reshape(B, a, P),
                 v_past.reshape(B, P * H, V_DIM), bias_c, bias_n)
    r, h_last, conv_new = _lru(xr.reshape(B, Q, lw), xg.reshape(B, Q, lw), h0, conv0, w)
    y = _stage_d(x1, o.reshape(B * Q, a), r.reshape(B * Q, lw), w)
    return (y.reshape(B, Q, D), k.reshape(B, Q, H, 2, HEAD_DIM), v.reshape(B, Q, H, V_DIM),
            h_last.reshape(B, lw), conv_new)


def kernel(x_prompt, x_sample, cache_k, cache_v, state_lru, state_conv, rel_bias, norm_ffn1, ffn1_gate, ffn1_up, ffn1_down, norm_mix, w_in, q_norm, k_norm, lambda_q1, lambda_k1, lambda_q2, lambda_k2, subln, conv_w, conv_b, gate_a_w, gate_a_b, gate_x_w, gate_x_b, lru_L, lru_out_norm, w_out, norm_ffn2, ffn2_gate, ffn2_up, ffn2_down):
    p = dict(norm_ffn1=norm_ffn1, ffn1_gate=ffn1_gate, ffn1_up=ffn1_up, ffn1_down=ffn1_down,
             norm_mix=norm_mix, w_in=w_in, q_norm=q_norm, k_norm=k_norm,
             lambda_q1=lambda_q1, lambda_k1=lambda_k1, lambda_q2=lambda_q2, lambda_k2=lambda_k2,
             subln=subln, conv_w=conv_w, conv_b=conv_b, gate_a_w=gate_a_w, gate_a_b=gate_a_b,
             gate_x_w=gate_x_w, gate_x_b=gate_x_b, lru_L=lru_L, lru_out_norm=lru_out_norm,
             w_out=w_out, norm_ffn2=norm_ffn2, ffn2_gate=ffn2_gate, ffn2_up=ffn2_up,
             ffn2_down=ffn2_down)
    depth = cache_k.shape[0]
    assert depth == 1, "the lambda initial value is specialised to a single layer"
    w = _prepare_weights(0, p)
    bounded = _scores_bounded(w, rel_bias)
    yp, kp, vp, hp, cp = _layer_prompt(x_prompt, w, rel_bias, bounded)
    ys, kn, vn, hn, cn = _layer_sample(x_sample, w, rel_bias, bounded, cache_k[0], cache_v[0],
                                       state_lru[0], state_conv[0])
    stack = lambda t: t[None]
    return (yp, ys, stack(kp), stack(vp), stack(hp), stack(cp),
            stack(kn), stack(vn), stack(hn), stack(cn))
```

```python
import functools
import math

import numpy as np
import jax
import jax.numpy as jnp
from jax import lax
from jax.experimental import pallas as pl
from jax.experimental.pallas import tpu as pltpu

F32 = jnp.float32
BF16 = jnp.bfloat16

HEAD_DIM = 64
V_DIM = 2 * HEAD_DIM
CHUNK = 64
NUM_BUCKETS = 32
MAX_DISTANCE = 128
CONV_WIDTH = 4
LRU_BLOCKS = 8
LRU_C = 8.0
EPS = 1e-6
NEG_INF = -1e30
LAM_INIT = 0.8 - 0.6 * math.exp(-0.3 * 0)

LANES = 128
SUBLANES = 8
MXU_DIM = 256
VMEM_BYTES = 64 * 1024 * 1024

FF_CHUNK = MXU_DIM
ATTN_TILE = 512
LAYOUT_TILE = 256
FAR_ROWS = 1024
FAR_TILES = 8
LOG2E = 1.4426950408889634
SCORE_RANGE = 60.0
TOKEN_TILE = 512
STAGE_D_TILE = 1024
LRU_TILE = 1024
LRU_GROUP = 256


def _vmem_limit(nbytes):
    return int(min(VMEM_BYTES - 4 * 1024 * 1024, max(nbytes, 16 * 1024 * 1024)))


def _rms(x, g):
    ms = jnp.mean(x * x, axis=-1, keepdims=True)
    return x * lax.rsqrt(ms + EPS) * g


def _sigmoid(x):
    return 0.5 * jnp.tanh(0.5 * x) + 0.5


def _gelu_tanh(x):
    return 0.5 * x * (1.0 + jnp.tanh(math.sqrt(2.0 / math.pi) * (x + 0.044715 * (x * x * x))))


def _const_spec(shape):
    n = len(shape)
    return pl.BlockSpec(shape, lambda *_: (0,) * n, pipeline_mode=pl.Buffered(1))


def _swiglu_into(acc_ref, xn, wg_ref, wu_ref, wd_ref):
    d_ff = wg_ref.shape[1]
    for c in range(0, d_ff, FF_CHUNK):
        cols = slice(c, c + FF_CHUNK)
        g = jnp.dot(xn, wg_ref[:, cols], preferred_element_type=F32)
        u = jnp.dot(xn, wu_ref[:, cols], preferred_element_type=F32)
        h = (g * _sigmoid(g) * u).astype(BF16)
        part = jnp.dot(h, wd_ref[cols, :], preferred_element_type=F32)
        if c == 0:
            acc_ref[...] = part
        else:
            acc_ref[...] += part


def _stage_a_kernel(x_ref, g1_ref, wg_ref, wu_ref, wd_ref, gm_ref, win_ref, gmat_ref, qg_ref, kg_ref,
                    x1_ref, k_ref, v_ref, xr_ref, xg_ref, qb_ref, kb_ref, vb_ref, acc_ref,
                    *, attn_width, lru_width, transposed, tile, q_scale):
    x = x_ref[...]
    xn = _rms(x, g1_ref[...]).astype(BF16)
    _swiglu_into(acc_ref, xn, wg_ref, wu_ref, wd_ref)
    x1 = x + 0.5 * acc_ref[...]
    x1_ref[...] = x1
    hn = _rms(x1, gm_ref[...]).astype(BF16)
    proj = jnp.dot(hn, win_ref[...], preferred_element_type=F32)
    a = attn_width
    q = proj[:, :a]
    k = proj[:, a:2 * a]
    v = proj[:, 2 * a:3 * a]
    xr_ref[...] = proj[:, 3 * a:3 * a + lru_width]
    xg_ref[...] = _gelu_tanh(proj[:, 3 * a + lru_width:]).astype(BF16)

    def group_norm(t, g):
        ms = jnp.dot((t * t).astype(BF16), gmat_ref[...], preferred_element_type=F32)
        return t * lax.rsqrt(ms + EPS) * g

    qn = group_norm(q, qg_ref[...]) * q_scale
    kn = group_norm(k, kg_ref[...])
    n_heads = a // V_DIM
    for h in range(n_heads):
        v_ref[pl.ds(h, x.shape[0], stride=n_heads), :] = v[:, h * V_DIM:(h + 1) * V_DIM]
    kb_ref[...] = kn.astype(BF16)
    if transposed:
        k_ref[0] = kn.T
        for j in range(x.shape[0] // tile):
            rows = slice(j * tile, (j + 1) * tile)
            qb_ref[0, j] = qn[rows, :].T.astype(BF16)
            vb_ref[0, j] = v[rows, :].T.astype(BF16)
    else:
        k_ref[...] = kn
        qb_ref[...] = qn.astype(BF16)
        vb_ref[...] = v.astype(BF16)


def _stage_a(x, w, *, transposed):
    B, S, D = x.shape
    a = w["attn_width"]
    lw = w["lru_width"]
    tm = TOKEN_TILE
    T = LAYOUT_TILE
    q_scale = HEAD_DIM ** -0.5 * (LOG2E if transposed else 1.0)
    assert (B * S) % tm == 0 and (not transposed or (tm % T == 0 and S % tm == 0))
    nt = S // tm
    x2 = x.reshape(B * S, D)
    d_ff = w["ffn1_gate"].shape[1]
    tok = lambda width: pl.BlockSpec((tm, width), lambda i: (i, 0))
    out_shape = [
        jax.ShapeDtypeStruct((B * S, D), F32),
        jax.ShapeDtypeStruct((B * S, a), F32),
        jax.ShapeDtypeStruct((B * S, a), F32),
        jax.ShapeDtypeStruct((B * S, lw), F32),
        jax.ShapeDtypeStruct((B * S, lw), BF16),
    ]
    out_specs = [tok(D), tok(a), tok(a), tok(lw), tok(lw)]
    H = a // V_DIM
    out_shape[2] = jax.ShapeDtypeStruct((B * S * H, V_DIM), F32)
    out_specs[2] = pl.BlockSpec((tm * H, V_DIM), lambda i: (i, 0))
    if transposed:
        out_shape[1] = jax.ShapeDtypeStruct((B, a, S), F32)
        out_specs[1] = pl.BlockSpec((1, a, tm), lambda i: (i // nt, 0, i % nt))
        tshape = jax.ShapeDtypeStruct((B * nt, tm // T, a, T), BF16)
        tspec = pl.BlockSpec((1, tm // T, a, T), lambda i: (i, 0, 0, 0))
        out_shape += [tshape, jax.ShapeDtypeStruct((B * S, a), BF16), tshape]
        out_specs += [tspec, tok(a), tspec]
    else:
        bshape = jax.ShapeDtypeStruct((B * S, a), BF16)
        out_shape += [bshape, bshape, bshape]
        out_specs += [tok(a), tok(a), tok(a)]
    weight_bytes = 2 * (3 * D * d_ff + D * (3 * a + 2 * lw) + a * a)
    act_bytes = 4 * tm * (2 * 2 * D + 2 * 2 * (2 * a + 2 * lw) + D + 2 * (3 * a + 2 * lw) + 4 * FF_CHUNK)
    kern = functools.partial(_stage_a_kernel, attn_width=a, lru_width=lw, transposed=transposed,
                             tile=T, q_scale=q_scale)
    outs = pl.pallas_call(
        kern,
        grid=(B * S // tm,),
        in_specs=[
            tok(D),
            _const_spec((1, D)),
            _const_spec(w["ffn1_gate"].shape),
            _const_spec(w["ffn1_up"].shape),
            _const_spec(w["ffn1_down"].shape),
            _const_spec((1, D)),
            _const_spec(w["w_in"].shape),
            _const_spec((a, a)),
            _const_spec((1, a)),
            _const_spec((1, a)),
        ],
        out_specs=out_specs,
        out_shape=out_shape,
        scratch_shapes=[pltpu.VMEM((tm, D), F32)],
        compiler_params=pltpu.CompilerParams(
            dimension_semantics=("parallel",),
            vmem_limit_bytes=_vmem_limit(weight_bytes + act_bytes)),
        name="stage_a_t" if transposed else "stage_a_n",
    )(x2, w["norm_ffn1"], w["ffn1_gate"], w["ffn1_up"], w["ffn1_down"], w["norm_mix"], w["w_in"],
      w["group_mat"], w["q_gain"], w["k_gain"])
    return outs


def _stage_d_kernel(x1_ref, o_ref, r_ref, woa_ref, wor_ref, g2_ref, wg_ref, wu_ref, wd_ref,
                    y_ref, acc_ref):
    x2 = (x1_ref[...]
          + jnp.dot(o_ref[...], woa_ref[...], preferred_element_type=F32)
          + jnp.dot(r_ref[...], wor_ref[...], preferred_element_type=F32))
    xn = _rms(x2, g2_ref[...]).astype(BF16)
    _swiglu_into(acc_ref, xn, wg_ref, wu_ref, wd_ref)
    y_ref[...] = x2 + 0.5 * acc_ref[...]


def _stage_d(x1, o, r, w):
    N, D = x1.shape
    a = o.shape[1]
    lw = r.shape[1]
    tm = min(STAGE_D_TILE, N)
    assert N % tm == 0
    d_ff = w["ffn2_gate"].shape[1]
    tok = lambda width: pl.BlockSpec((tm, width), lambda i: (i, 0))
    weight_bytes = 2 * (3 * D * d_ff + D * D)
    act_bytes = 4 * tm * (2 * 2 * D + 2 * (a + lw) + 2 * D + 4 * FF_CHUNK)
    return pl.pallas_call(
        _stage_d_kernel,
        grid=(N // tm,),
        in_specs=[
            tok(D), tok(a), tok(lw),
            _const_spec((a, D)), _const_spec((lw, D)), _const_spec((1, D)),
            _const_spec(w["ffn2_gate"].shape), _const_spec(w["ffn2_up"].shape),
            _const_spec(w["ffn2_down"].shape),
        ],
        out_specs=tok(D),
        out_shape=jax.ShapeDtypeStruct((N, D), F32),
        scratch_shapes=[pltpu.VMEM((tm, D), F32)],
        compiler_params=pltpu.CompilerParams(
            dimension_semantics=("parallel",),
            vmem_limit_bytes=_vmem_limit(weight_bytes + act_bytes)),
        name="stage_d",
    )(x1, o, r, w["w_out_attn"], w["w_out_lru"], w["norm_ffn2"],
      w["ffn2_gate"], w["ffn2_up"], w["ffn2_down"])


def _t5_bucket(rel):
    n = NUM_BUCKETS // 2
    max_exact = n // 2
    ret = jnp.where(rel > 0, n, 0)
    rel = jnp.abs(rel)
    relf = jnp.maximum(rel, 1).astype(jnp.float32)
    large = max_exact + (jnp.log(relf / max_exact) / math.log(MAX_DISTANCE / max_exact)
                         * (n - max_exact)).astype(jnp.int32)
    large = jnp.minimum(large, n - 1)
    return ret + jnp.where(rel < max_exact, rel, large)


FAR_BUCKET = NUM_BUCKETS // 2 - 1


def _bucket_tile(q_pos, k_pos, keys_first):
    rel = k_pos[None, :] - q_pos[:, None]
    visible = (k_pos[None, :] // CHUNK) <= (q_pos[:, None] // CHUNK)
    idx = jnp.where(visible, _t5_bucket(rel), -1)
    return idx.T if keys_first else idx


def _bias_kernel(table_ref, idx_ref, out_ref, *, scale):
    h = pl.program_id(0)
    idx = idx_ref[0]
    acc = jnp.zeros(idx.shape, F32)
    for b in range(NUM_BUCKETS):
        acc = jnp.where(idx == b, table_ref[b, h], acc)
    out_ref[0, 0] = jnp.where(idx < 0, NEG_INF, (acc - table_ref[FAR_BUCKET, h]) * scale)


def _bias_tiles(rel_bias, idx, scale=1.0):
    n, R, C = idx.shape
    H = rel_bias.shape[1]
    return pl.pallas_call(
        functools.partial(_bias_kernel, scale=scale),
        grid=(H, n),
        in_specs=[pl.BlockSpec(memory_space=pltpu.SMEM),
                  pl.BlockSpec((1, R, C), lambda h, i: (i, 0, 0))],
        out_specs=pl.BlockSpec((1, 1, R, C), lambda h, i: (h, i, 0, 0)),
        out_shape=jax.ShapeDtypeStruct((H, n, R, C), F32),
        name="bias_tiles",
    )(rel_bias, idx)


def _lambda(lq1_ref, lk1_ref, lq2_ref, lk2_ref):
    s1 = jnp.sum(lq1_ref[...] * lk1_ref[...], axis=-1, keepdims=True)
    s2 = jnp.sum(lq2_ref[...] * lk2_ref[...], axis=-1, keepdims=True)
    return jnp.exp(s1) - jnp.exp(s2) + LAM_INIT


def _split_components(qt):
    first = lax.broadcasted_iota(jnp.int32, qt.shape, 0) < HEAD_DIM
    zero = jnp.zeros_like(qt)
    return jnp.concatenate([jnp.where(first, qt, zero), jnp.where(first, zero, qt)], axis=1)


def _attn_finish(o, sub_ref):
    o = o * lax.rsqrt(jnp.mean(o * o, axis=0, keepdims=True) + EPS)
    return o.T * sub_ref[...] * (1.0 - LAM_INIT)


def _prompt_attn_fixed_kernel(qT_ref, k_ref, vT_ref, bias_ref, lq1_ref, lk1_ref, lq2_ref, lk2_ref,
                              sub_ref, o_ref, l_ref, acc_ref, *, tile, far_rows):
    G = qT_ref.shape[3]
    T = tile
    sub = T // G
    nq = qT_ref.shape[1] // sub
    lam = _lambda(lq1_ref, lk1_ref, lq2_ref, lk2_ref)

    def lane_tiles(ref, first, count):
        return jnp.concatenate([ref[0, first + j] for j in range(count)], axis=1)

    def scores(first_group, count, q2):
        rows = pl.ds(pl.multiple_of(first_group * G, G), count * G)
        return jnp.dot(k_ref[0, rows, :], q2, preferred_element_type=F32)

    def weigh(first_group, count, s):
        p = jnp.exp2(s)
        return (jnp.sum(p, axis=0, keepdims=True),
                jnp.dot(lane_tiles(vT_ref, first_group, count), p.astype(BF16),
                        preferred_element_type=F32))

    def far_groups(first_group, count, q2):
        per = min(far_rows // G, count)
        firsts = [first_group + j for j in range(0, count, per)]
        parts = [weigh(g, per, scores(g, per, q2)) for g in firsts]
        l_ref[...] += functools.reduce(lambda x, y: x + y, [p[0] for p in parts])
        acc_ref[...] += functools.reduce(lambda x, y: x + y, [p[1] for p in parts])

    def q_tile(qi, carry):
        q2 = _split_components(lane_tiles(qT_ref, qi * sub, sub))

        start = jnp.maximum(qi - 1, 0)
        bias = jnp.concatenate([bias_ref[0, jnp.where(qi > 0, 1, 0)],
                                jnp.where(qi > 0, bias_ref[0, 0], NEG_INF)], axis=0)
        near = scores(start * sub, 2 * sub, q2) + jnp.concatenate([bias, bias], axis=1)
        l_near, acc_near = weigh(start * sub, 2 * sub, near)
        l_ref[...] = l_near
        acc_ref[...] = acc_near

        n_far = jnp.maximum(qi - 1, 0)

        def far(i, c):
            far_groups(i * FAR_TILES * sub, FAR_TILES * sub, q2)
            return c

        lax.fori_loop(0, n_far // FAR_TILES, far, 0)
        done = n_far - n_far % FAR_TILES
        chunk = FAR_TILES // 2
        while chunk >= 1:
            take = (n_far - done) >= chunk

            @pl.when(take)
            def _(done=done, chunk=chunk):
                far_groups(done * sub, chunk * sub, q2)

            done = done + jnp.where(take, chunk, 0)
            chunk //= 2

        on = acc_ref[...] * (1.0 / l_ref[...])
        o = on[:, :T] - lam * on[:, T:]
        rows = pl.ds(pl.multiple_of(qi * T, T), T)
        o_ref[0, rows, :] = _attn_finish(o, sub_ref).astype(o_ref.dtype)
        return carry

    lax.fori_loop(0, nq, q_tile, 0)


def _prompt_attn_online_kernel(qT_ref, k_ref, vT_ref, bias_ref, lq1_ref, lk1_ref, lq2_ref, lk2_ref,
                               sub_ref, o_ref, m_ref, l_ref, acc_ref, *, tile):
    T = tile
    sub = T // qT_ref.shape[3]
    nq = qT_ref.shape[1] // sub
    lam = _lambda(lq1_ref, lk1_ref, lq2_ref, lk2_ref)

    def lane_tiles(ref, i):
        return jnp.concatenate([ref[0, i * sub + j] for j in range(sub)], axis=1)

    def scores(ki, q2):
        kt = k_ref[0, pl.ds(pl.multiple_of(ki * T, T), T), :]
        return jnp.dot(kt, q2, preferred_element_type=F32)

    def pv(p, ki):
        return jnp.dot(lane_tiles(vT_ref, ki), p.astype(BF16), preferred_element_type=F32)

    def update(s, ki):
        m_prev = m_ref[...]
        m_new = jnp.maximum(m_prev, jnp.max(s, axis=0, keepdims=True))
        alpha = jnp.exp2(m_prev - m_new)
        p = jnp.exp2(s - m_new)
        l_ref[...] = alpha * l_ref[...] + jnp.sum(p, axis=0, keepdims=True)
        acc_ref[...] = alpha * acc_ref[...] + pv(p, ki)
        m_ref[...] = m_new

    def q_tile(qi, carry):
        q2 = _split_components(lane_tiles(qT_ref, qi))

        bias = bias_ref[0, 0]
        s = scores(qi, q2) + jnp.concatenate([bias, bias], axis=1)
        m = jnp.max(s, axis=0, keepdims=True)
        p = jnp.exp2(s - m)
        m_ref[...] = m
        l_ref[...] = jnp.sum(p, axis=0, keepdims=True)
        acc_ref[...] = pv(p, qi)
        kp = jnp.maximum(qi - 1, 0)
        bias = jnp.where(qi > 0, bias_ref[0, 1], NEG_INF)
        s = scores(kp, q2) + jnp.concatenate([bias, bias], axis=1)
        n_far = qi - 1

        update(s, kp)

        def far(ki, s_cur):
            s_next = scores(jnp.minimum(ki + 1, jnp.maximum(n_far - 1, 0)), q2)
            update(s_cur, ki)
            return s_next

        lax.fori_loop(0, n_far, far, scores(0, q2))

        on = acc_ref[...] * (1.0 / l_ref[...])
        o = on[:, :T] - lam * on[:, T:]
        rows = pl.ds(pl.multiple_of(qi * T, T), T)
        o_ref[0, rows, :] = _attn_finish(o, sub_ref).astype(o_ref.dtype)
        return carry

    lax.fori_loop(0, nq, q_tile, 0)


def _near_bias(rel_bias, tile, offsets):
    pos = jnp.arange(tile, dtype=jnp.int32)
    base = tile * max(offsets)
    idx = [_bucket_tile(pos + base, pos + base - o * tile, True) for o in offsets]
    return _bias_tiles(rel_bias, jnp.stack(idx), LOG2E)


def _prompt_attention(qT, kb, vT, rel_bias, w, *, fixed_shift):
    T = ATTN_TILE
    B, S = kb.shape[:2]
    G = qT.shape[-1]
    A = kb.shape[-1]
    H = A // V_DIM
    assert S % T == 0 and G % CHUNK == 0 and G >= MAX_DISTANCE and T == 2 * G
    lam_spec = pl.BlockSpec((1, HEAD_DIM), lambda b, h: (0, 0))
    tile_spec = pl.BlockSpec((1, S // G, V_DIM, G), lambda b, h: (b, 0, h, 0))
    seq_spec = pl.BlockSpec((1, S, V_DIM), lambda b, h: (b, 0, h))
    bias = _near_bias(rel_bias, T, (0, 1))
    tmp_bytes = 4 * 8 * T * 2 * T
    if fixed_shift:
        kern = functools.partial(_prompt_attn_fixed_kernel, tile=T, far_rows=FAR_ROWS)
        stats = []
    else:
        kern = functools.partial(_prompt_attn_online_kernel, tile=T)
        stats = [pltpu.VMEM((1, 2 * T), F32)]
    blk_bytes = 2 * 2 * (3 * S * V_DIM) + 2 * 2 * S * V_DIM + 2 * 4 * bias[0].size
    return pl.pallas_call(
        kern,
        grid=(B, H),
        in_specs=[tile_spec, seq_spec, tile_spec,
                  pl.BlockSpec((1,) + bias.shape[1:], lambda b, h: (h, 0, 0, 0)),
                  lam_spec, lam_spec, lam_spec, lam_spec,
                  pl.BlockSpec((1, V_DIM), lambda b, h: (0, 0))],
        out_specs=seq_spec,
        out_shape=jax.ShapeDtypeStruct((B, S, A), BF16),
        scratch_shapes=stats + [pltpu.VMEM((1, 2 * T), F32), pltpu.VMEM((V_DIM, 2 * T), F32)],
        compiler_params=pltpu.CompilerParams(
            dimension_semantics=("parallel", "parallel"),
            vmem_limit_bytes=_vmem_limit(blk_bytes + tmp_bytes + 4 * V_DIM * 2 * T)),
        name="prompt_attn_fixed" if fixed_shift else "prompt_attn_online",
    )(qT, kb, vT, bias, w["lambda_q1"], w["lambda_k1"], w["lambda_q2"], w["lambda_k2"], w["subln"])


def _sample_attn_kernel(q_ref, kn_ref, vn_ref, ck_ref, cv_ref, bc_ref, bn_ref,
                        lq1_ref, lk1_ref, lq2_ref, lk2_ref, sub_ref, o_ref, *, near, shift_free):
    Q = q_ref.shape[1]
    P = ck_ref.shape[2]
    H = q_ref.shape[2] // V_DIM
    lam = _lambda(lq1_ref, lk1_ref, lq2_ref, lk2_ref)
    first = lax.broadcasted_iota(jnp.int32, (Q, V_DIM), 1) < HEAD_DIM
    nt = (((1,), (1,)), ((), ()))
    far = P - near
    for h in range(H):
        cols = slice(h * V_DIM, (h + 1) * V_DIM)
        qh = q_ref[0, :, cols]
        zero = jnp.zeros_like(qh)
        q2 = jnp.concatenate([jnp.where(first, qh, zero), jnp.where(first, zero, qh)], axis=0)
        k_far = ck_ref[0, cols, :far].astype(BF16)
        k_near = ck_ref[0, cols, far:].astype(BF16)
        bc = bc_ref[h]
        bn = bn_ref[h]
        s_far = jnp.dot(q2, k_far, preferred_element_type=F32)
        s_near = (jnp.dot(q2, k_near, preferred_element_type=F32)
                  + jnp.concatenate([bc, bc], axis=0))
        s_new = (lax.dot_general(q2, kn_ref[0, :, cols], nt, preferred_element_type=F32)
                 + jnp.concatenate([bn, bn], axis=0))
        if not shift_free:
            m = jnp.maximum(jnp.maximum(jnp.max(s_far, axis=-1, keepdims=True),
                                        jnp.max(s_near, axis=-1, keepdims=True)),
                            jnp.max(s_new, axis=-1, keepdims=True))
            s_far, s_near, s_new = s_far - m, s_near - m, s_new - m
        p_far = jnp.exp(s_far)
        p_near = jnp.exp(s_near)
        p_new = jnp.exp(s_new)
        l = (jnp.sum(p_far, axis=-1, keepdims=True) + jnp.sum(p_near, axis=-1, keepdims=True)
             + jnp.sum(p_new, axis=-1, keepdims=True))
        v_far = cv_ref[0, pl.ds(h, far, stride=H), :].astype(BF16)
        v_near = cv_ref[0, pl.ds(far * H + h, near, stride=H), :].astype(BF16)
        acc = (jnp.dot(p_far.astype(BF16), v_far, preferred_element_type=F32)
               + jnp.dot(p_near.astype(BF16), v_near, preferred_element_type=F32)
               + jnp.dot(p_new.astype(BF16), vn_ref[0, :, cols], preferred_element_type=F32))
        on = acc * (1.0 / l)
        o = on[:Q] - lam * on[Q:]
        o = o * lax.rsqrt(jnp.mean(o * o, axis=-1, keepdims=True) + EPS)
        o_ref[0, :, cols] = (o * sub_ref[...] * (1.0 - LAM_INIT)).astype(o_ref.dtype)


def _sample_attention(qb, kb, vb, cache_k, cache_v, bias_c, bias_n, w, *, shift_free):
    B, Q, A = qb.shape
    P = cache_k.shape[2]
    near = bias_c.shape[-1]
    new_spec = pl.BlockSpec((1, Q, A), lambda b: (b, 0, 0))
    ck_spec = pl.BlockSpec((1, A, P), lambda b: (b, 0, 0))
    cv_spec = pl.BlockSpec((1,) + cache_v.shape[1:], lambda b: (b, 0, 0))
    lam_spec = pl.BlockSpec((1, HEAD_DIM), lambda b: (0, 0))
    blk_bytes = 2 * 2 * 4 * P * A + 2 * 4 * 2 * Q * A
    tmp_bytes = 4 * 2 * Q * (3 * P + 2 * P) + 2 * 2 * P * V_DIM
    return pl.pallas_call(
        functools.partial(_sample_attn_kernel, near=near, shift_free=shift_free),
        grid=(B,),
        in_specs=[new_spec, new_spec, new_spec, ck_spec, cv_spec,
                  pl.BlockSpec(bias_c.shape, lambda b: (0, 0, 0)),
                  pl.BlockSpec(bias_n.shape, lambda b: (0, 0, 0)),
                  lam_spec, lam_spec, lam_spec, lam_spec,
                  pl.BlockSpec((1, V_DIM), lambda b: (0, 0))],
        out_specs=new_spec,
        out_shape=jax.ShapeDtypeStruct((B, Q, A), BF16),
        compiler_params=pltpu.CompilerParams(
            dimension_semantics=("parallel",),
            vmem_limit_bytes=_vmem_limit(blk_bytes + tmp_bytes)),
        name="sample_attn_free" if shift_free else "sample_attn_rowmax",
    )(qb, kb, vb, cache_k, cache_v, bias_c, bias_n,
      w["lambda_q1"], w["lambda_k1"], w["lambda_q2"], w["lambda_k2"], w["subln"])


def _lru_kernel(xr_ref, gate_ref, h0_ref, c0_ref, cw_ref, cb_ref, wgate_ref, bgate_ref, L_ref, gn_ref,
                perm_ref, r_ref, hl_ref, cn_ref, tail_ref, hc_ref):
    t = pl.program_id(1)
    bb, ts, W = xr_ref.shape
    gs = perm_ref.shape[1]
    n = gs // SUBLANES
    tail = CONV_WIDTH - 1
    row = lax.broadcasted_iota(jnp.int32, (SUBLANES, W), 0)

    @pl.when(t == 0)
    def _():
        tail_ref[:, SUBLANES - tail:, :] = c0_ref[...]
        hc_ref[...] = jnp.broadcast_to(h0_ref[...], hc_ref.shape)

    def regroup(x):
        hi = x.astype(BF16)
        lo = (x - hi.astype(F32)).astype(BF16)
        return (jnp.dot(perm_ref[0], hi, preferred_element_type=F32)
                + jnp.dot(perm_ref[0], lo, preferred_element_type=F32))

    def blocks(v):
        return [v[j * SUBLANES:(j + 1) * SUBLANES] for j in range(n)]

    def shift_subsequences(block, entering):
        return jnp.where(row == 0, entering, pltpu.roll(block, 1, 0))

    z = -L_ref[...]
    softplus = jnp.maximum(z, 0.0) + jnp.log1p(jnp.exp(-jnp.abs(z)))

    def coefficients(sid, rows, before):
        x = xr_ref[sid, rows, :]
        xp = blocks(regroup(x))
        lead = [shift_subsequences(xp[n - k], before[tail - k:tail - k + 1, :])
                for k in range(tail, 0, -1)]
        ext = lead + xp
        xc = cb_ref[...]
        for j in range(CONV_WIDTH):
            xc = xc + jnp.concatenate(ext[j:j + n], axis=0) * cw_ref[j:j + 1, :]

        gates = (jnp.dot(xc.astype(BF16), wgate_ref[...], preferred_element_type=F32)
                 + bgate_ref[...])
        r = _sigmoid(gates[:, :W])
        i = _sigmoid(gates[:, W:])
        log_a = -LRU_C * r * softplus
        a = blocks(jnp.exp(log_a))
        th = jnp.tanh(log_a)
        u = -2.0 * th / (1.0 - th)
        b = blocks(jnp.where(u > 0.0, u * lax.rsqrt(u), 0.0) * (i * xc))
        return x[gs - tail:, :], a, b

    def recur(sid, rows, a, b, h_prev):
        end_a, end_b = a[0], b[0]
        for j in range(1, n):
            end_b = a[j] * end_b + b[j]
            end_a = a[j] * end_a
        d = 1
        while d < SUBLANES:
            keep = row >= d
            a_sh = jnp.where(keep, pltpu.roll(end_a, d, 0), 1.0)
            b_sh = jnp.where(keep, pltpu.roll(end_b, d, 0), 0.0)
            end_b = end_a * b_sh + end_b
            end_a = end_a * a_sh
            d *= 2
        h_end = end_a * h_prev + end_b
        state = shift_subsequences(h_end, h_prev)
        states = []
        for j in range(n):
            state = a[j] * state + b[j]
            states.append(state)
        h = jnp.concatenate(states, axis=0)

        gate = jnp.dot(perm_ref[0], gate_ref[sid, rows, :], preferred_element_type=F32)
        normed = _rms(h * gate, gn_ref[...]).astype(BF16)
        r_ref[sid, rows, :] = jnp.dot(perm_ref[1], normed,
                                      preferred_element_type=F32).astype(r_ref.dtype)
        return h_end[SUBLANES - 1:, :]

    groups = []
    befores = []
    for sid in range(bb):
        before = tail_ref[sid, SUBLANES - tail:, :]
        for r0 in range(0, ts, gs):
            rows = slice(r0, r0 + gs)
            before, a, b = coefficients(sid, rows, before)
            groups.append((sid, rows, a, b))
        befores.append(before)
    states = [hc_ref[sid, 0:1, :] for sid in range(bb)]
    for sid, rows, a, b in groups:
        states[sid] = recur(sid, rows, a, b, states[sid])
    for sid in range(bb):
        tail_ref[sid, SUBLANES - tail:, :] = befores[sid]
        hc_ref[sid] = jnp.broadcast_to(states[sid], hc_ref.shape[1:])

    @pl.when(t == pl.num_programs(1) - 1)
    def _():
        for sid in range(bb):
            hl_ref[sid] = states[sid]
            cn_ref[sid] = befores[sid]


def _lru(xr, xg, h0, conv0, w):
    B, S, W = xr.shape
    ts = min(LRU_TILE, S)
    bb = math.gcd(B, max(1, LRU_TILE // ts))
    assert S % ts == 0 and ts % SUBLANES == 0 and S >= CONV_WIDTH - 1
    tail = CONV_WIDTH - 1
    seq = pl.BlockSpec((bb, ts, W), lambda b, t: (b, t, 0))
    per_b = lambda r: pl.BlockSpec((bb, r, W), lambda b, t: (b, 0, 0))
    const = lambda shape: pl.BlockSpec(shape, lambda b, t: (0, 0))
    gs = min(LRU_GROUP, ts)
    assert ts % gs == 0 and gs % SUBLANES == 0 and gs >= SUBLANES * (CONV_WIDTH - 1)
    n = gs // SUBLANES
    time = np.arange(gs)
    perm = np.zeros((2, gs, gs), np.float32)
    perm[0, SUBLANES * (time % n) + time // n, time] = 1.0
    perm[1] = perm[0].T
    return pl.pallas_call(
        _lru_kernel,
        grid=(B // bb, S // ts),
        in_specs=[seq, seq, per_b(1), per_b(tail),
                  const((CONV_WIDTH, W)), const((1, W)), const((W, 2 * W)), const((1, 2 * W)),
                  const((1, W)), const((1, W)),
                  pl.BlockSpec((2, gs, gs), lambda b, t: (0, 0, 0))],
        out_specs=[seq, per_b(1), per_b(tail)],
        out_shape=[jax.ShapeDtypeStruct((B, S, W), BF16),
                   jax.ShapeDtypeStruct((B, 1, W), F32),
                   jax.ShapeDtypeStruct((B, tail, W), F32)],
        scratch_shapes=[pltpu.VMEM((bb, SUBLANES, W), F32), pltpu.VMEM((bb, SUBLANES, W), F32)],
        compiler_params=pltpu.CompilerParams(dimension_semantics=("parallel", "arbitrary")),
        name="rg_lru",
    )(xr, xg, h0.reshape(B, 1, W), conv0, w["conv_w"], w["conv_b"], w["w_gate"], w["b_gate"],
      w["lru_L"], w["lru_out_norm"], jnp.asarray(perm, BF16))


def _block_diag(wb):
    n, c, d = wb.shape
    eye = jnp.eye(n, dtype=wb.dtype)
    return (eye[:, None, :, None] * wb[:, :, None, :]).reshape(n * c, n * d)


def _prepare_weights(l, p):
    D, d_ff = p["ffn1_gate"].shape[1:]
    assert d_ff % FF_CHUNK == 0
    lw = p["conv_w"].shape[-1]
    in_width = p["w_in"].shape[-1]
    a = (in_width - 2 * lw) // 3
    assert a % V_DIM == 0

    group = np.arange(a) // HEAD_DIM
    w = dict(attn_width=a, lru_width=lw)
    for name in ("ffn1_gate", "ffn1_up", "ffn1_down", "ffn2_gate", "ffn2_up", "ffn2_down"):
        w[name] = p[name][l].astype(BF16)
    for name in ("norm_ffn1", "norm_mix", "norm_ffn2", "lru_out_norm", "conv_b", "lru_L", "subln",
                 "lambda_q1", "lambda_k1", "lambda_q2", "lambda_k2"):
        w[name] = p[name][l][None, :]
    w["w_in"] = p["w_in"][l].astype(BF16)
    w["group_mat"] = jnp.asarray((group[:, None] == group[None, :]) / HEAD_DIM, BF16)
    w["q_gain"] = jnp.tile(p["q_norm"][l], a // HEAD_DIM)[None, :]
    w["k_gain"] = jnp.tile(p["k_norm"][l], a // HEAD_DIM)[None, :]
    w["w_out_attn"] = p["w_out"][l][:a].astype(BF16)
    w["w_out_lru"] = p["w_out"][l][a:].astype(BF16)
    w["conv_w"] = p["conv_w"][l]
    w["w_gate"] = jnp.concatenate([_block_diag(p["gate_a_w"][l]), _block_diag(p["gate_x_w"][l])],
                                  axis=1).astype(BF16)
    w["b_gate"] = jnp.concatenate([p["gate_a_b"][l], p["gate_x_b"][l]])[None, :]
    return w


def _scores_bounded(w, rel_bias):
    score_bound = (HEAD_DIM ** 0.5 * LOG2E * (1 + 2.0 ** -7)
                   * jnp.max(jnp.abs(w["q_gain"])) * jnp.max(jnp.abs(w["k_gain"])))
    bias_bound = 2 * LOG2E * jnp.max(jnp.abs(rel_bias))
    return score_bound + bias_bound <= SCORE_RANGE


def _layer_prompt(x, w, rel_bias, bounded):
    B, S, D = x.shape
    T = ATTN_TILE
    a, lw = w["attn_width"], w["lru_width"]
    H = a // V_DIM
    x1, kT, v, xr, xg, qT, kb, vT = _stage_a(x, w, transposed=True)
    k = kT.reshape(B, H, 2, HEAD_DIM, S).transpose(0, 4, 1, 2, 3)
    nl = S // LAYOUT_TILE
    attend = lambda fixed: functools.partial(_prompt_attention, w=w, fixed_shift=fixed)
    o = lax.cond(bounded, attend(True), attend(False),
                 qT.reshape(B, nl, a, LAYOUT_TILE), kb.reshape(B, S, a),
                 vT.reshape(B, nl, a, LAYOUT_TILE), rel_bias)
    r, h_last, conv_new = _lru(xr.reshape(B, S, lw), xg.reshape(B, S, lw),
                               jnp.zeros((B, lw), F32), jnp.zeros((B, CONV_WIDTH - 1, lw), F32), w)
    y = _stage_d(x1, o.reshape(B * S, a), r.reshape(B * S, lw), w)
    return (y.reshape(B, S, D), k, v.reshape(B, S, H, V_DIM), h_last.reshape(B, lw), conv_new)


def _layer_sample(x, w, rel_bias, bounded, k_past, v_past, h0, conv0):
    B, Q, D = x.shape
    P = k_past.shape[1]
    a, lw = w["attn_width"], w["lru_width"]
    H = a // V_DIM
    near = LANES
    assert P % CHUNK == 0 and Q <= CHUNK and near >= MAX_DISTANCE and P > near
    x1, k, v, xr, xg, qb, kb, vb = _stage_a(x, w, transposed=False)
    q_pos = P + jnp.arange(Q, dtype=jnp.int32)
    idx_c = _bucket_tile(q_pos, jnp.arange(P - near, P, dtype=jnp.int32), False)
    idx_n = _bucket_tile(q_pos, q_pos, False)
    bias_c = _bias_tiles(rel_bias, idx_c[None])[:, 0]
    bias_n = _bias_tiles(rel_bias, idx_n[None])[:, 0]
    attend = lambda free: functools.partial(_sample_attention, w=w, shift_free=free)
    o = lax.cond(bounded, attend(True), attend(False),
                 qb.reshape(B, Q, a), kb.reshape(B, Q, a), vb.reshape(B, Q, a),
                 k_past.transpose(0, 2, 3, 4, 1).reshape(B, a, P),
                 v_past.reshape(B, P * H, V_DIM), bias_c, bias_n)
    r, h_last, conv_new = _lru(xr.reshape(B, Q, lw), xg.reshape(B, Q, lw), h0, conv0, w)
    y = _stage_d(x1, o.reshape(B * Q, a), r.reshape(B * Q, lw), w)
    return (y.reshape(B, Q, D), k.reshape(B, Q, H, 2, HEAD_DIM), v.reshape(B, Q, H, V_DIM),
            h_last.reshape(B, lw), conv_new)


def kernel(x_prompt, x_sample, cache_k, cache_v, state_lru, state_conv, rel_bias, norm_ffn1, ffn1_gate, ffn1_up, ffn1_down, norm_mix, w_in, q_norm, k_norm, lambda_q1, lambda_k1, lambda_q2, lambda_k2, subln, conv_w, conv_b, gate_a_w, gate_a_b, gate_x_w, gate_x_b, lru_L, lru_out_norm, w_out, norm_ffn2, ffn2_gate, ffn2_up, ffn2_down):
    p = dict(norm_ffn1=norm_ffn1, ffn1_gate=ffn1_gate, ffn1_up=ffn1_up, ffn1_down=ffn1_down,
             norm_mix=norm_mix, w_in=w_in, q_norm=q_norm, k_norm=k_norm,
             lambda_q1=lambda_q1, lambda_k1=lambda_k1, lambda_q2=lambda_q2, lambda_k2=lambda_k2,
             subln=subln, conv_w=conv_w, conv_b=conv_b, gate_a_w=gate_a_w, gate_a_b=gate_a_b,
             gate_x_w=gate_x_w, gate_x_b=gate_x_b, lru_L=lru_L, lru_out_norm=lru_out_norm,
             w_out=w_out, norm_ffn2=norm_ffn2, ffn2_gate=ffn2_gate, ffn2_up=ffn2_up,
             ffn2_down=ffn2_down)
    depth = cache_k.shape[0]
    assert depth == 1, "the lambda initial value is specialised to a single layer"
    w = _prepare_weights(0, p)
    bounded = _scores_bounded(w, rel_bias)
    yp, kp, vp, hp, cp = _layer_prompt(x_prompt, w, rel_bias, bounded)
    ys, kn, vn, hn, cn = _layer_sample(x_sample, w, rel_bias, bounded, cache_k[0], cache_v[0],
                                       state_lru[0], state_conv[0])
    stack = lambda t: t[None]
    return (yp, ys, stack(kp), stack(vp), stack(hp), stack(cp),
            stack(kn), stack(vn), stack(hn), stack(cn))
```

```python
import functools
import math

import numpy as np
import jax
import jax.numpy as jnp
from jax import lax
from jax.experimental import pallas as pl
from jax.experimental.pallas import tpu as pltpu

F32 = jnp.float32
BF16 = jnp.bfloat16

HEAD_DIM = 64
V_DIM = 2 * HEAD_DIM
CHUNK = 64
NUM_BUCKETS = 32
MAX_DISTANCE = 128
CONV_WIDTH = 4
LRU_C = 8.0
EPS = 1e-6
NEG_INF = -1e30
LAM_INIT = 0.8 - 0.6 * math.exp(-0.3 * 0)

LANES = 128
SUBLANES = 8
MXU_DIM = 256
MIB = 1024 * 1024
VMEM_BYTES = 64 * MIB
VMEM_RESERVED_BYTES = 4 * MIB
VMEM_DEFAULT_LIMIT_BYTES = 16 * MIB

FF_CHUNK = MXU_DIM
ATTN_TILE = 512
LAYOUT_TILE = 256
FAR_ROWS = 1024
FAR_TILES = 8
LOG2E = 1.4426950408889634
SCORE_RANGE = 60.0
TOKEN_TILE = 512
STAGE_D_TILE = 1024
MIN_STEPS = 4
LRU_TILE = 1024
LRU_GROUP = 256


def _vmem_limit(nbytes):
    return int(min(VMEM_BYTES - VMEM_RESERVED_BYTES, max(nbytes, VMEM_DEFAULT_LIMIT_BYTES)))


def _rms(x, g):
    ms = jnp.mean(x * x, axis=-1, keepdims=True)
    return x * lax.rsqrt(ms + EPS) * g


def _sigmoid(x):
    return 0.5 * jnp.tanh(0.5 * x) + 0.5


def _gelu_tanh(x):
    return 0.5 * x * (1.0 + jnp.tanh(math.sqrt(2.0 / math.pi) * (x + 0.044715 * (x * x * x))))


def _const_spec(shape):
    n = len(shape)
    return pl.BlockSpec(shape, lambda *_: (0,) * n, pipeline_mode=pl.Buffered(1))


def _swiglu_into(acc_ref, xn, wg_ref, wu_ref, wd_ref):
    d_ff = wg_ref.shape[1]
    for c in range(0, d_ff, FF_CHUNK):
        cols = slice(c, c + FF_CHUNK)
        g = jnp.dot(xn, wg_ref[:, cols], preferred_element_type=F32)
        u = jnp.dot(xn, wu_ref[:, cols], preferred_element_type=F32)
        h = (g * _sigmoid(g) * u).astype(BF16)
        part = jnp.dot(h, wd_ref[cols, :], preferred_element_type=F32)
        if c == 0:
            acc_ref[...] = part
        else:
            acc_ref[...] += part


def _stage_a_kernel(x_ref, g1_ref, wg_ref, wu_ref, wd_ref, gm_ref, win_ref, gmat_ref, qg_ref, kg_ref,
                    x1_ref, k_ref, v_ref, xr_ref, xg_ref, qb_ref, kb_ref, vb_ref, acc_ref,
                    *, attn_width, lru_width, transposed, tile, q_scale):
    x = x_ref[...]
    xn = _rms(x, g1_ref[...]).astype(BF16)
    _swiglu_into(acc_ref, xn, wg_ref, wu_ref, wd_ref)
    x1 = x + 0.5 * acc_ref[...]
    x1_ref[...] = x1
    hn = _rms(x1, gm_ref[...]).astype(BF16)
    proj = jnp.dot(hn, win_ref[...], preferred_element_type=F32)
    a = attn_width
    q = proj[:, :a]
    k = proj[:, a:2 * a]
    v = proj[:, 2 * a:3 * a]
    xr_ref[...] = proj[:, 3 * a:3 * a + lru_width]
    xg_ref[...] = _gelu_tanh(proj[:, 3 * a + lru_width:]).astype(BF16)

    def group_norm(t, g):
        ms = jnp.dot((t * t).astype(BF16), gmat_ref[...], preferred_element_type=F32)
        return t * lax.rsqrt(ms + EPS) * g

    qn = group_norm(q, qg_ref[...]) * q_scale
    kn = group_norm(k, kg_ref[...])
    n_heads = a // V_DIM
    for h in range(n_heads):
        v_ref[pl.ds(h, x.shape[0], stride=n_heads), :] = v[:, h * V_DIM:(h + 1) * V_DIM]
    kb_ref[...] = kn.astype(BF16)
    if transposed:
        k_ref[0] = kn.T
        for j in range(x.shape[0] // tile):
            rows = slice(j * tile, (j + 1) * tile)
            qb_ref[0, j] = qn[rows, :].T.astype(BF16)
            vb_ref[0, j] = v[rows, :].T.astype(BF16)
    else:
        k_ref[...] = kn
        qb_ref[...] = qn.astype(BF16)
        vb_ref[...] = v.astype(BF16)


def _stage_a(x, w, *, transposed):
    B, S, D = x.shape
    a = w["attn_width"]
    lw = w["lru_width"]
    tm = TOKEN_TILE
    T = LAYOUT_TILE
    q_scale = HEAD_DIM ** -0.5 * (LOG2E if transposed else 1.0)
    assert (B * S) % tm == 0 and (not transposed or (tm % T == 0 and S % tm == 0))
    nt = S // tm
    x2 = x.reshape(B * S, D)
    d_ff = w["ffn1_gate"].shape[1]
    tok = lambda width: pl.BlockSpec((tm, width), lambda i: (i, 0))
    out_shape = [
        jax.ShapeDtypeStruct((B * S, D), F32),
        jax.ShapeDtypeStruct((B * S, a), F32),
        jax.ShapeDtypeStruct((B * S, a), F32),
        jax.ShapeDtypeStruct((B * S, lw), F32),
        jax.ShapeDtypeStruct((B * S, lw), BF16),
    ]
    out_specs = [tok(D), tok(a), tok(a), tok(lw), tok(lw)]
    H = a // V_DIM
    out_shape[2] = jax.ShapeDtypeStruct((B * S * H, V_DIM), F32)
    out_specs[2] = pl.BlockSpec((tm * H, V_DIM), lambda i: (i, 0))
    if transposed:
        out_shape[1] = jax.ShapeDtypeStruct((B, a, S), F32)
        out_specs[1] = pl.BlockSpec((1, a, tm), lambda i: (i // nt, 0, i % nt))
        tshape = jax.ShapeDtypeStruct((B * nt, tm // T, a, T), BF16)
        tspec = pl.BlockSpec((1, tm // T, a, T), lambda i: (i, 0, 0, 0))
        out_shape += [tshape, jax.ShapeDtypeStruct((B * S, a), BF16), tshape]
        out_specs += [tspec, tok(a), tspec]
    else:
        bshape = jax.ShapeDtypeStruct((B * S, a), BF16)
        out_shape += [bshape, bshape, bshape]
        out_specs += [tok(a), tok(a), tok(a)]
    weight_bytes = 2 * (3 * D * d_ff + D * (3 * a + 2 * lw) + a * a)
    act_bytes = 4 * tm * (2 * 2 * D + 2 * 2 * (2 * a + 2 * lw) + D + 2 * (3 * a + 2 * lw) + 4 * FF_CHUNK)
    kern = functools.partial(_stage_a_kernel, attn_width=a, lru_width=lw, transposed=transposed,
                             tile=T, q_scale=q_scale)
    outs = pl.pallas_call(
        kern,
        grid=(B * S // tm,),
        in_specs=[
            tok(D),
            _const_spec((1, D)),
            _const_spec(w["ffn1_gate"].shape),
            _const_spec(w["ffn1_up"].shape),
            _const_spec(w["ffn1_down"].shape),
            _const_spec((1, D)),
            _const_spec(w["w_in"].shape),
            _const_spec((a, a)),
            _const_spec((1, a)),
            _const_spec((1, a)),
        ],
        out_specs=out_specs,
        out_shape=out_shape,
        scratch_shapes=[pltpu.VMEM((tm, D), F32)],
        compiler_params=pltpu.CompilerParams(
            dimension_semantics=("parallel",),
            vmem_limit_bytes=_vmem_limit(weight_bytes + act_bytes)),
        name="stage_a_t" if transposed else "stage_a_n",
    )(x2, w["norm_ffn1"], w["ffn1_gate"], w["ffn1_up"], w["ffn1_down"], w["norm_mix"], w["w_in"],
      w["group_mat"], w["q_gain"], w["k_gain"])
    return outs


def _stage_d_kernel(x1_ref, o_ref, r_ref, woa_ref, wor_ref, g2_ref, wg_ref, wu_ref, wd_ref,
                    y_ref, acc_ref):
    x2 = (x1_ref[...]
          + jnp.dot(o_ref[...], woa_ref[...], preferred_element_type=F32)
          + jnp.dot(r_ref[...], wor_ref[...], preferred_element_type=F32))
    xn = _rms(x2, g2_ref[...]).astype(BF16)
    _swiglu_into(acc_ref, xn, wg_ref, wu_ref, wd_ref)
    y_ref[...] = x2 + 0.5 * acc_ref[...]


def _stage_d(x1, o, r, w):
    N, D = x1.shape
    a = o.shape[1]
    lw = r.shape[1]
    tm = STAGE_D_TILE if N >= MIN_STEPS * STAGE_D_TILE else min(TOKEN_TILE, N)
    assert N % tm == 0
    d_ff = w["ffn2_gate"].shape[1]
    tok = lambda width: pl.BlockSpec((tm, width), lambda i: (i, 0))
    weight_bytes = 2 * (3 * D * d_ff + D * D)
    act_bytes = 4 * tm * (2 * 2 * D + 2 * (a + lw) + 2 * D + 4 * FF_CHUNK)
    return pl.pallas_call(
        _stage_d_kernel,
        grid=(N // tm,),
        in_specs=[
            tok(D), tok(a), tok(lw),
            _const_spec((a, D)), _const_spec((lw, D)), _const_spec((1, D)),
            _const_spec(w["ffn2_gate"].shape), _const_spec(w["ffn2_up"].shape),
            _const_spec(w["ffn2_down"].shape),
        ],
        out_specs=tok(D),
        out_shape=jax.ShapeDtypeStruct((N, D), F32),
        scratch_shapes=[pltpu.VMEM((tm, D), F32)],
        compiler_params=pltpu.CompilerParams(
            dimension_semantics=("parallel",),
            vmem_limit_bytes=_vmem_limit(weight_bytes + act_bytes)),
        name="stage_d",
    )(x1, o, r, w["w_out_attn"], w["w_out_lru"], w["norm_ffn2"],
      w["ffn2_gate"], w["ffn2_up"], w["ffn2_down"])


def _t5_bucket(rel):
    n = NUM_BUCKETS // 2
    max_exact = n // 2
    ret = jnp.where(rel > 0, n, 0)
    rel = jnp.abs(rel)
    relf = jnp.maximum(rel, 1).astype(jnp.float32)
    large = max_exact + (jnp.log(relf / max_exact) / math.log(MAX_DISTANCE / max_exact)
                         * (n - max_exact)).astype(jnp.int32)
    large = jnp.minimum(large, n - 1)
    return ret + jnp.where(rel < max_exact, rel, large)


FAR_BUCKET = NUM_BUCKETS // 2 - 1


def _bucket_tile(q_pos, k_pos, keys_first):
    rel = k_pos[None, :] - q_pos[:, None]
    visible = (k_pos[None, :] // CHUNK) <= (q_pos[:, None] // CHUNK)
    idx = jnp.where(visible, _t5_bucket(rel), -1)
    return idx.T if keys_first else idx


def _bias_kernel(table_ref, idx_ref, out_ref, *, scale):
    h = pl.program_id(0)
    idx = idx_ref[0]
    acc = jnp.zeros(idx.shape, F32)
    for b in range(NUM_BUCKETS):
        acc = jnp.where(idx == b, table_ref[b, h], acc)
    out_ref[0, 0] = jnp.where(idx < 0, NEG_INF, (acc - table_ref[FAR_BUCKET, h]) * scale)


def _bias_tiles(rel_bias, idx, scale=1.0):
    n, R, C = idx.shape
    H = rel_bias.shape[1]
    return pl.pallas_call(
        functools.partial(_bias_kernel, scale=scale),
        grid=(H, n),
        in_specs=[pl.BlockSpec(memory_space=pltpu.SMEM),
                  pl.BlockSpec((1, R, C), lambda h, i: (i, 0, 0))],
        out_specs=pl.BlockSpec((1, 1, R, C), lambda h, i: (h, i, 0, 0)),
        out_shape=jax.ShapeDtypeStruct((H, n, R, C), F32),
        name="bias_tiles",
    )(rel_bias, idx)


def _lambda(lq1_ref, lk1_ref, lq2_ref, lk2_ref):
    s1 = jnp.sum(lq1_ref[...] * lk1_ref[...], axis=-1, keepdims=True)
    s2 = jnp.sum(lq2_ref[...] * lk2_ref[...], axis=-1, keepdims=True)
    return jnp.exp(s1) - jnp.exp(s2) + LAM_INIT


def _split_components(qt):
    first = lax.broadcasted_iota(jnp.int32, qt.shape, 0) < HEAD_DIM
    zero = jnp.zeros_like(qt)
    return jnp.concatenate([jnp.where(first, qt, zero), jnp.where(first, zero, qt)], axis=1)


def _attn_finish(o, sub_ref):
    o = o * lax.rsqrt(jnp.mean(o * o, axis=0, keepdims=True) + EPS)
    return o.T * sub_ref[...] * (1.0 - LAM_INIT)


def _prompt_attn_fixed_kernel(qT_ref, k_ref, vT_ref, bias_ref, lq1_ref, lk1_ref, lq2_ref, lk2_ref,
                              sub_ref, o_ref, l_ref, acc_ref, *, tile, far_rows):
    G = qT_ref.shape[3]
    T = tile
    sub = T // G
    nq = qT_ref.shape[1] // sub
    lam = _lambda(lq1_ref, lk1_ref, lq2_ref, lk2_ref)

    def lane_tiles(ref, first, count):
        return jnp.concatenate([ref[0, first + j] for j in range(count)], axis=1)

    def scores(first_group, count, q2):
        rows = pl.ds(pl.multiple_of(first_group * G, G), count * G)
        return jnp.dot(k_ref[0, rows, :], q2, preferred_element_type=F32)

    def weigh(first_group, count, s):
        p = jnp.exp2(s)
        return (jnp.sum(p, axis=0, keepdims=True),
                jnp.dot(lane_tiles(vT_ref, first_group, count), p.astype(BF16),
                        preferred_element_type=F32))

    def far_groups(first_group, count, q2):
        per = min(far_rows // G, count)
        firsts = [first_group + j for j in range(0, count, per)]
        parts = [weigh(g, per, scores(g, per, q2)) for g in firsts]
        l_ref[...] += functools.reduce(lambda x, y: x + y, [p[0] for p in parts])
        acc_ref[...] += functools.reduce(lambda x, y: x + y, [p[1] for p in parts])

    def q_tile(qi, carry):
        q2 = _split_components(lane_tiles(qT_ref, qi * sub, sub))

        start = jnp.maximum(qi - 1, 0)
        bias = jnp.concatenate([bias_ref[0, jnp.where(qi > 0, 1, 0)],
                                jnp.where(qi > 0, bias_ref[0, 0], NEG_INF)], axis=0)
        near = scores(start * sub, 2 * sub, q2) + jnp.concatenate([bias, bias], axis=1)
        l_near, acc_near = weigh(start * sub, 2 * sub, near)
        l_ref[...] = l_near
        acc_ref[...] = acc_near

        n_far = jnp.maximum(qi - 1, 0)

        def far(i, c):
            far_groups(i * FAR_TILES * sub, FAR_TILES * sub, q2)
            return c

        lax.fori_loop(0, n_far // FAR_TILES, far, 0)
        done = n_far - n_far % FAR_TILES
        chunk = FAR_TILES // 2
        while chunk >= 1:
            take = (n_far - done) >= chunk

            @pl.when(take)
            def _(done=done, chunk=chunk):
                far_groups(done * sub, chunk * sub, q2)

            done = done + jnp.where(take, chunk, 0)
            chunk //= 2

        on = acc_ref[...] * (1.0 / l_ref[...])
        o = on[:, :T] - lam * on[:, T:]
        rows = pl.ds(pl.multiple_of(qi * T, T), T)
        o_ref[0, rows, :] = _attn_finish(o, sub_ref).astype(o_ref.dtype)
        return carry

    lax.fori_loop(0, nq, q_tile, 0)


def _prompt_attn_online_kernel(qT_ref, k_ref, vT_ref, bias_ref, lq1_ref, lk1_ref, lq2_ref, lk2_ref,
                               sub_ref, o_ref, m_ref, l_ref, acc_ref, *, tile):
    T = tile
    sub = T // qT_ref.shape[3]
    nq = qT_ref.shape[1] // sub
    lam = _lambda(lq1_ref, lk1_ref, lq2_ref, lk2_ref)

    def lane_tiles(ref, i):
        return jnp.concatenate([ref[0, i * sub + j] for j in range(sub)], axis=1)

    def scores(ki, q2):
        kt = k_ref[0, pl.ds(pl.multiple_of(ki * T, T), T), :]
        return jnp.dot(kt, q2, preferred_element_type=F32)

    def pv(p, ki):
        return jnp.dot(lane_tiles(vT_ref, ki), p.astype(BF16), preferred_element_type=F32)

    def update(s, ki):
        m_prev = m_ref[...]
        m_new = jnp.maximum(m_prev, jnp.max(s, axis=0, keepdims=True))
        alpha = jnp.exp2(m_prev - m_new)
        p = jnp.exp2(s - m_new)
        l_ref[...] = alpha * l_ref[...] + jnp.sum(p, axis=0, keepdims=True)
        acc_ref[...] = alpha * acc_ref[...] + pv(p, ki)
        m_ref[...] = m_new

    def q_tile(qi, carry):
        q2 = _split_components(lane_tiles(qT_ref, qi))

        bias = bias_ref[0, 0]
        s = scores(qi, q2) + jnp.concatenate([bias, bias], axis=1)
        m = jnp.max(s, axis=0, keepdims=True)
        p = jnp.exp2(s - m)
        m_ref[...] = m
        l_ref[...] = jnp.sum(p, axis=0, keepdims=True)
        acc_ref[...] = pv(p, qi)
        kp = jnp.maximum(qi - 1, 0)
        bias = jnp.where(qi > 0, bias_ref[0, 1], NEG_INF)
        s = scores(kp, q2) + jnp.concatenate([bias, bias], axis=1)
        n_far = qi - 1

        update(s, kp)

        def far(ki, s_cur):
            s_next = scores(jnp.minimum(ki + 1, jnp.maximum(n_far - 1, 0)), q2)
            update(s_cur, ki)
            return s_next

        lax.fori_loop(0, n_far, far, scores(0, q2))

        on = acc_ref[...] * (1.0 / l_ref[...])
        o = on[:, :T] - lam * on[:, T:]
        rows = pl.ds(pl.multiple_of(qi * T, T), T)
        o_ref[0, rows, :] = _attn_finish(o, sub_ref).astype(o_ref.dtype)
        return carry

    lax.fori_loop(0, nq, q_tile, 0)


def _near_bias(rel_bias, tile, offsets):
    pos = jnp.arange(tile, dtype=jnp.int32)
    base = tile * max(offsets)
    idx = [_bucket_tile(pos + base, pos + base - o * tile, True) for o in offsets]
    return _bias_tiles(rel_bias, jnp.stack(idx), LOG2E)


def _prompt_attention(qT, kb, vT, rel_bias, w, *, fixed_shift):
    T = ATTN_TILE
    B, S = kb.shape[:2]
    G = qT.shape[-1]
    A = kb.shape[-1]
    H = A // V_DIM
    assert S % T == 0 and G % CHUNK == 0 and G >= MAX_DISTANCE and T == 2 * G
    lam_spec = pl.BlockSpec((1, HEAD_DIM), lambda b, h: (0, 0))
    tile_spec = pl.BlockSpec((1, S // G, V_DIM, G), lambda b, h: (b, 0, h, 0))
    seq_spec = pl.BlockSpec((1, S, V_DIM), lambda b, h: (b, 0, h))
    bias = _near_bias(rel_bias, T, (0, 1))
    tmp_bytes = 4 * 8 * T * 2 * T
    if fixed_shift:
        kern = functools.partial(_prompt_attn_fixed_kernel, tile=T, far_rows=FAR_ROWS)
        stats = []
    else:
        kern = functools.partial(_prompt_attn_online_kernel, tile=T)
        stats = [pltpu.VMEM((1, 2 * T), F32)]
    blk_bytes = 2 * 2 * (3 * S * V_DIM) + 2 * 2 * S * V_DIM + 2 * 4 * bias[0].size
    return pl.pallas_call(
        kern,
        grid=(B, H),
        in_specs=[tile_spec, seq_spec, tile_spec,
                  pl.BlockSpec((1,) + bias.shape[1:], lambda b, h: (h, 0, 0, 0)),
                  lam_spec, lam_spec, lam_spec, lam_spec,
                  pl.BlockSpec((1, V_DIM), lambda b, h: (0, 0))],
        out_specs=seq_spec,
        out_shape=jax.ShapeDtypeStruct((B, S, A), BF16),
        scratch_shapes=stats + [pltpu.VMEM((1, 2 * T), F32), pltpu.VMEM((V_DIM, 2 * T), F32)],
        compiler_params=pltpu.CompilerParams(
            dimension_semantics=("parallel", "parallel"),
            vmem_limit_bytes=_vmem_limit(blk_bytes + tmp_bytes + 4 * V_DIM * 2 * T)),
        name="prompt_attn_fixed" if fixed_shift else "prompt_attn_online",
    )(qT, kb, vT, bias, w["lambda_q1"], w["lambda_k1"], w["lambda_q2"], w["lambda_k2"], w["subln"])


def _sample_attn_kernel(q_ref, kn_ref, vn_ref, ck_ref, cv_ref, bc_ref, bn_ref,
                        lq1_ref, lk1_ref, lq2_ref, lk2_ref, sub_ref, o_ref, *, near, shift_free):
    Q = q_ref.shape[1]
    P = ck_ref.shape[2]
    H = q_ref.shape[2] // V_DIM
    lam = _lambda(lq1_ref, lk1_ref, lq2_ref, lk2_ref)
    first = lax.broadcasted_iota(jnp.int32, (Q, V_DIM), 1) < HEAD_DIM
    nt = (((1,), (1,)), ((), ()))
    far = P - near
    for h in range(H):
        cols = slice(h * V_DIM, (h + 1) * V_DIM)
        qh = q_ref[0, :, cols]
        zero = jnp.zeros_like(qh)
        q2 = jnp.concatenate([jnp.where(first, qh, zero), jnp.where(first, zero, qh)], axis=0)
        k_far = ck_ref[0, cols, :far].astype(BF16)
        k_near = ck_ref[0, cols, far:].astype(BF16)
        bc = bc_ref[h]
        bn = bn_ref[h]
        s_far = jnp.dot(q2, k_far, preferred_element_type=F32)
        s_near = (jnp.dot(q2, k_near, preferred_element_type=F32)
                  + jnp.concatenate([bc, bc], axis=0))
        s_new = (lax.dot_general(q2, kn_ref[0, :, cols], nt, preferred_element_type=F32)
                 + jnp.concatenate([bn, bn], axis=0))
        if not shift_free:
            m = jnp.maximum(jnp.maximum(jnp.max(s_far, axis=-1, keepdims=True),
                                        jnp.max(s_near, axis=-1, keepdims=True)),
                            jnp.max(s_new, axis=-1, keepdims=True))
            s_far, s_near, s_new = s_far - m, s_near - m, s_new - m
        p_far = jnp.exp(s_far)
        p_near = jnp.exp(s_near)
        p_new = jnp.exp(s_new)
        l = (jnp.sum(p_far, axis=-1, keepdims=True) + jnp.sum(p_near, axis=-1, keepdims=True)
             + jnp.sum(p_new, axis=-1, keepdims=True))
        v_far = cv_ref[0, pl.ds(h, far, stride=H), :].astype(BF16)
        v_near = cv_ref[0, pl.ds(far * H + h, near, stride=H), :].astype(BF16)
        acc = (jnp.dot(p_far.astype(BF16), v_far, preferred_element_type=F32)
               + jnp.dot(p_near.astype(BF16), v_near, preferred_element_type=F32)
               + jnp.dot(p_new.astype(BF16), vn_ref[0, :, cols], preferred_element_type=F32))
        on = acc * (1.0 / l)
        o = on[:Q] - lam * on[Q:]
        o = o * lax.rsqrt(jnp.mean(o * o, axis=-1, keepdims=True) + EPS)
        o_ref[0, :, cols] = (o * sub_ref[...] * (1.0 - LAM_INIT)).astype(o_ref.dtype)


def _sample_attention(qb, kb, vb, cache_k, cache_v, bias_c, bias_n, w, *, shift_free):
    B, Q, A = qb.shape
    P = cache_k.shape[2]
    near = bias_c.shape[-1]
    new_spec = pl.BlockSpec((1, Q, A), lambda b: (b, 0, 0))
    ck_spec = pl.BlockSpec((1, A, P), lambda b: (b, 0, 0))
    cv_spec = pl.BlockSpec((1,) + cache_v.shape[1:], lambda b: (b, 0, 0))
    lam_spec = pl.BlockSpec((1, HEAD_DIM), lambda b: (0, 0))
    blk_bytes = 2 * 2 * 4 * P * A + 2 * 4 * 2 * Q * A
    tmp_bytes = 4 * 2 * Q * (3 * P + 2 * P) + 2 * 2 * P * V_DIM
    return pl.pallas_call(
        functools.partial(_sample_attn_kernel, near=near, shift_free=shift_free),
        grid=(B,),
        in_specs=[new_spec, new_spec, new_spec, ck_spec, cv_spec,
                  pl.BlockSpec(bias_c.shape, lambda b: (0, 0, 0)),
                  pl.BlockSpec(bias_n.shape, lambda b: (0, 0, 0)),
                  lam_spec, lam_spec, lam_spec, lam_spec,
                  pl.BlockSpec((1, V_DIM), lambda b: (0, 0))],
        out_specs=new_spec,
        out_shape=jax.ShapeDtypeStruct((B, Q, A), BF16),
        compiler_params=pltpu.CompilerParams(
            dimension_semantics=("parallel",),
            vmem_limit_bytes=_vmem_limit(blk_bytes + tmp_bytes)),
        name="sample_attn_free" if shift_free else "sample_attn_rowmax",
    )(qb, kb, vb, cache_k, cache_v, bias_c, bias_n,
      w["lambda_q1"], w["lambda_k1"], w["lambda_q2"], w["lambda_k2"], w["subln"])


def _lru_kernel(xr_ref, gate_ref, h0_ref, c0_ref, cw_ref, cb_ref, wgate_ref, bgate_ref, L_ref, gn_ref,
                perm_ref, r_ref, hl_ref, cn_ref, tail_ref, hc_ref):
    t = pl.program_id(1)
    bb, ts, W = xr_ref.shape
    gs = perm_ref.shape[1]
    n = gs // SUBLANES
    tail = CONV_WIDTH - 1
    row = lax.broadcasted_iota(jnp.int32, (SUBLANES, W), 0)

    @pl.when(t == 0)
    def _():
        tail_ref[:, SUBLANES - tail:, :] = c0_ref[...]
        hc_ref[...] = jnp.broadcast_to(h0_ref[...], hc_ref.shape)

    def regroup(x):
        hi = x.astype(BF16)
        lo = (x - hi.astype(F32)).astype(BF16)
        return (jnp.dot(perm_ref[0], hi, preferred_element_type=F32)
                + jnp.dot(perm_ref[0], lo, preferred_element_type=F32))

    def blocks(v):
        return [v[j * SUBLANES:(j + 1) * SUBLANES] for j in range(n)]

    def shift_subsequences(block, entering):
        return jnp.where(row == 0, entering, pltpu.roll(block, 1, 0))

    z = -L_ref[...]
    softplus = jnp.maximum(z, 0.0) + jnp.log1p(jnp.exp(-jnp.abs(z)))

    def coefficients(sid, rows, before):
        x = xr_ref[sid, rows, :]
        xp = blocks(regroup(x))
        lead = [shift_subsequences(xp[n - k], before[tail - k:tail - k + 1, :])
                for k in range(tail, 0, -1)]
        ext = lead + xp
        xc = cb_ref[...]
        for j in range(CONV_WIDTH):
            xc = xc + jnp.concatenate(ext[j:j + n], axis=0) * cw_ref[j:j + 1, :]

        gates = (jnp.dot(xc.astype(BF16), wgate_ref[...], preferred_element_type=F32)
                 + bgate_ref[...])
        r = _sigmoid(gates[:, :W])
        i = _sigmoid(gates[:, W:])
        log_a = -LRU_C * r * softplus
        a = blocks(jnp.exp(log_a))
        th = jnp.tanh(log_a)
        u = -2.0 * th / (1.0 - th)
        b = blocks(jnp.where(u > 0.0, u * lax.rsqrt(u), 0.0) * (i * xc))
        return x[gs - tail:, :], a, b

    def recur(sid, rows, a, b, h_prev):
        end_a, end_b = a[0], b[0]
        for j in range(1, n):
            end_b = a[j] * end_b + b[j]
            end_a = a[j] * end_a
        d = 1
        while d < SUBLANES:
            keep = row >= d
            a_sh = jnp.where(keep, pltpu.roll(end_a, d, 0), 1.0)
            b_sh = jnp.where(keep, pltpu.roll(end_b, d, 0), 0.0)
            end_b = end_a * b_sh + end_b
            end_a = end_a * a_sh
            d *= 2
        h_end = end_a * h_prev + end_b
        state = shift_subsequences(h_end, h_prev)
        states = []
        for j in range(n):
            state = a[j] * state + b[j]
            states.append(state)
        h = jnp.concatenate(states, axis=0)

        gate = jnp.dot(perm_ref[0], gate_ref[sid, rows, :], preferred_element_type=F32)
        normed = _rms(h * gate, gn_ref[...]).astype(BF16)
        r_ref[sid, rows, :] = jnp.dot(perm_ref[1], normed,
                                      preferred_element_type=F32).astype(r_ref.dtype)
        return h_end[SUBLANES - 1:, :]

    groups = []
    befores = []
    for sid in range(bb):
        before = tail_ref[sid, SUBLANES - tail:, :]
        for r0 in range(0, ts, gs):
            rows = slice(r0, r0 + gs)
            before, a, b = coefficients(sid, rows, before)
            groups.append((sid, rows, a, b))
        befores.append(before)
    states = [hc_ref[sid, 0:1, :] for sid in range(bb)]
    for sid, rows, a, b in groups:
        states[sid] = recur(sid, rows, a, b, states[sid])
    for sid in range(bb):
        tail_ref[sid, SUBLANES - tail:, :] = befores[sid]
        hc_ref[sid] = jnp.broadcast_to(states[sid], hc_ref.shape[1:])

    @pl.when(t == pl.num_programs(1) - 1)
    def _():
        for sid in range(bb):
            hl_ref[sid] = states[sid]
            cn_ref[sid] = befores[sid]


def _lru(xr, xg, h0, conv0, w):
    B, S, W = xr.shape
    ts = min(LRU_TILE, S)
    bb = math.gcd(B, max(1, LRU_TILE // ts))
    assert S % ts == 0 and ts % SUBLANES == 0 and S >= CONV_WIDTH - 1
    tail = CONV_WIDTH - 1
    seq = pl.BlockSpec((bb, ts, W), lambda b, t: (b, t, 0))
    per_b = lambda r: pl.BlockSpec((bb, r, W), lambda b, t: (b, 0, 0))
    const = lambda shape: pl.BlockSpec(shape, lambda b, t: (0, 0))
    gs = min(LRU_GROUP, ts)
    assert ts % gs == 0 and gs % SUBLANES == 0 and gs >= SUBLANES * (CONV_WIDTH - 1)
    n = gs // SUBLANES
    time = np.arange(gs)
    perm = np.zeros((2, gs, gs), np.float32)
    perm[0, SUBLANES * (time % n) + time // n, time] = 1.0
    perm[1] = perm[0].T
    return pl.pallas_call(
        _lru_kernel,
        grid=(B // bb, S // ts),
        in_specs=[seq, seq, per_b(1), per_b(tail),
                  const((CONV_WIDTH, W)), const((1, W)), const((W, 2 * W)), const((1, 2 * W)),
                  const((1, W)), const((1, W)),
                  pl.BlockSpec((2, gs, gs), lambda b, t: (0, 0, 0))],
        out_specs=[seq, per_b(1), per_b(tail)],
        out_shape=[jax.ShapeDtypeStruct((B, S, W), BF16),
                   jax.ShapeDtypeStruct((B, 1, W), F32),
                   jax.ShapeDtypeStruct((B, tail, W), F32)],
        scratch_shapes=[pltpu.VMEM((bb, SUBLANES, W), F32), pltpu.VMEM((bb, SUBLANES, W), F32)],
        compiler_params=pltpu.CompilerParams(dimension_semantics=("parallel", "arbitrary")),
        name="rg_lru",
    )(xr, xg, h0.reshape(B, 1, W), conv0, w["conv_w"], w["conv_b"], w["w_gate"], w["b_gate"],
      w["lru_L"], w["lru_out_norm"], jnp.asarray(perm, BF16))


def _block_diag(wb):
    n, c, d = wb.shape
    eye = jnp.eye(n, dtype=wb.dtype)
    return (eye[:, None, :, None] * wb[:, :, None, :]).reshape(n * c, n * d)


def _prepare_weights(l, p):
    D, d_ff = p["ffn1_gate"].shape[1:]
    assert d_ff % FF_CHUNK == 0
    lw = p["conv_w"].shape[-1]
    in_width = p["w_in"].shape[-1]
    a = (in_width - 2 * lw) // 3
    assert a % V_DIM == 0

    group = np.arange(a) // HEAD_DIM
    w = dict(attn_width=a, lru_width=lw)
    for name in ("ffn1_gate", "ffn1_up", "ffn1_down", "ffn2_gate", "ffn2_up", "ffn2_down"):
        w[name] = p[name][l].astype(BF16)
    for name in ("norm_ffn1", "norm_mix", "norm_ffn2", "lru_out_norm", "conv_b", "lru_L", "subln",
                 "lambda_q1", "lambda_k1", "lambda_q2", "lambda_k2"):
        w[name] = p[name][l][None, :]
    w["w_in"] = p["w_in"][l].astype(BF16)
    w["group_mat"] = jnp.asarray((group[:, None] == group[None, :]) / HEAD_DIM, BF16)
    w["q_gain"] = jnp.tile(p["q_norm"][l], a // HEAD_DIM)[None, :]
    w["k_gain"] = jnp.tile(p["k_norm"][l], a // HEAD_DIM)[None, :]
    w["w_out_attn"] = p["w_out"][l][:a].astype(BF16)
    w["w_out_lru"] = p["w_out"][l][a:].astype(BF16)
    w["conv_w"] = p["conv_w"][l]
    w["w_gate"] = jnp.concatenate([_block_diag(p["gate_a_w"][l]), _block_diag(p["gate_x_w"][l])],
                                  axis=1).astype(BF16)
    w["b_gate"] = jnp.concatenate([p["gate_a_b"][l], p["gate_x_b"][l]])[None, :]
    return w


def _scores_bounded(w, rel_bias):
    score_bound = (HEAD_DIM ** 0.5 * LOG2E * (1 + 2.0 ** -7)
                   * jnp.max(jnp.abs(w["q_gain"])) * jnp.max(jnp.abs(w["k_gain"])))
    bias_bound = 2 * LOG2E * jnp.max(jnp.abs(rel_bias))
    return score_bound + bias_bound <= SCORE_RANGE


def _layer_prompt(x, w, rel_bias, bounded):
    B, S, D = x.shape
    T = ATTN_TILE
    a, lw = w["attn_width"], w["lru_width"]
    H = a // V_DIM
    x1, kT, v, xr, xg, qT, kb, vT = _stage_a(x, w, transposed=True)
    k = kT.reshape(B, H, 2, HEAD_DIM, S).transpose(0, 4, 1, 2, 3)
    nl = S // LAYOUT_TILE
    attend = lambda fixed: functools.partial(_prompt_attention, w=w, fixed_shift=fixed)
    o = lax.cond(bounded, attend(True), attend(False),
                 qT.reshape(B, nl, a, LAYOUT_TILE), kb.reshape(B, S, a),
                 vT.reshape(B, nl, a, LAYOUT_TILE), rel_bias)
    r, h_last, conv_new = _lru(xr.reshape(B, S, lw), xg.reshape(B, S, lw),
                               jnp.zeros((B, lw), F32), jnp.zeros((B, CONV_WIDTH - 1, lw), F32), w)
    y = _stage_d(x1, o.reshape(B * S, a), r.reshape(B * S, lw), w)
    return (y.reshape(B, S, D), k, v.reshape(B, S, H, V_DIM), h_last.reshape(B, lw), conv_new)


def _layer_sample(x, w, rel_bias, bounded, k_past, v_past, h0, conv0):
    B, Q, D = x.shape
    P = k_past.shape[1]
    a, lw = w["attn_width"], w["lru_width"]
    H = a // V_DIM
    near = LANES
    assert P % CHUNK == 0 and Q <= CHUNK and near >= MAX_DISTANCE and P > near
    x1, k, v, xr, xg, qb, kb, vb = _stage_a(x, w, transposed=False)
    q_pos = P + jnp.arange(Q, dtype=jnp.int32)
    idx_c = _bucket_tile(q_pos, jnp.arange(P - near, P, dtype=jnp.int32), False)
    idx_n = _bucket_tile(q_pos, q_pos, False)
    bias_c = _bias_tiles(rel_bias, idx_c[None])[:, 0]
    bias_n = _bias_tiles(rel_bias, idx_n[None])[:, 0]
    attend = lambda free: functools.partial(_sample_attention, w=w, shift_free=free)
    o = lax.cond(bounded, attend(True), attend(False),
                 qb.reshape(B, Q, a), kb.reshape(B, Q, a), vb.reshape(B, Q, a),
                 k_past.transpose(0, 2, 3, 4, 1).reshape(B, a, P),
                 v_past.reshape(B, P * H, V_DIM), bias_c, bias_n)
    r, h_last, conv_new = _lru(xr.reshape(B, Q, lw), xg.reshape(B, Q, lw), h0, conv0, w)
    y = _stage_d(x1, o.reshape(B * Q, a), r.reshape(B * Q, lw), w)
    return (y.reshape(B, Q, D), k.reshape(B, Q, H, 2, HEAD_DIM), v.reshape(B, Q, H, V_DIM),
            h_last.reshape(B, lw), conv_new)


def kernel(x_prompt, x_sample, cache_k, cache_v, state_lru, state_conv, rel_bias, norm_ffn1, ffn1_gate, ffn1_up, ffn1_down, norm_mix, w_in, q_norm, k_norm, lambda_q1, lambda_k1, lambda_q2, lambda_k2, subln, conv_w, conv_b, gate_a_w, gate_a_b, gate_x_w, gate_x_b, lru_L, lru_out_norm, w_out, norm_ffn2, ffn2_gate, ffn2_up, ffn2_down):
    p = dict(norm_ffn1=norm_ffn1, ffn1_gate=ffn1_gate, ffn1_up=ffn1_up, ffn1_down=ffn1_down,
             norm_mix=norm_mix, w_in=w_in, q_norm=q_norm, k_norm=k_norm,
             lambda_q1=lambda_q1, lambda_k1=lambda_k1, lambda_q2=lambda_q2, lambda_k2=lambda_k2,
             subln=subln, conv_w=conv_w, conv_b=conv_b, gate_a_w=gate_a_w, gate_a_b=gate_a_b,
             gate_x_w=gate_x_w, gate_x_b=gate_x_b, lru_L=lru_L, lru_out_norm=lru_out_norm,
             w_out=w_out, norm_ffn2=norm_ffn2, ffn2_gate=ffn2_gate, ffn2_up=ffn2_up,
             ffn2_down=ffn2_down)
    depth = cache_k.shape[0]
    assert depth == 1, "the lambda initial value is specialised to a single layer"
    w = _prepare_weights(0, p)
    bounded = _scores_bounded(w, rel_bias)
    yp, kp, vp, hp, cp = _layer_prompt(x_prompt, w, rel_bias, bounded)
    ys, kn, vn, hn, cn = _layer_sample(x_sample, w, rel_bias, bounded, cache_k[0], cache_v[0],
                                       state_lru[0], state_conv[0])
    stack = lambda t: t[None]
    return (yp, ys, stack(kp), stack(vp), stack(hp), stack(cp),
            stack(kn), stack(vn), stack(hn), stack(cn))
```

```python
import functools
import math

import numpy as np
import jax
import jax.numpy as jnp
from jax import lax
from jax.experimental import pallas as pl
from jax.experimental.pallas import tpu as pltpu

F32 = jnp.float32
BF16 = jnp.bfloat16

HEAD_DIM = 64
V_DIM = 2 * HEAD_DIM
CHUNK = 64
NUM_BUCKETS = 32
MAX_DISTANCE = 128
CONV_WIDTH = 4
LRU_C = 8.0
EPS = 1e-6
NEG_INF = -1e30
LAM_INIT = 0.8 - 0.6 * math.exp(-0.3 * 0)

LANES = 128
SUBLANES = 8
MXU_DIM = 256
MIB = 1024 * 1024
VMEM_BYTES = 64 * MIB
VMEM_RESERVED_BYTES = 4 * MIB
VMEM_DEFAULT_LIMIT_BYTES = 16 * MIB

FF_CHUNK = MXU_DIM
ATTN_TILE = 512
LAYOUT_TILE = 256
BIAS_BLOCK = LANES
FAR_ROWS = 1024
FAR_TILES = 8
LOG2E = 1.4426950408889634
SCORE_RANGE = 60.0
TOKEN_TILE = 512
STAGE_D_TILE = 1024
MIN_STEPS = 4
LRU_TILE = 1024
LRU_GROUP = 256


def _vmem_limit(nbytes):
    return int(min(VMEM_BYTES - VMEM_RESERVED_BYTES, max(nbytes, VMEM_DEFAULT_LIMIT_BYTES)))


def _rms(x, g):
    ms = jnp.mean(x * x, axis=-1, keepdims=True)
    return x * lax.rsqrt(ms + EPS) * g


def _sigmoid(x):
    return 0.5 * jnp.tanh(0.5 * x) + 0.5


def _gelu_tanh(x):
    return 0.5 * x * (1.0 + jnp.tanh(math.sqrt(2.0 / math.pi) * (x + 0.044715 * (x * x * x))))


def _const_spec(shape):
    n = len(shape)
    return pl.BlockSpec(shape, lambda *_: (0,) * n, pipeline_mode=pl.Buffered(1))


def _swiglu_into(acc_ref, xn, wg_ref, wu_ref, wd_ref):
    d_ff = wg_ref.shape[1]
    for c in range(0, d_ff, FF_CHUNK):
        cols = slice(c, c + FF_CHUNK)
        g = jnp.dot(xn, wg_ref[:, cols], preferred_element_type=F32)
        u = jnp.dot(xn, wu_ref[:, cols], preferred_element_type=F32)
        h = (g * _sigmoid(g) * u).astype(BF16)
        part = jnp.dot(h, wd_ref[cols, :], preferred_element_type=F32)
        if c == 0:
            acc_ref[...] = part
        else:
            acc_ref[...] += part


def _stage_a_kernel(x_ref, g1_ref, wg_ref, wu_ref, wd_ref, gm_ref, win_ref, gmat_ref, qg_ref, kg_ref,
                    x1_ref, k_ref, v_ref, xr_ref, xg_ref, qb_ref, kb_ref, vb_ref, acc_ref,
                    *, attn_width, lru_width, transposed, tile, q_scale):
    x = x_ref[...]
    xn = _rms(x, g1_ref[...]).astype(BF16)
    _swiglu_into(acc_ref, xn, wg_ref, wu_ref, wd_ref)
    x1 = x + 0.5 * acc_ref[...]
    x1_ref[...] = x1
    hn = _rms(x1, gm_ref[...]).astype(BF16)
    proj = jnp.dot(hn, win_ref[...], preferred_element_type=F32)
    a = attn_width
    q = proj[:, :a]
    k = proj[:, a:2 * a]
    v = proj[:, 2 * a:3 * a]
    xr_ref[...] = proj[:, 3 * a:3 * a + lru_width]
    xg_ref[...] = _gelu_tanh(proj[:, 3 * a + lru_width:]).astype(BF16)

    def group_norm(t, g):
        ms = jnp.dot((t * t).astype(BF16), gmat_ref[...], preferred_element_type=F32)
        return t * lax.rsqrt(ms + EPS) * g

    qn = group_norm(q, qg_ref[...]) * q_scale
    kn = group_norm(k, kg_ref[...])
    n_heads = a // V_DIM
    for h in range(n_heads):
        v_ref[pl.ds(h, x.shape[0], stride=n_heads), :] = v[:, h * V_DIM:(h + 1) * V_DIM]
    kb_ref[...] = kn.astype(BF16)
    if transposed:
        k_ref[0] = kn.T
        for j in range(x.shape[0] // tile):
            rows = slice(j * tile, (j + 1) * tile)
            qb_ref[0, j] = qn[rows, :].T.astype(BF16)
            vb_ref[0, j] = v[rows, :].T.astype(BF16)
    else:
        k_ref[...] = kn
        qb_ref[...] = qn.astype(BF16)
        vb_ref[...] = v.astype(BF16)


def _stage_a(x, w, *, transposed):
    B, S, D = x.shape
    a = w["attn_width"]
    lw = w["lru_width"]
    tm = TOKEN_TILE
    T = LAYOUT_TILE
    q_scale = HEAD_DIM ** -0.5 * (LOG2E if transposed else 1.0)
    assert (B * S) % tm == 0 and (not transposed or (tm % T == 0 and S % tm == 0))
    nt = S // tm
    x2 = x.reshape(B * S, D)
    d_ff = w["ffn1_gate"].shape[1]
    tok = lambda width: pl.BlockSpec((tm, width), lambda i: (i, 0))
    out_shape = [
        jax.ShapeDtypeStruct((B * S, D), F32),
        jax.ShapeDtypeStruct((B * S, a), F32),
        jax.ShapeDtypeStruct((B * S, a), F32),
        jax.ShapeDtypeStruct((B * S, lw), F32),
        jax.ShapeDtypeStruct((B * S, lw), BF16),
    ]
    out_specs = [tok(D), tok(a), tok(a), tok(lw), tok(lw)]
    H = a // V_DIM
    out_shape[2] = jax.ShapeDtypeStruct((B * S * H, V_DIM), F32)
    out_specs[2] = pl.BlockSpec((tm * H, V_DIM), lambda i: (i, 0))
    if transposed:
        out_shape[1] = jax.ShapeDtypeStruct((B, a, S), F32)
        out_specs[1] = pl.BlockSpec((1, a, tm), lambda i: (i // nt, 0, i % nt))
        tshape = jax.ShapeDtypeStruct((B * nt, tm // T, a, T), BF16)
        tspec = pl.BlockSpec((1, tm // T, a, T), lambda i: (i, 0, 0, 0))
        out_shape += [tshape, jax.ShapeDtypeStruct((B * S, a), BF16), tshape]
        out_specs += [tspec, tok(a), tspec]
    else:
        bshape = jax.ShapeDtypeStruct((B * S, a), BF16)
        out_shape += [bshape, bshape, bshape]
        out_specs += [tok(a), tok(a), tok(a)]
    weight_bytes = 2 * (3 * D * d_ff + D * (3 * a + 2 * lw) + a * a)
    act_bytes = 4 * tm * (2 * 2 * D + 2 * 2 * (2 * a + 2 * lw) + D + 2 * (3 * a + 2 * lw) + 4 * FF_CHUNK)
    kern = functools.partial(_stage_a_kernel, attn_width=a, lru_width=lw, transposed=transposed,
                             tile=T, q_scale=q_scale)
    outs = pl.pallas_call(
        kern,
        grid=(B * S // tm,),
        in_specs=[
            tok(D),
            _const_spec((1, D)),
            _const_spec(w["ffn1_gate"].shape),
            _const_spec(w["ffn1_up"].shape),
            _const_spec(w["ffn1_down"].shape),
            _const_spec((1, D)),
            _const_spec(w["w_in"].shape),
            _const_spec((a, a)),
            _const_spec((1, a)),
            _const_spec((1, a)),
        ],
        out_specs=out_specs,
        out_shape=out_shape,
        scratch_shapes=[pltpu.VMEM((tm, D), F32)],
        compiler_params=pltpu.CompilerParams(
            dimension_semantics=("parallel",),
            vmem_limit_bytes=_vmem_limit(weight_bytes + act_bytes)),
        name="stage_a_t" if transposed else "stage_a_n",
    )(x2, w["norm_ffn1"], w["ffn1_gate"], w["ffn1_up"], w["ffn1_down"], w["norm_mix"], w["w_in"],
      w["group_mat"], w["q_gain"], w["k_gain"])
    return outs


def _stage_d_kernel(x1_ref, o_ref, r_ref, woa_ref, wor_ref, g2_ref, wg_ref, wu_ref, wd_ref,
                    y_ref, acc_ref):
    x2 = (x1_ref[...]
          + jnp.dot(o_ref[...], woa_ref[...], preferred_element_type=F32)
          + jnp.dot(r_ref[...], wor_ref[...], preferred_element_type=F32))
    xn = _rms(x2, g2_ref[...]).astype(BF16)
    _swiglu_into(acc_ref, xn, wg_ref, wu_ref, wd_ref)
    y_ref[...] = x2 + 0.5 * acc_ref[...]


def _stage_d(x1, o, r, w):
    N, D = x1.shape
    a = o.shape[1]
    lw = r.shape[1]
    tm = STAGE_D_TILE if N >= MIN_STEPS * STAGE_D_TILE else min(TOKEN_TILE, N)
    assert N % tm == 0
    d_ff = w["ffn2_gate"].shape[1]
    tok = lambda width: pl.BlockSpec((tm, width), lambda i: (i, 0))
    weight_bytes = 2 * (3 * D * d_ff + D * D)
    act_bytes = 4 * tm * (2 * 2 * D + 2 * (a + lw) + 2 * D + 4 * FF_CHUNK)
    return pl.pallas_call(
        _stage_d_kernel,
        grid=(N // tm,),
        in_specs=[
            tok(D), tok(a), tok(lw),
            _const_spec((a, D)), _const_spec((lw, D)), _const_spec((1, D)),
            _const_spec(w["ffn2_gate"].shape), _const_spec(w["ffn2_up"].shape),
            _const_spec(w["ffn2_down"].shape),
        ],
        out_specs=tok(D),
        out_shape=jax.ShapeDtypeStruct((N, D), F32),
        scratch_shapes=[pltpu.VMEM((tm, D), F32)],
        compiler_params=pltpu.CompilerParams(
            dimension_semantics=("parallel",),
            vmem_limit_bytes=_vmem_limit(weight_bytes + act_bytes)),
        name="stage_d",
    )(x1, o, r, w["w_out_attn"], w["w_out_lru"], w["norm_ffn2"],
      w["ffn2_gate"], w["ffn2_up"], w["ffn2_down"])


def _t5_bucket(rel):
    n = NUM_BUCKETS // 2
    max_exact = n // 2
    ret = jnp.where(rel > 0, n, 0)
    rel = jnp.abs(rel)
    relf = jnp.maximum(rel, 1).astype(jnp.float32)
    large = max_exact + (jnp.log(relf / max_exact) / math.log(MAX_DISTANCE / max_exact)
                         * (n - max_exact)).astype(jnp.int32)
    large = jnp.minimum(large, n - 1)
    return ret + jnp.where(rel < max_exact, rel, large)


FAR_BUCKET = NUM_BUCKETS // 2 - 1


def _bucket_tile(q_pos, k_pos, keys_first):
    rel = k_pos[None, :] - q_pos[:, None]
    visible = (k_pos[None, :] // CHUNK) <= (q_pos[:, None] // CHUNK)
    idx = jnp.where(visible, _t5_bucket(rel), -1)
    return idx.T if keys_first else idx


def _bias_kernel(table_ref, idx_ref, out_ref, *, scale):
    h = pl.program_id(0)
    idx = idx_ref[0]
    acc = jnp.zeros(idx.shape, F32)
    for b in range(NUM_BUCKETS):
        acc = jnp.where(idx == b, table_ref[b, h], acc)
    out_ref[0, 0] = jnp.where(idx < 0, NEG_INF, (acc - table_ref[FAR_BUCKET, h]) * scale)


def _bias_tiles(rel_bias, idx, scale=1.0):
    n, R, C = idx.shape
    H = rel_bias.shape[1]
    return pl.pallas_call(
        functools.partial(_bias_kernel, scale=scale),
        grid=(H, n),
        in_specs=[pl.BlockSpec(memory_space=pltpu.SMEM),
                  pl.BlockSpec((1, R, C), lambda h, i: (i, 0, 0))],
        out_specs=pl.BlockSpec((1, 1, R, C), lambda h, i: (h, i, 0, 0)),
        out_shape=jax.ShapeDtypeStruct((H, n, R, C), F32),
        name="bias_tiles",
    )(rel_bias, idx)


def _lambda(lq1_ref, lk1_ref, lq2_ref, lk2_ref):
    s1 = jnp.sum(lq1_ref[...] * lk1_ref[...], axis=-1, keepdims=True)
    s2 = jnp.sum(lq2_ref[...] * lk2_ref[...], axis=-1, keepdims=True)
    return jnp.exp(s1) - jnp.exp(s2) + LAM_INIT


def _split_components(qt):
    first = lax.broadcasted_iota(jnp.int32, qt.shape, 0) < HEAD_DIM
    zero = jnp.zeros_like(qt)
    return jnp.concatenate([jnp.where(first, qt, zero), jnp.where(first, zero, qt)], axis=1)


def _attn_finish(o, sub_ref):
    o = o * lax.rsqrt(jnp.mean(o * o, axis=0, keepdims=True) + EPS)
    return o.T * sub_ref[...] * (1.0 - LAM_INIT)


def _prompt_attn_fixed_kernel(qT_ref, k_ref, vT_ref, bias_ref, lq1_ref, lk1_ref, lq2_ref, lk2_ref,
                              sub_ref, o_ref, l_ref, acc_ref, *, tile, far_rows):
    G = qT_ref.shape[3]
    T = tile
    Bk = bias_ref.shape[2]
    nb = T // Bk
    sub = T // G
    nq = qT_ref.shape[1] // sub
    lam = _lambda(lq1_ref, lk1_ref, lq2_ref, lk2_ref)

    def lane_tiles(ref, first, count):
        return jnp.concatenate([ref[0, first + j] for j in range(count)], axis=1)

    def scores(first_group, count, q2):
        rows = pl.ds(pl.multiple_of(first_group * G, G), count * G)
        return jnp.dot(k_ref[0, rows, :], q2, preferred_element_type=F32)

    def weigh(first_group, count, s):
        p = jnp.exp2(s)
        return (jnp.sum(p, axis=0, keepdims=True),
                jnp.dot(lane_tiles(vT_ref, first_group, count), p.astype(BF16),
                        preferred_element_type=F32))

    def far_groups(first_group, count, q2):
        per = min(far_rows // G, count)
        firsts = [first_group + j for j in range(0, count, per)]
        parts = [weigh(g, per, scores(g, per, q2)) for g in firsts]
        l_ref[...] += functools.reduce(lambda x, y: x + y, [p[0] for p in parts])
        acc_ref[...] += functools.reduce(lambda x, y: x + y, [p[1] for p in parts])

    def q_tile(qi, carry):
        q2 = _split_components(lane_tiles(qT_ref, qi * sub, sub))

        def near(start, lead):
            s = scores(start * sub, 2 * sub, q2)
            rows = []
            for kb in range(2 * nb):
                pieces = []
                for comp in range(2):
                    for rb in range(nb):
                        lanes = slice(comp * T + rb * Bk, comp * T + (rb + 1) * Bk)
                        piece = s[kb * Bk:(kb + 1) * Bk, lanes]
                        d = kb - lead - rb
                        if d > 0:
                            piece = jnp.full_like(piece, NEG_INF)
                        elif d >= -1:
                            piece = piece + bias_ref[0, -d]
                        pieces.append(piece)
                rows.append(jnp.concatenate(pieces, axis=1))
            l_near, acc_near = weigh(start * sub, 2 * sub, jnp.concatenate(rows, axis=0))
            l_ref[...] = l_near
            acc_ref[...] = acc_near

        @pl.when(qi == 0)
        def _():
            near(0, 0)

        @pl.when(qi > 0)
        def _():
            near(qi - 1, nb)

        n_far = jnp.maximum(qi - 1, 0)

        def far(i, c):
            far_groups(i * FAR_TILES * sub, FAR_TILES * sub, q2)
            return c

        lax.fori_loop(0, n_far // FAR_TILES, far, 0)
        done = n_far - n_far % FAR_TILES
        chunk = FAR_TILES // 2
        while chunk >= 1:
            take = (n_far - done) >= chunk

            @pl.when(take)
            def _(done=done, chunk=chunk):
                far_groups(done * sub, chunk * sub, q2)

            done = done + jnp.where(take, chunk, 0)
            chunk //= 2

        on = acc_ref[...] * (1.0 / l_ref[...])
        o = on[:, :T] - lam * on[:, T:]
        rows = pl.ds(pl.multiple_of(qi * T, T), T)
        o_ref[0, rows, :] = _attn_finish(o, sub_ref).astype(o_ref.dtype)
        return carry

    lax.fori_loop(0, nq, q_tile, 0)


def _prompt_attn_online_kernel(qT_ref, k_ref, vT_ref, bias_ref, lq1_ref, lk1_ref, lq2_ref, lk2_ref,
                               sub_ref, o_ref, m_ref, l_ref, acc_ref, *, tile):
    T = tile
    sub = T // qT_ref.shape[3]
    nq = qT_ref.shape[1] // sub
    lam = _lambda(lq1_ref, lk1_ref, lq2_ref, lk2_ref)

    def lane_tiles(ref, i):
        return jnp.concatenate([ref[0, i * sub + j] for j in range(sub)], axis=1)

    def scores(ki, q2):
        kt = k_ref[0, pl.ds(pl.multiple_of(ki * T, T), T), :]
        return jnp.dot(kt, q2, preferred_element_type=F32)

    def pv(p, ki):
        return jnp.dot(lane_tiles(vT_ref, ki), p.astype(BF16), preferred_element_type=F32)

    def update(s, ki):
        m_prev = m_ref[...]
        m_new = jnp.maximum(m_prev, jnp.max(s, axis=0, keepdims=True))
        alpha = jnp.exp2(m_prev - m_new)
        p = jnp.exp2(s - m_new)
        l_ref[...] = alpha * l_ref[...] + jnp.sum(p, axis=0, keepdims=True)
        acc_ref[...] = alpha * acc_ref[...] + pv(p, ki)
        m_ref[...] = m_new

    def q_tile(qi, carry):
        q2 = _split_components(lane_tiles(qT_ref, qi))

        bias = bias_ref[0, 0]
        s = scores(qi, q2) + jnp.concatenate([bias, bias], axis=1)
        m = jnp.max(s, axis=0, keepdims=True)
        p = jnp.exp2(s - m)
        m_ref[...] = m
        l_ref[...] = jnp.sum(p, axis=0, keepdims=True)
        acc_ref[...] = pv(p, qi)
        kp = jnp.maximum(qi - 1, 0)
        bias = jnp.where(qi > 0, bias_ref[0, 1], NEG_INF)
        s = scores(kp, q2) + jnp.concatenate([bias, bias], axis=1)
        n_far = qi - 1

        update(s, kp)

        def far(ki, s_cur):
            s_next = scores(jnp.minimum(ki + 1, jnp.maximum(n_far - 1, 0)), q2)
            update(s_cur, ki)
            return s_next

        lax.fori_loop(0, n_far, far, scores(0, q2))

        on = acc_ref[...] * (1.0 / l_ref[...])
        o = on[:, :T] - lam * on[:, T:]
        rows = pl.ds(pl.multiple_of(qi * T, T), T)
        o_ref[0, rows, :] = _attn_finish(o, sub_ref).astype(o_ref.dtype)
        return carry

    lax.fori_loop(0, nq, q_tile, 0)


def _near_bias(rel_bias, tile, offsets):
    pos = jnp.arange(tile, dtype=jnp.int32)
    base = tile * max(offsets)
    idx = [_bucket_tile(pos + base, pos + base - o * tile, True) for o in offsets]
    return _bias_tiles(rel_bias, jnp.stack(idx), LOG2E)


def _prompt_attention(qT, kb, vT, rel_bias, w, *, fixed_shift):
    T = ATTN_TILE
    B, S = kb.shape[:2]
    G = qT.shape[-1]
    A = kb.shape[-1]
    H = A // V_DIM
    assert S % T == 0 and G % CHUNK == 0 and G >= MAX_DISTANCE and T == 2 * G
    lam_spec = pl.BlockSpec((1, HEAD_DIM), lambda b, h: (0, 0))
    tile_spec = pl.BlockSpec((1, S // G, V_DIM, G), lambda b, h: (b, 0, h, 0))
    seq_spec = pl.BlockSpec((1, S, V_DIM), lambda b, h: (b, 0, h))
    tmp_bytes = 4 * 8 * T * 2 * T
    if fixed_shift:
        assert BIAS_BLOCK % CHUNK == 0 and BIAS_BLOCK >= MAX_DISTANCE and T % BIAS_BLOCK == 0
        bias = _near_bias(rel_bias, BIAS_BLOCK, (0, 1))
        kern = functools.partial(_prompt_attn_fixed_kernel, tile=T, far_rows=FAR_ROWS)
        stats = []
    else:
        bias = _near_bias(rel_bias, T, (0, 1))
        kern = functools.partial(_prompt_attn_online_kernel, tile=T)
        stats = [pltpu.VMEM((1, 2 * T), F32)]
    blk_bytes = 2 * 2 * (3 * S * V_DIM) + 2 * 2 * S * V_DIM + 2 * 4 * bias[0].size
    return pl.pallas_call(
        kern,
        grid=(B, H),
        in_specs=[tile_spec, seq_spec, tile_spec,
                  pl.BlockSpec((1,) + bias.shape[1:], lambda b, h: (h, 0, 0, 0)),
                  lam_spec, lam_spec, lam_spec, lam_spec,
                  pl.BlockSpec((1, V_DIM), lambda b, h: (0, 0))],
        out_specs=seq_spec,
        out_shape=jax.ShapeDtypeStruct((B, S, A), BF16),
        scratch_shapes=stats + [pltpu.VMEM((1, 2 * T), F32), pltpu.VMEM((V_DIM, 2 * T), F32)],
        compiler_params=pltpu.CompilerParams(
            dimension_semantics=("parallel", "parallel"),
            vmem_limit_bytes=_vmem_limit(blk_bytes + tmp_bytes + 4 * V_DIM * 2 * T)),
        name="prompt_attn_fixed" if fixed_shift else "prompt_attn_online",
    )(qT, kb, vT, bias, w["lambda_q1"], w["lambda_k1"], w["lambda_q2"], w["lambda_k2"], w["subln"])


def _sample_attn_kernel(q_ref, kn_ref, vn_ref, ck_ref, cv_ref, bc_ref, bn_ref,
                        lq1_ref, lk1_ref, lq2_ref, lk2_ref, sub_ref, o_ref, *, near, shift_free):
    Q = q_ref.shape[1]
    P = ck_ref.shape[2]
    H = q_ref.shape[2] // V_DIM
    lam = _lambda(lq1_ref, lk1_ref, lq2_ref, lk2_ref)
    first = lax.broadcasted_iota(jnp.int32, (Q, V_DIM), 1) < HEAD_DIM
    nt = (((1,), (1,)), ((), ()))
    far = P - near
    for h in range(H):
        cols = slice(h * V_DIM, (h + 1) * V_DIM)
        qh = q_ref[0, :, cols]
        zero = jnp.zeros_like(qh)
        q2 = jnp.concatenate([jnp.where(first, qh, zero), jnp.where(first, zero, qh)], axis=0)
        k_far = ck_ref[0, cols, :far].astype(BF16)
        k_near = ck_ref[0, cols, far:].astype(BF16)
        bc = bc_ref[h]
        bn = bn_ref[h]
        s_far = jnp.dot(q2, k_far, preferred_element_type=F32)
        s_near = (jnp.dot(q2, k_near, preferred_element_type=F32)
                  + jnp.concatenate([bc, bc], axis=0))
        s_new = (lax.dot_general(q2, kn_ref[0, :, cols], nt, preferred_element_type=F32)
                 + jnp.concatenate([bn, bn], axis=0))
        if not shift_free:
            m = jnp.maximum(jnp.maximum(jnp.max(s_far, axis=-1, keepdims=True),
                                        jnp.max(s_near, axis=-1, keepdims=True)),
                            jnp.max(s_new, axis=-1, keepdims=True))
            s_far, s_near, s_new = s_far - m, s_near - m, s_new - m
        p_far = jnp.exp(s_far)
        p_near = jnp.exp(s_near)
        p_new = jnp.exp(s_new)
        l = (jnp.sum(p_far, axis=-1, keepdims=True) + jnp.sum(p_near, axis=-1, keepdims=True)
             + jnp.sum(p_new, axis=-1, keepdims=True))
        v_far = cv_ref[0, pl.ds(h, far, stride=H), :].astype(BF16)
        v_near = cv_ref[0, pl.ds(far * H + h, near, stride=H), :].astype(BF16)
        acc = (jnp.dot(p_far.astype(BF16), v_far, preferred_element_type=F32)
               + jnp.dot(p_near.astype(BF16), v_near, preferred_element_type=F32)
               + jnp.dot(p_new.astype(BF16), vn_ref[0, :, cols], preferred_element_type=F32))
        on = acc * (1.0 / l)
        o = on[:Q] - lam * on[Q:]
        o = o * lax.rsqrt(jnp.mean(o * o, axis=-1, keepdims=True) + EPS)
        o_ref[0, :, cols] = (o * sub_ref[...] * (1.0 - LAM_INIT)).astype(o_ref.dtype)


def _sample_attention(qb, kb, vb, cache_k, cache_v, bias_c, bias_n, w, *, shift_free):
    B, Q, A = qb.shape
    P = cache_k.shape[2]
    near = bias_c.shape[-1]
    new_spec = pl.BlockSpec((1, Q, A), lambda b: (b, 0, 0))
    ck_spec = pl.BlockSpec((1, A, P), lambda b: (b, 0, 0))
    cv_spec = pl.BlockSpec((1,) + cache_v.shape[1:], lambda b: (b, 0, 0))
    lam_spec = pl.BlockSpec((1, HEAD_DIM), lambda b: (0, 0))
    blk_bytes = 2 * 2 * 4 * P * A + 2 * 4 * 2 * Q * A
    tmp_bytes = 4 * 2 * Q * (3 * P + 2 * P) + 2 * 2 * P * V_DIM
    return pl.pallas_call(
        functools.partial(_sample_attn_kernel, near=near, shift_free=shift_free),
        grid=(B,),
        in_specs=[new_spec, new_spec, new_spec, ck_spec, cv_spec,
                  pl.BlockSpec(bias_c.shape, lambda b: (0, 0, 0)),
                  pl.BlockSpec(bias_n.shape, lambda b: (0, 0, 0)),
                  lam_spec, lam_spec, lam_spec, lam_spec,
                  pl.BlockSpec((1, V_DIM), lambda b: (0, 0))],
        out_specs=new_spec,
        out_shape=jax.ShapeDtypeStruct((B, Q, A), BF16),
        compiler_params=pltpu.CompilerParams(
            dimension_semantics=("parallel",),
            vmem_limit_bytes=_vmem_limit(blk_bytes + tmp_bytes)),
        name="sample_attn_free" if shift_free else "sample_attn_rowmax",
    )(qb, kb, vb, cache_k, cache_v, bias_c, bias_n,
      w["lambda_q1"], w["lambda_k1"], w["lambda_q2"], w["lambda_k2"], w["subln"])


def _lru_kernel(xr_ref, gate_ref, h0_ref, c0_ref, cw_ref, cb_ref, wgate_ref, bgate_ref, L_ref, gn_ref,
                perm_ref, r_ref, hl_ref, cn_ref, tail_ref, hc_ref):
    t = pl.program_id(1)
    bb, ts, W = xr_ref.shape
    gs = perm_ref.shape[1]
    n = gs // SUBLANES
    tail = CONV_WIDTH - 1
    row = lax.broadcasted_iota(jnp.int32, (SUBLANES, W), 0)

    @pl.when(t == 0)
    def _():
        tail_ref[:, SUBLANES - tail:, :] = c0_ref[...]
        hc_ref[...] = jnp.broadcast_to(h0_ref[...], hc_ref.shape)

    def regroup(x):
        hi = x.astype(BF16)
        lo = (x - hi.astype(F32)).astype(BF16)
        return (jnp.dot(perm_ref[0], hi, preferred_element_type=F32)
                + jnp.dot(perm_ref[0], lo, preferred_element_type=F32))

    def blocks(v):
        return [v[j * SUBLANES:(j + 1) * SUBLANES] for j in range(n)]

    def shift_subsequences(block, entering):
        return jnp.where(row == 0, entering, pltpu.roll(block, 1, 0))

    z = -L_ref[...]
    softplus = jnp.maximum(z, 0.0) + jnp.log1p(jnp.exp(-jnp.abs(z)))

    def coefficients(sid, rows, before):
        x = xr_ref[sid, rows, :]
        xp = blocks(regroup(x))
        lead = [shift_subsequences(xp[n - k], before[tail - k:tail - k + 1, :])
                for k in range(tail, 0, -1)]
        ext = lead + xp
        xc = cb_ref[...]
        for j in range(CONV_WIDTH):
            xc = xc + jnp.concatenate(ext[j:j + n], axis=0) * cw_ref[j:j + 1, :]

        gates = (jnp.dot(xc.astype(BF16), wgate_ref[...], preferred_element_type=F32)
                 + bgate_ref[...])
        r = _sigmoid(gates[:, :W])
        i = _sigmoid(gates[:, W:])
        log_a = -LRU_C * r * softplus
        a = blocks(jnp.exp(log_a))
        th = jnp.tanh(log_a)
        u = -2.0 * th / (1.0 - th)
        b = blocks(jnp.where(u > 0.0, u * lax.rsqrt(u), 0.0) * (i * xc))
        return x[gs - tail:, :], a, b

    def recur(sid, rows, a, b, h_prev):
        end_a, end_b = a[0], b[0]
        for j in range(1, n):
            end_b = a[j] * end_b + b[j]
            end_a = a[j] * end_a
        d = 1
        while d < SUBLANES:
            keep = row >= d
            a_sh = jnp.where(keep, pltpu.roll(end_a, d, 0), 1.0)
            b_sh = jnp.where(keep, pltpu.roll(end_b, d, 0), 0.0)
            end_b = end_a * b_sh + end_b
            end_a = end_a * a_sh
            d *= 2
        h_end = end_a * h_prev + end_b
        state = shift_subsequences(h_end, h_prev)
        states = []
        for j in range(n):
            state = a[j] * state + b[j]
            states.append(state)
        h = jnp.concatenate(states, axis=0)

        gate = jnp.dot(perm_ref[0], gate_ref[sid, rows, :], preferred_element_type=F32)
        normed = _rms(h * gate, gn_ref[...]).astype(BF16)
        r_ref[sid, rows, :] = jnp.dot(perm_ref[1], normed,
                                      preferred_element_type=F32).astype(r_ref.dtype)
        return h_end[SUBLANES - 1:, :]

    groups = []
    befores = []
    for sid in range(bb):
        before = tail_ref[sid, SUBLANES - tail:, :]
        for r0 in range(0, ts, gs):
            rows = slice(r0, r0 + gs)
            before, a, b = coefficients(sid, rows, before)
            groups.append((sid, rows, a, b))
        befores.append(before)
    states = [hc_ref[sid, 0:1, :] for sid in range(bb)]
    for sid, rows, a, b in groups:
        states[sid] = recur(sid, rows, a, b, states[sid])
    for sid in range(bb):
        tail_ref[sid, SUBLANES - tail:, :] = befores[sid]
        hc_ref[sid] = jnp.broadcast_to(states[sid], hc_ref.shape[1:])

    @pl.when(t == pl.num_programs(1) - 1)
    def _():
        for sid in range(bb):
            hl_ref[sid] = states[sid]
            cn_ref[sid] = befores[sid]


def _lru(xr, xg, h0, conv0, w):
    B, S, W = xr.shape
    ts = min(LRU_TILE, S)
    bb = math.gcd(B, max(1, LRU_TILE // ts))
    assert S % ts == 0 and ts % SUBLANES == 0 and S >= CONV_WIDTH - 1
    tail = CONV_WIDTH - 1
    seq = pl.BlockSpec((bb, ts, W), lambda b, t: (b, t, 0))
    per_b = lambda r: pl.BlockSpec((bb, r, W), lambda b, t: (b, 0, 0))
    const = lambda shape: pl.BlockSpec(shape, lambda b, t: (0, 0))
    gs = min(LRU_GROUP, ts)
    assert ts % gs == 0 and gs % SUBLANES == 0 and gs >= SUBLANES * (CONV_WIDTH - 1)
    n = gs // SUBLANES
    time = np.arange(gs)
    perm = np.zeros((2, gs, gs), np.float32)
    perm[0, SUBLANES * (time % n) + time // n, time] = 1.0
    perm[1] = perm[0].T
    return pl.pallas_call(
        _lru_kernel,
        grid=(B // bb, S // ts),
        in_specs=[seq, seq, per_b(1), per_b(tail),
                  const((CONV_WIDTH, W)), const((1, W)), const((W, 2 * W)), const((1, 2 * W)),
                  const((1, W)), const((1, W)),
                  pl.BlockSpec((2, gs, gs), lambda b, t: (0, 0, 0))],
        out_specs=[seq, per_b(1), per_b(tail)],
        out_shape=[jax.ShapeDtypeStruct((B, S, W), BF16),
                   jax.ShapeDtypeStruct((B, 1, W), F32),
                   jax.ShapeDtypeStruct((B, tail, W), F32)],
        scratch_shapes=[pltpu.VMEM((bb, SUBLANES, W), F32), pltpu.VMEM((bb, SUBLANES, W), F32)],
        compiler_params=pltpu.CompilerParams(dimension_semantics=("parallel", "arbitrary")),
        name="rg_lru",
    )(xr, xg, h0.reshape(B, 1, W), conv0, w["conv_w"], w["conv_b"], w["w_gate"], w["b_gate"],
      w["lru_L"], w["lru_out_norm"], jnp.asarray(perm, BF16))


def _block_diag(wb):
    n, c, d = wb.shape
    eye = jnp.eye(n, dtype=wb.dtype)
    return (eye[:, None, :, None] * wb[:, :, None, :]).reshape(n * c, n * d)


def _prepare_weights(l, p):
    D, d_ff = p["ffn1_gate"].shape[1:]
    assert d_ff % FF_CHUNK == 0
    lw = p["conv_w"].shape[-1]
    in_width = p["w_in"].shape[-1]
    a = (in_width - 2 * lw) // 3
    assert a % V_DIM == 0

    group = np.arange(a) // HEAD_DIM
    w = dict(attn_width=a, lru_width=lw)
    for name in ("ffn1_gate", "ffn1_up", "ffn1_down", "ffn2_gate", "ffn2_up", "ffn2_down"):
        w[name] = p[name][l].astype(BF16)
    for name in ("norm_ffn1", "norm_mix", "norm_ffn2", "lru_out_norm", "conv_b", "lru_L", "subln",
                 "lambda_q1", "lambda_k1", "lambda_q2", "lambda_k2"):
        w[name] = p[name][l][None, :]
    w["w_in"] = p["w_in"][l].astype(BF16)
    w["group_mat"] = jnp.asarray((group[:, None] == group[None, :]) / HEAD_DIM, BF16)
    w["q_gain"] = jnp.tile(p["q_norm"][l], a // HEAD_DIM)[None, :]
    w["k_gain"] = jnp.tile(p["k_norm"][l], a // HEAD_DIM)[None, :]
    w["w_out_attn"] = p["w_out"][l][:a].astype(BF16)
    w["w_out_lru"] = p["w_out"][l][a:].astype(BF16)
    w["conv_w"] = p["conv_w"][l]
    w["w_gate"] = jnp.concatenate([_block_diag(p["gate_a_w"][l]), _block_diag(p["gate_x_w"][l])],
                                  axis=1).astype(BF16)
    w["b_gate"] = jnp.concatenate([p["gate_a_b"][l], p["gate_x_b"][l]])[None, :]
    return w


def _scores_bounded(w, rel_bias):
    score_bound = (HEAD_DIM ** 0.5 * LOG2E * (1 + 2.0 ** -7)
                   * jnp.max(jnp.abs(w["q_gain"])) * jnp.max(jnp.abs(w["k_gain"])))
    bias_bound = 2 * LOG2E * jnp.max(jnp.abs(rel_bias))
    return score_bound + bias_bound <= SCORE_RANGE


def _layer_prompt(x, w, rel_bias, bounded):
    B, S, D = x.shape
    T = ATTN_TILE
    a, lw = w["attn_width"], w["lru_width"]
    H = a // V_DIM
    x1, kT, v, xr, xg, qT, kb, vT = _stage_a(x, w, transposed=True)
    k = kT.reshape(B, H, 2, HEAD_DIM, S).transpose(0, 4, 1, 2, 3)
    nl = S // LAYOUT_TILE
    attend = lambda fixed: functools.partial(_prompt_attention, w=w, fixed_shift=fixed)
    o = lax.cond(bounded, attend(True), attend(False),
                 qT.reshape(B, nl, a, LAYOUT_TILE), kb.reshape(B, S, a),
                 vT.reshape(B, nl, a, LAYOUT_TILE), rel_bias)
    r, h_last, conv_new = _lru(xr.reshape(B, S, lw), xg.reshape(B, S, lw),
                               jnp.zeros((B, lw), F32), jnp.zeros((B, CONV_WIDTH - 1, lw), F32), w)
    y = _stage_d(x1, o.reshape(B * S, a), r.reshape(B * S, lw), w)
    return (y.reshape(B, S, D), k, v.reshape(B, S, H, V_DIM), h_last.reshape(B, lw), conv_new)


def _layer_sample(x, w, rel_bias, bounded, k_past, v_past, h0, conv0):
    B, Q, D = x.shape
    P = k_past.shape[1]
    a, lw = w["attn_width"], w["lru_width"]
    H = a // V_DIM
    near = LANES
    assert P % CHUNK == 0 and Q <= CHUNK and near >= MAX_DISTANCE and P > near
    x1, k, v, xr, xg, qb, kb, vb = _stage_a(x, w, transposed=False)
    q_pos = P + jnp.arange(Q, dtype=jnp.int32)
    idx_c = _bucket_tile(q_pos, jnp.arange(P - near, P, dtype=jnp.int32), False)
    idx_n = _bucket_tile(q_pos, q_pos, False)
    bias_c = _bias_tiles(rel_bias, idx_c[None])[:, 0]
    bias_n = _bias_tiles(rel_bias, idx_n[None])[:, 0]
    attend = lambda free: functools.partial(_sample_attention, w=w, shift_free=free)
    o = lax.cond(bounded, attend(True), attend(False),
                 qb.reshape(B, Q, a), kb.reshape(B, Q, a), vb.reshape(B, Q, a),
                 k_past.transpose(0, 2, 3, 4, 1).reshape(B, a, P),
                 v_past.reshape(B, P * H, V_DIM), bias_c, bias_n)
    r, h_last, conv_new = _lru(xr.reshape(B, Q, lw), xg.reshape(B, Q, lw), h0, conv0, w)
    y = _stage_d(x1, o.reshape(B * Q, a), r.reshape(B * Q, lw), w)
    return (y.reshape(B, Q, D), k.reshape(B, Q, H, 2, HEAD_DIM), v.reshape(B, Q, H, V_DIM),
            h_last.reshape(B, lw), conv_new)


def kernel(x_prompt, x_sample, cache_k, cache_v, state_lru, state_conv, rel_bias, norm_ffn1, ffn1_gate, ffn1_up, ffn1_down, norm_mix, w_in, q_norm, k_norm, lambda_q1, lambda_k1, lambda_q2, lambda_k2, subln, conv_w, conv_b, gate_a_w, gate_a_b, gate_x_w, gate_x_b, lru_L, lru_out_norm, w_out, norm_ffn2, ffn2_gate, ffn2_up, ffn2_down):
    p = dict(norm_ffn1=norm_ffn1, ffn1_gate=ffn1_gate, ffn1_up=ffn1_up, ffn1_down=ffn1_down,
             norm_mix=norm_mix, w_in=w_in, q_norm=q_norm, k_norm=k_norm,
             lambda_q1=lambda_q1, lambda_k1=lambda_k1, lambda_q2=lambda_q2, lambda_k2=lambda_k2,
             subln=subln, conv_w=conv_w, conv_b=conv_b, gate_a_w=gate_a_w, gate_a_b=gate_a_b,
             gate_x_w=gate_x_w, gate_x_b=gate_x_b, lru_L=lru_L, lru_out_norm=lru_out_norm,
             w_out=w_out, norm_ffn2=norm_ffn2, ffn2_gate=ffn2_gate, ffn2_up=ffn2_up,
             ffn2_down=ffn2_down)
    depth = cache_k.shape[0]
    assert depth == 1, "the lambda initial value is specialised to a single layer"
    w = _prepare_weights(0, p)
    bounded = _scores_bounded(w, rel_bias)
    yp, kp, vp, hp, cp = _layer_prompt(x_prompt, w, rel_bias, bounded)
    ys, kn, vn, hn, cn = _layer_sample(x_sample, w, rel_bias, bounded, cache_k[0], cache_v[0],
                                       state_lru[0], state_conv[0])
    stack = lambda t: t[None]
    return (yp, ys, stack(kp), stack(vp), stack(hp), stack(cp),
            stack(kn), stack(vn), stack(hn), stack(cn))
```

```python
import functools
import math

import numpy as np
import jax
import jax.numpy as jnp
from jax import lax
from jax.experimental import pallas as pl
from jax.experimental.pallas import tpu as pltpu

F32 = jnp.float32
BF16 = jnp.bfloat16

HEAD_DIM = 64
V_DIM = 2 * HEAD_DIM
CHUNK = 64
NUM_BUCKETS = 32
MAX_DISTANCE = 128
CONV_WIDTH = 4
LRU_C = 8.0
EPS = 1e-6
NEG_INF = -1e30
LAM_INIT = 0.8 - 0.6 * math.exp(-0.3 * 0)

LANES = 128
SUBLANES = 8
MXU_DIM = 256
MIB = 1024 * 1024
VMEM_BYTES = 64 * MIB
VMEM_RESERVED_BYTES = 4 * MIB
VMEM_DEFAULT_LIMIT_BYTES = 16 * MIB

FF_CHUNK = MXU_DIM
ATTN_TILE = 512
LAYOUT_TILE = 256
BIAS_BLOCK = LANES
FAR_ROWS = 1024
FAR_TILES = 8
LOG2E = 1.4426950408889634
SCORE_RANGE = 60.0
TOKEN_TILE = 512
STAGE_D_TILE = 1024
MIN_STEPS = 4
LRU_TILE = 1024
LRU_GROUP = 256


def _vmem_limit(nbytes):
    return int(min(VMEM_BYTES - VMEM_RESERVED_BYTES, max(nbytes, VMEM_DEFAULT_LIMIT_BYTES)))


def _rms(x, g):
    ms = jnp.mean(x * x, axis=-1, keepdims=True)
    return x * lax.rsqrt(ms + EPS) * g


def _sigmoid(x):
    return 0.5 * jnp.tanh(0.5 * x) + 0.5


def _gelu_tanh(x):
    return 0.5 * x * (1.0 + jnp.tanh(math.sqrt(2.0 / math.pi) * (x + 0.044715 * (x * x * x))))


def _const_spec(shape):
    n = len(shape)
    return pl.BlockSpec(shape, lambda *_: (0,) * n, pipeline_mode=pl.Buffered(1))


def _swiglu_into(acc_ref, xn, wg_ref, wu_ref, wd_ref):
    d_ff = wg_ref.shape[1]
    for c in range(0, d_ff, FF_CHUNK):
        cols = slice(c, c + FF_CHUNK)
        g = jnp.dot(xn, wg_ref[:, cols], preferred_element_type=F32)
        u = jnp.dot(xn, wu_ref[:, cols], preferred_element_type=F32)
        h = (g * _sigmoid(g) * u).astype(BF16)
        part = jnp.dot(h, wd_ref[cols, :], preferred_element_type=F32)
        if c == 0:
            acc_ref[...] = part
        else:
            acc_ref[...] += part


def _stage_a_kernel(x_ref, g1_ref, wg_ref, wu_ref, wd_ref, gm_ref, win_ref, gmat_ref, qg_ref, kg_ref,
                    x1_ref, k_ref, v_ref, xr_ref, xg_ref, qb_ref, kb_ref, vb_ref, acc_ref,
                    *, attn_width, lru_width, transposed, tile, q_scale):
    x = x_ref[...]
    xn = _rms(x, g1_ref[...]).astype(BF16)
    _swiglu_into(acc_ref, xn, wg_ref, wu_ref, wd_ref)
    x1 = x + 0.5 * acc_ref[...]
    x1_ref[...] = x1
    hn = _rms(x1, gm_ref[...]).astype(BF16)
    proj = jnp.dot(hn, win_ref[...], preferred_element_type=F32)
    a = attn_width
    q = proj[:, :a]
    k = proj[:, a:2 * a]
    v = proj[:, 2 * a:3 * a]
    xr_ref[...] = proj[:, 3 * a:3 * a + lru_width]
    xg_ref[...] = _gelu_tanh(proj[:, 3 * a + lru_width:]).astype(BF16)

    def group_norm(t, g):
        ms = jnp.dot((t * t).astype(BF16), gmat_ref[...], preferred_element_type=F32)
        return t * lax.rsqrt(ms + EPS) * g

    qn = group_norm(q, qg_ref[...]) * q_scale
    kn = group_norm(k, kg_ref[...])
    n_heads = a // V_DIM
    for h in range(n_heads):
        v_ref[pl.ds(h, x.shape[0], stride=n_heads), :] = v[:, h * V_DIM:(h + 1) * V_DIM]
    kb_ref[...] = kn.astype(BF16)
    if transposed:
        k_ref[0] = kn.T
        for j in range(x.shape[0] // tile):
            rows = slice(j * tile, (j + 1) * tile)
            qb_ref[0, j] = qn[rows, :].T.astype(BF16)
            vb_ref[0, j] = v[rows, :].T.astype(BF16)
    else:
        n_vec = a // HEAD_DIM
        pad = jnp.zeros((x.shape[0], LANES - HEAD_DIM), F32)
        for g in range(n_vec):
            vec = jnp.concatenate([kn[:, g * HEAD_DIM:(g + 1) * HEAD_DIM], pad], axis=1)
            k_ref[pl.ds(g, x.shape[0], stride=n_vec), :] = vec
        qb_ref[...] = qn.astype(BF16)
        vb_ref[...] = v.astype(BF16)


def _stage_a(x, w, *, transposed):
    B, S, D = x.shape
    a = w["attn_width"]
    lw = w["lru_width"]
    tm = TOKEN_TILE
    T = LAYOUT_TILE
    q_scale = HEAD_DIM ** -0.5 * (LOG2E if transposed else 1.0)
    assert (B * S) % tm == 0 and (not transposed or (tm % T == 0 and S % tm == 0))
    nt = S // tm
    x2 = x.reshape(B * S, D)
    d_ff = w["ffn1_gate"].shape[1]
    tok = lambda width: pl.BlockSpec((tm, width), lambda i: (i, 0))
    out_shape = [
        jax.ShapeDtypeStruct((B * S, D), F32),
        jax.ShapeDtypeStruct((B * S, a), F32),
        jax.ShapeDtypeStruct((B * S, a), F32),
        jax.ShapeDtypeStruct((B * S, lw), F32),
        jax.ShapeDtypeStruct((B * S, lw), BF16),
    ]
    out_specs = [tok(D), tok(a), tok(a), tok(lw), tok(lw)]
    H = a // V_DIM
    out_shape[2] = jax.ShapeDtypeStruct((B * S * H, V_DIM), F32)
    out_specs[2] = pl.BlockSpec((tm * H, V_DIM), lambda i: (i, 0))
    if transposed:
        out_shape[1] = jax.ShapeDtypeStruct((B, a, S), F32)
        out_specs[1] = pl.BlockSpec((1, a, tm), lambda i: (i // nt, 0, i % nt))
        tshape = jax.ShapeDtypeStruct((B * nt, tm // T, a, T), BF16)
        tspec = pl.BlockSpec((1, tm // T, a, T), lambda i: (i, 0, 0, 0))
        out_shape += [tshape, jax.ShapeDtypeStruct((B * S, a), BF16), tshape]
        out_specs += [tspec, tok(a), tspec]
    else:
        n_vec = a // HEAD_DIM
        out_shape[1] = jax.ShapeDtypeStruct((B * S * n_vec, LANES), F32)
        out_specs[1] = pl.BlockSpec((tm * n_vec, LANES), lambda i: (i, 0))
        bshape = jax.ShapeDtypeStruct((B * S, a), BF16)
        out_shape += [bshape, bshape, bshape]
        out_specs += [tok(a), tok(a), tok(a)]
    weight_bytes = 2 * (3 * D * d_ff + D * (3 * a + 2 * lw) + a * a)
    act_bytes = 4 * tm * (2 * 2 * D + 2 * 2 * (2 * a + 2 * lw) + D + 2 * (3 * a + 2 * lw) + 4 * FF_CHUNK)
    kern = functools.partial(_stage_a_kernel, attn_width=a, lru_width=lw, transposed=transposed,
                             tile=T, q_scale=q_scale)
    outs = pl.pallas_call(
        kern,
        grid=(B * S // tm,),
        in_specs=[
            tok(D),
            _const_spec((1, D)),
            _const_spec(w["ffn1_gate"].shape),
            _const_spec(w["ffn1_up"].shape),
            _const_spec(w["ffn1_down"].shape),
            _const_spec((1, D)),
            _const_spec(w["w_in"].shape),
            _const_spec((a, a)),
            _const_spec((1, a)),
            _const_spec((1, a)),
        ],
        out_specs=out_specs,
        out_shape=out_shape,
        scratch_shapes=[pltpu.VMEM((tm, D), F32)],
        compiler_params=pltpu.CompilerParams(
            dimension_semantics=("parallel",),
            vmem_limit_bytes=_vmem_limit(weight_bytes + act_bytes)),
        name="stage_a_t" if transposed else "stage_a_n",
    )(x2, w["norm_ffn1"], w["ffn1_gate"], w["ffn1_up"], w["ffn1_down"], w["norm_mix"], w["w_in"],
      w["group_mat"], w["q_gain"], w["k_gain"])
    return outs


def _stage_d_kernel(x1_ref, o_ref, r_ref, woa_ref, wor_ref, g2_ref, wg_ref, wu_ref, wd_ref,
                    y_ref, acc_ref):
    x2 = (x1_ref[...]
          + jnp.dot(o_ref[...], woa_ref[...], preferred_element_type=F32)
          + jnp.dot(r_ref[...], wor_ref[...], preferred_element_type=F32))
    xn = _rms(x2, g2_ref[...]).astype(BF16)
    _swiglu_into(acc_ref, xn, wg_ref, wu_ref, wd_ref)
    y_ref[...] = x2 + 0.5 * acc_ref[...]


def _stage_d(x1, o, r, w):
    N, D = x1.shape
    a = o.shape[1]
    lw = r.shape[1]
    tm = STAGE_D_TILE if N >= MIN_STEPS * STAGE_D_TILE else min(TOKEN_TILE, N)
    assert N % tm == 0
    d_ff = w["ffn2_gate"].shape[1]
    tok = lambda width: pl.BlockSpec((tm, width), lambda i: (i, 0))
    weight_bytes = 2 * (3 * D * d_ff + D * D)
    act_bytes = 4 * tm * (2 * 2 * D + 2 * (a + lw) + 2 * D + 4 * FF_CHUNK)
    return pl.pallas_call(
        _stage_d_kernel,
        grid=(N // tm,),
        in_specs=[
            tok(D), tok(a), tok(lw),
            _const_spec((a, D)), _const_spec((lw, D)), _const_spec((1, D)),
            _const_spec(w["ffn2_gate"].shape), _const_spec(w["ffn2_up"].shape),
            _const_spec(w["ffn2_down"].shape),
        ],
        out_specs=tok(D),
        out_shape=jax.ShapeDtypeStruct((N, D), F32),
        scratch_shapes=[pltpu.VMEM((tm, D), F32)],
        compiler_params=pltpu.CompilerParams(
            dimension_semantics=("parallel",),
            vmem_limit_bytes=_vmem_limit(weight_bytes + act_bytes)),
        name="stage_d",
    )(x1, o, r, w["w_out_attn"], w["w_out_lru"], w["norm_ffn2"],
      w["ffn2_gate"], w["ffn2_up"], w["ffn2_down"])


def _t5_bucket(rel):
    n = NUM_BUCKETS // 2
    max_exact = n // 2
    ret = jnp.where(rel > 0, n, 0)
    rel = jnp.abs(rel)
    relf = jnp.maximum(rel, 1).astype(jnp.float32)
    large = max_exact + (jnp.log(relf / max_exact) / math.log(MAX_DISTANCE / max_exact)
                         * (n - max_exact)).astype(jnp.int32)
    large = jnp.minimum(large, n - 1)
    return ret + jnp.where(rel < max_exact, rel, large)


FAR_BUCKET = NUM_BUCKETS // 2 - 1


def _bucket_tile(q_pos, k_pos, keys_first):
    rel = k_pos[None, :] - q_pos[:, None]
    visible = (k_pos[None, :] // CHUNK) <= (q_pos[:, None] // CHUNK)
    idx = jnp.where(visible, _t5_bucket(rel), -1)
    return idx.T if keys_first else idx


def _bias_kernel(table_ref, idx_ref, out_ref, *, scale):
    h = pl.program_id(0)
    idx = idx_ref[0]
    acc = jnp.zeros(idx.shape, F32)
    for b in range(NUM_BUCKETS):
        acc = jnp.where(idx == b, table_ref[b, h], acc)
    out_ref[0, 0] = jnp.where(idx < 0, NEG_INF, (acc - table_ref[FAR_BUCKET, h]) * scale)


def _bias_tiles(rel_bias, idx, scale=1.0):
    n, R, C = idx.shape
    H = rel_bias.shape[1]
    return pl.pallas_call(
        functools.partial(_bias_kernel, scale=scale),
        grid=(H, n),
        in_specs=[pl.BlockSpec(memory_space=pltpu.SMEM),
                  pl.BlockSpec((1, R, C), lambda h, i: (i, 0, 0))],
        out_specs=pl.BlockSpec((1, 1, R, C), lambda h, i: (h, i, 0, 0)),
        out_shape=jax.ShapeDtypeStruct((H, n, R, C), F32),
        name="bias_tiles",
    )(rel_bias, idx)


def _lambda(lq1_ref, lk1_ref, lq2_ref, lk2_ref):
    s1 = jnp.sum(lq1_ref[...] * lk1_ref[...], axis=-1, keepdims=True)
    s2 = jnp.sum(lq2_ref[...] * lk2_ref[...], axis=-1, keepdims=True)
    return jnp.exp(s1) - jnp.exp(s2) + LAM_INIT


def _split_components(qt):
    first = lax.broadcasted_iota(jnp.int32, qt.shape, 0) < HEAD_DIM
    zero = jnp.zeros_like(qt)
    return jnp.concatenate([jnp.where(first, qt, zero), jnp.where(first, zero, qt)], axis=1)


def _attn_finish(o, sub_ref):
    o = o * lax.rsqrt(jnp.mean(o * o, axis=0, keepdims=True) + EPS)
    return o.T * sub_ref[...] * (1.0 - LAM_INIT)


def _prompt_attn_fixed_kernel(qT_ref, k_ref, vT_ref, bias_ref, lq1_ref, lk1_ref, lq2_ref, lk2_ref,
                              sub_ref, o_ref, l_ref, acc_ref, *, tile, far_rows):
    G = qT_ref.shape[3]
    T = tile
    Bk = bias_ref.shape[2]
    nb = T // Bk
    sub = T // G
    nq = qT_ref.shape[1] // sub
    lam = _lambda(lq1_ref, lk1_ref, lq2_ref, lk2_ref)

    def lane_tiles(ref, first, count):
        return jnp.concatenate([ref[0, first + j] for j in range(count)], axis=1)

    def scores(first_group, count, q2):
        rows = pl.ds(pl.multiple_of(first_group * G, G), count * G)
        return jnp.dot(k_ref[0, rows, :], q2, preferred_element_type=F32)

    def weigh(first_group, count, s):
        p = jnp.exp2(s)
        return (jnp.sum(p, axis=0, keepdims=True),
                jnp.dot(lane_tiles(vT_ref, first_group, count), p.astype(BF16),
                        preferred_element_type=F32))

    def far_groups(first_group, count, q2):
        per = min(far_rows // G, count)
        firsts = [first_group + j for j in range(0, count, per)]
        parts = [weigh(g, per, scores(g, per, q2)) for g in firsts]
        l_ref[...] += functools.reduce(lambda x, y: x + y, [p[0] for p in parts])
        acc_ref[...] += functools.reduce(lambda x, y: x + y, [p[1] for p in parts])

    def q_tile(qi, carry):
        q2 = _split_components(lane_tiles(qT_ref, qi * sub, sub))

        def near(start, count, lead):
            s = scores(start * sub, count * sub, q2)
            rows = []
            for kb in range(count * nb):
                pieces = []
                for comp in range(2):
                    for rb in range(nb):
                        lanes = slice(comp * T + rb * Bk, comp * T + (rb + 1) * Bk)
                        piece = s[kb * Bk:(kb + 1) * Bk, lanes]
                        d = kb - lead - rb
                        if d > 0:
                            piece = jnp.full_like(piece, NEG_INF)
                        elif d >= -1:
                            piece = piece + bias_ref[0, -d]
                        pieces.append(piece)
                rows.append(jnp.concatenate(pieces, axis=1))
            l_near, acc_near = weigh(start * sub, count * sub, jnp.concatenate(rows, axis=0))
            l_ref[...] = l_near
            acc_ref[...] = acc_near

        @pl.when(qi == 0)
        def _():
            near(0, 1, 0)

        @pl.when(qi > 0)
        def _():
            near(qi - 1, 2, nb)

        n_far = jnp.maximum(qi - 1, 0)

        def far(i, c):
            far_groups(i * FAR_TILES * sub, FAR_TILES * sub, q2)
            return c

        lax.fori_loop(0, n_far // FAR_TILES, far, 0)
        done = n_far - n_far % FAR_TILES
        chunk = FAR_TILES // 2
        while chunk >= 1:
            take = (n_far - done) >= chunk

            @pl.when(take)
            def _(done=done, chunk=chunk):
                far_groups(done * sub, chunk * sub, q2)

            done = done + jnp.where(take, chunk, 0)
            chunk //= 2

        on = acc_ref[...] * (1.0 / l_ref[...])
        o = on[:, :T] - lam * on[:, T:]
        rows = pl.ds(pl.multiple_of(qi * T, T), T)
        o_ref[0, rows, :] = _attn_finish(o, sub_ref).astype(o_ref.dtype)
        return carry

    lax.fori_loop(0, nq, q_tile, 0)


def _prompt_attn_online_kernel(qT_ref, k_ref, vT_ref, bias_ref, lq1_ref, lk1_ref, lq2_ref, lk2_ref,
                               sub_ref, o_ref, m_ref, l_ref, acc_ref, *, tile):
    T = tile
    sub = T // qT_ref.shape[3]
    nq = qT_ref.shape[1] // sub
    lam = _lambda(lq1_ref, lk1_ref, lq2_ref, lk2_ref)

    def lane_tiles(ref, i):
        return jnp.concatenate([ref[0, i * sub + j] for j in range(sub)], axis=1)

    def scores(ki, q2):
        kt = k_ref[0, pl.ds(pl.multiple_of(ki * T, T), T), :]
        return jnp.dot(kt, q2, preferred_element_type=F32)

    def pv(p, ki):
        return jnp.dot(lane_tiles(vT_ref, ki), p.astype(BF16), preferred_element_type=F32)

    def update(s, ki):
        m_prev = m_ref[...]
        m_new = jnp.maximum(m_prev, jnp.max(s, axis=0, keepdims=True))
        alpha = jnp.exp2(m_prev - m_new)
        p = jnp.exp2(s - m_new)
        l_ref[...] = alpha * l_ref[...] + jnp.sum(p, axis=0, keepdims=True)
        acc_ref[...] = alpha * acc_ref[...] + pv(p, ki)
        m_ref[...] = m_new

    def q_tile(qi, carry):
        q2 = _split_components(lane_tiles(qT_ref, qi))

        bias = bias_ref[0, 0]
        s = scores(qi, q2) + jnp.concatenate([bias, bias], axis=1)
        m = jnp.max(s, axis=0, keepdims=True)
        p = jnp.exp2(s - m)
        m_ref[...] = m
        l_ref[...] = jnp.sum(p, axis=0, keepdims=True)
        acc_ref[...] = pv(p, qi)
        kp = jnp.maximum(qi - 1, 0)
        bias = jnp.where(qi > 0, bias_ref[0, 1], NEG_INF)
        s = scores(kp, q2) + jnp.concatenate([bias, bias], axis=1)
        n_far = qi - 1

        update(s, kp)

        def far(ki, s_cur):
            s_next = scores(jnp.minimum(ki + 1, jnp.maximum(n_far - 1, 0)), q2)
            update(s_cur, ki)
            return s_next

        lax.fori_loop(0, n_far, far, scores(0, q2))

        on = acc_ref[...] * (1.0 / l_ref[...])
        o = on[:, :T] - lam * on[:, T:]
        rows = pl.ds(pl.multiple_of(qi * T, T), T)
        o_ref[0, rows, :] = _attn_finish(o, sub_ref).astype(o_ref.dtype)
        return carry

    lax.fori_loop(0, nq, q_tile, 0)


def _near_bias(rel_bias, tile, offsets):
    pos = jnp.arange(tile, dtype=jnp.int32)
    base = tile * max(offsets)
    idx = [_bucket_tile(pos + base, pos + base - o * tile, True) for o in offsets]
    return _bias_tiles(rel_bias, jnp.stack(idx), LOG2E)


def _prompt_attention(qT, kb, vT, rel_bias, w, *, fixed_shift):
    T = ATTN_TILE
    B, S = kb.shape[:2]
    G = qT.shape[-1]
    A = kb.shape[-1]
    H = A // V_DIM
    assert S % T == 0 and G % CHUNK == 0 and G >= MAX_DISTANCE and T == 2 * G
    lam_spec = pl.BlockSpec((1, HEAD_DIM), lambda b, h: (0, 0))
    tile_spec = pl.BlockSpec((1, S // G, V_DIM, G), lambda b, h: (b, 0, h, 0))
    seq_spec = pl.BlockSpec((1, S, V_DIM), lambda b, h: (b, 0, h))
    tmp_bytes = 4 * 8 * T * 2 * T
    if fixed_shift:
        assert BIAS_BLOCK % CHUNK == 0 and BIAS_BLOCK >= MAX_DISTANCE and T % BIAS_BLOCK == 0
        bias = _near_bias(rel_bias, BIAS_BLOCK, (0, 1))
        kern = functools.partial(_prompt_attn_fixed_kernel, tile=T, far_rows=FAR_ROWS)
        stats = []
    else:
        bias = _near_bias(rel_bias, T, (0, 1))
        kern = functools.partial(_prompt_attn_online_kernel, tile=T)
        stats = [pltpu.VMEM((1, 2 * T), F32)]
    blk_bytes = 2 * 2 * (3 * S * V_DIM) + 2 * 2 * S * V_DIM + 2 * 4 * bias[0].size
    return pl.pallas_call(
        kern,
        grid=(B, H),
        in_specs=[tile_spec, seq_spec, tile_spec,
                  pl.BlockSpec((1,) + bias.shape[1:], lambda b, h: (h, 0, 0, 0)),
                  lam_spec, lam_spec, lam_spec, lam_spec,
                  pl.BlockSpec((1, V_DIM), lambda b, h: (0, 0))],
        out_specs=seq_spec,
        out_shape=jax.ShapeDtypeStruct((B, S, A), BF16),
        scratch_shapes=stats + [pltpu.VMEM((1, 2 * T), F32), pltpu.VMEM((V_DIM, 2 * T), F32)],
        compiler_params=pltpu.CompilerParams(
            dimension_semantics=("parallel", "parallel"),
            vmem_limit_bytes=_vmem_limit(blk_bytes + tmp_bytes + 4 * V_DIM * 2 * T)),
        name="prompt_attn_fixed" if fixed_shift else "prompt_attn_online",
    )(qT, kb, vT, bias, w["lambda_q1"], w["lambda_k1"], w["lambda_q2"], w["lambda_k2"], w["subln"])


def _sample_attn_kernel(q_ref, kn_ref, vn_ref, ck_ref, cv_ref, bc_ref, bn_ref,
                        lq1_ref, lk1_ref, lq2_ref, lk2_ref, sub_ref, o_ref, *, near, shift_free):
    Q = q_ref.shape[1]
    P = ck_ref.shape[2]
    H = q_ref.shape[2] // V_DIM
    lam = _lambda(lq1_ref, lk1_ref, lq2_ref, lk2_ref)
    first = lax.broadcasted_iota(jnp.int32, (Q, V_DIM), 1) < HEAD_DIM
    nt = (((1,), (1,)), ((), ()))
    far = P - near
    for h in range(H):
        cols = slice(h * V_DIM, (h + 1) * V_DIM)
        qh = q_ref[0, :, cols]
        zero = jnp.zeros_like(qh)
        q2 = jnp.concatenate([jnp.where(first, qh, zero), jnp.where(first, zero, qh)], axis=0)
        k_far = ck_ref[0, cols, :far].astype(BF16)
        k_near = ck_ref[0, cols, far:].astype(BF16)
        bc = bc_ref[h]
        bn = bn_ref[h]
        s_far = jnp.dot(q2, k_far, preferred_element_type=F32)
        s_near = (jnp.dot(q2, k_near, preferred_element_type=F32)
                  + jnp.concatenate([bc, bc], axis=0))
        s_new = (lax.dot_general(q2, kn_ref[0, :, cols], nt, preferred_element_type=F32)
                 + jnp.concatenate([bn, bn], axis=0))
        if not shift_free:
            m = jnp.maximum(jnp.maximum(jnp.max(s_far, axis=-1, keepdims=True),
                                        jnp.max(s_near, axis=-1, keepdims=True)),
                            jnp.max(s_new, axis=-1, keepdims=True))
            s_far, s_near, s_new = s_far - m, s_near - m, s_new - m
        p_far = jnp.exp(s_far)
        p_near = jnp.exp(s_near)
        p_new = jnp.exp(s_new)
        l = (jnp.sum(p_far, axis=-1, keepdims=True) + jnp.sum(p_near, axis=-1, keepdims=True)
             + jnp.sum(p_new, axis=-1, keepdims=True))
        v_far = cv_ref[0, pl.ds(h, far, stride=H), :].astype(BF16)
        v_near = cv_ref[0, pl.ds(far * H + h, near, stride=H), :].astype(BF16)
        acc = (jnp.dot(p_far.astype(BF16), v_far, preferred_element_type=F32)
               + jnp.dot(p_near.astype(BF16), v_near, preferred_element_type=F32)
               + jnp.dot(p_new.astype(BF16), vn_ref[0, :, cols], preferred_element_type=F32))
        on = acc * (1.0 / l)
        o = on[:Q] - lam * on[Q:]
        o = o * lax.rsqrt(jnp.mean(o * o, axis=-1, keepdims=True) + EPS)
        o_ref[0, :, cols] = (o * sub_ref[...] * (1.0 - LAM_INIT)).astype(o_ref.dtype)


def _sample_attention(qb, kb, vb, cache_k, cache_v, bias_c, bias_n, w, *, shift_free):
    B, Q, A = qb.shape
    P = cache_k.shape[2]
    near = bias_c.shape[-1]
    new_spec = pl.BlockSpec((1, Q, A), lambda b: (b, 0, 0))
    ck_spec = pl.BlockSpec((1, A, P), lambda b: (b, 0, 0))
    cv_spec = pl.BlockSpec((1,) + cache_v.shape[1:], lambda b: (b, 0, 0))
    lam_spec = pl.BlockSpec((1, HEAD_DIM), lambda b: (0, 0))
    blk_bytes = 2 * 2 * 4 * P * A + 2 * 4 * 2 * Q * A
    tmp_bytes = 4 * 2 * Q * (3 * P + 2 * P) + 2 * 2 * P * V_DIM
    return pl.pallas_call(
        functools.partial(_sample_attn_kernel, near=near, shift_free=shift_free),
        grid=(B,),
        in_specs=[new_spec, new_spec, new_spec, ck_spec, cv_spec,
                  pl.BlockSpec(bias_c.shape, lambda b: (0, 0, 0)),
                  pl.BlockSpec(bias_n.shape, lambda b: (0, 0, 0)),
                  lam_spec, lam_spec, lam_spec, lam_spec,
                  pl.BlockSpec((1, V_DIM), lambda b: (0, 0))],
        out_specs=new_spec,
        out_shape=jax.ShapeDtypeStruct((B, Q, A), BF16),
        compiler_params=pltpu.CompilerParams(
            dimension_semantics=("parallel",),
            vmem_limit_bytes=_vmem_limit(blk_bytes + tmp_bytes)),
        name="sample_attn_free" if shift_free else "sample_attn_rowmax",
    )(qb, kb, vb, cache_k, cache_v, bias_c, bias_n,
      w["lambda_q1"], w["lambda_k1"], w["lambda_q2"], w["lambda_k2"], w["subln"])


def _lru_kernel(xr_ref, gate_ref, h0_ref, c0_ref, cw_ref, cb_ref, wgate_ref, bgate_ref, L_ref, gn_ref,
                perm_ref, r_ref, hl_ref, cn_ref, tail_ref, hc_ref):
    t = pl.program_id(1)
    bb, ts, W = xr_ref.shape
    gs = perm_ref.shape[1]
    n = gs // SUBLANES
    tail = CONV_WIDTH - 1
    row = lax.broadcasted_iota(jnp.int32, (SUBLANES, W), 0)

    @pl.when(t == 0)
    def _():
        tail_ref[:, SUBLANES - tail:, :] = c0_ref[...]
        hc_ref[...] = jnp.broadcast_to(h0_ref[...], hc_ref.shape)

    def regroup(x):
        hi = x.astype(BF16)
        lo = (x - hi.astype(F32)).astype(BF16)
        return (jnp.dot(perm_ref[0], hi, preferred_element_type=F32)
                + jnp.dot(perm_ref[0], lo, preferred_element_type=F32))

    def blocks(v):
        return [v[j * SUBLANES:(j + 1) * SUBLANES] for j in range(n)]

    def shift_subsequences(block, entering):
        return jnp.where(row == 0, entering, pltpu.roll(block, 1, 0))

    z = -L_ref[...]
    softplus = jnp.maximum(z, 0.0) + jnp.log1p(jnp.exp(-jnp.abs(z)))

    def coefficients(sid, rows, before):
        x = xr_ref[sid, rows, :]
        xp = blocks(regroup(x))
        lead = [shift_subsequences(xp[n - k], before[tail - k:tail - k + 1, :])
                for k in range(tail, 0, -1)]
        ext = lead + xp
        xc = cb_ref[...]
        for j in range(CONV_WIDTH):
            xc = xc + jnp.concatenate(ext[j:j + n], axis=0) * cw_ref[j:j + 1, :]

        gates = (jnp.dot(xc.astype(BF16), wgate_ref[...], preferred_element_type=F32)
                 + bgate_ref[...])
        r = _sigmoid(gates[:, :W])
        i = _sigmoid(gates[:, W:])
        log_a = -LRU_C * r * softplus
        a = blocks(jnp.exp(log_a))
        th = jnp.tanh(log_a)
        u = -2.0 * th / (1.0 - th)
        b = blocks(jnp.where(u > 0.0, u * lax.rsqrt(u), 0.0) * (i * xc))
        return x[gs - tail:, :], a, b

    def recur(sid, rows, a, b, h_prev):
        end_a, end_b = a[0], b[0]
        for j in range(1, n):
            end_b = a[j] * end_b + b[j]
            end_a = a[j] * end_a
        d = 1
        while d < SUBLANES:
            keep = row >= d
            a_sh = jnp.where(keep, pltpu.roll(end_a, d, 0), 1.0)
            b_sh = jnp.where(keep, pltpu.roll(end_b, d, 0), 0.0)
            end_b = end_a * b_sh + end_b
            end_a = end_a * a_sh
            d *= 2
        h_end = end_a * h_prev + end_b
        state = shift_subsequences(h_end, h_prev)
        states = []
        for j in range(n):
            state = a[j] * state + b[j]
            states.append(state)
        h = jnp.concatenate(states, axis=0)

        gate = jnp.dot(perm_ref[0], gate_ref[sid, rows, :], preferred_element_type=F32)
        normed = _rms(h * gate, gn_ref[...]).astype(BF16)
        r_ref[sid, rows, :] = jnp.dot(perm_ref[1], normed,
                                      preferred_element_type=F32).astype(r_ref.dtype)
        return h_end[SUBLANES - 1:, :]

    groups = []
    befores = []
    for sid in range(bb):
        before = tail_ref[sid, SUBLANES - tail:, :]
        for r0 in range(0, ts, gs):
            rows = slice(r0, r0 + gs)
            before, a, b = coefficients(sid, rows, before)
            groups.append((sid, rows, a, b))
        befores.append(before)
    states = [hc_ref[sid, 0:1, :] for sid in range(bb)]
    for sid, rows, a, b in groups:
        states[sid] = recur(sid, rows, a, b, states[sid])
    for sid in range(bb):
        tail_ref[sid, SUBLANES - tail:, :] = befores[sid]
        hc_ref[sid] = jnp.broadcast_to(states[sid], hc_ref.shape[1:])

    @pl.when(t == pl.num_programs(1) - 1)
    def _():
        for sid in range(bb):
            hl_ref[sid] = states[sid]
            cn_ref[sid] = befores[sid]


def _lru(xr, xg, h0, conv0, w):
    B, S, W = xr.shape
    ts = min(LRU_TILE, S)
    bb = math.gcd(B, max(1, LRU_TILE // ts))
    assert S % ts == 0 and ts % SUBLANES == 0 and S >= CONV_WIDTH - 1
    tail = CONV_WIDTH - 1
    seq = pl.BlockSpec((bb, ts, W), lambda b, t: (b, t, 0))
    per_b = lambda r: pl.BlockSpec((bb, r, W), lambda b, t: (b, 0, 0))
    const = lambda shape: pl.BlockSpec(shape, lambda b, t: (0, 0))
    gs = min(LRU_GROUP, ts)
    assert ts % gs == 0 and gs % SUBLANES == 0 and gs >= SUBLANES * (CONV_WIDTH - 1)
    n = gs // SUBLANES
    time = np.arange(gs)
    perm = np.zeros((2, gs, gs), np.float32)
    perm[0, SUBLANES * (time % n) + time // n, time] = 1.0
    perm[1] = perm[0].T
    return pl.pallas_call(
        _lru_kernel,
        grid=(B // bb, S // ts),
        in_specs=[seq, seq, per_b(1), per_b(tail),
                  const((CONV_WIDTH, W)), const((1, W)), const((W, 2 * W)), const((1, 2 * W)),
                  const((1, W)), const((1, W)),
                  pl.BlockSpec((2, gs, gs), lambda b, t: (0, 0, 0))],
        out_specs=[seq, per_b(1), per_b(tail)],
        out_shape=[jax.ShapeDtypeStruct((B, S, W), BF16),
                   jax.ShapeDtypeStruct((B, 1, W), F32),
                   jax.ShapeDtypeStruct((B, tail, W), F32)],
        scratch_shapes=[pltpu.VMEM((bb, SUBLANES, W), F32), pltpu.VMEM((bb, SUBLANES, W), F32)],
        compiler_params=pltpu.CompilerParams(dimension_semantics=("parallel", "arbitrary")),
        name="rg_lru",
    )(xr, xg, h0.reshape(B, 1, W), conv0, w["conv_w"], w["conv_b"], w["w_gate"], w["b_gate"],
      w["lru_L"], w["lru_out_norm"], jnp.asarray(perm, BF16))


def _block_diag(wb):
    n, c, d = wb.shape
    eye = jnp.eye(n, dtype=wb.dtype)
    return (eye[:, None, :, None] * wb[:, :, None, :]).reshape(n * c, n * d)


def _prepare_weights(l, p):
    D, d_ff = p["ffn1_gate"].shape[1:]
    assert d_ff % FF_CHUNK == 0
    lw = p["conv_w"].shape[-1]
    in_width = p["w_in"].shape[-1]
    a = (in_width - 2 * lw) // 3
    assert a % V_DIM == 0

    group = np.arange(a) // HEAD_DIM
    w = dict(attn_width=a, lru_width=lw)
    for name in ("ffn1_gate", "ffn1_up", "ffn1_down", "ffn2_gate", "ffn2_up", "ffn2_down"):
        w[name] = p[name][l].astype(BF16)
    for name in ("norm_ffn1", "norm_mix", "norm_ffn2", "lru_out_norm", "conv_b", "lru_L", "subln",
                 "lambda_q1", "lambda_k1", "lambda_q2", "lambda_k2"):
        w[name] = p[name][l][None, :]
    w["w_in"] = p["w_in"][l].astype(BF16)
    w["group_mat"] = jnp.asarray((group[:, None] == group[None, :]) / HEAD_DIM, BF16)
    w["q_gain"] = jnp.tile(p["q_norm"][l], a // HEAD_DIM)[None, :]
    w["k_gain"] = jnp.tile(p["k_norm"][l], a // HEAD_DIM)[None, :]
    w["w_out_attn"] = p["w_out"][l][:a].astype(BF16)
    w["w_out_lru"] = p["w_out"][l][a:].astype(BF16)
    w["conv_w"] = p["conv_w"][l]
    w["w_gate"] = jnp.concatenate([_block_diag(p["gate_a_w"][l]), _block_diag(p["gate_x_w"][l])],
                                  axis=1).astype(BF16)
    w["b_gate"] = jnp.concatenate([p["gate_a_b"][l], p["gate_x_b"][l]])[None, :]
    return w


def _scores_bounded(w, rel_bias):
    score_bound = (HEAD_DIM ** 0.5 * LOG2E * (1 + 2.0 ** -7)
                   * jnp.max(jnp.abs(w["q_gain"])) * jnp.max(jnp.abs(w["k_gain"])))
    bias_bound = 2 * LOG2E * jnp.max(jnp.abs(rel_bias))
    return score_bound + bias_bound <= SCORE_RANGE


def _layer_prompt(x, w, rel_bias, bounded):
    B, S, D = x.shape
    T = ATTN_TILE
    a, lw = w["attn_width"], w["lru_width"]
    H = a // V_DIM
    x1, kT, v, xr, xg, qT, kb, vT = _stage_a(x, w, transposed=True)
    k = kT.reshape(B, H, 2, HEAD_DIM, S).transpose(0, 4, 1, 2, 3)
    nl = S // LAYOUT_TILE
    attend = lambda fixed: functools.partial(_prompt_attention, w=w, fixed_shift=fixed)
    o = lax.cond(bounded, attend(True), attend(False),
                 qT.reshape(B, nl, a, LAYOUT_TILE), kb.reshape(B, S, a),
                 vT.reshape(B, nl, a, LAYOUT_TILE), rel_bias)
    r, h_last, conv_new = _lru(xr.reshape(B, S, lw), xg.reshape(B, S, lw),
                               jnp.zeros((B, lw), F32), jnp.zeros((B, CONV_WIDTH - 1, lw), F32), w)
    y = _stage_d(x1, o.reshape(B * S, a), r.reshape(B * S, lw), w)
    return (y.reshape(B, S, D), k, v.reshape(B, S, H, V_DIM), h_last.reshape(B, lw), conv_new)


def _layer_sample(x, w, rel_bias, bounded, k_past, v_past, h0, conv0):
    B, Q, D = x.shape
    P = k_past.shape[1]
    a, lw = w["attn_width"], w["lru_width"]
    H = a // V_DIM
    near = LANES
    assert P % CHUNK == 0 and Q <= CHUNK and near >= MAX_DISTANCE and P > near
    x1, k, v, xr, xg, qb, kb, vb = _stage_a(x, w, transposed=False)
    q_pos = P + jnp.arange(Q, dtype=jnp.int32)
    idx_c = _bucket_tile(q_pos, jnp.arange(P - near, P, dtype=jnp.int32), False)
    idx_n = _bucket_tile(q_pos, q_pos, False)
    bias_c = _bias_tiles(rel_bias, idx_c[None])[:, 0]
    bias_n = _bias_tiles(rel_bias, idx_n[None])[:, 0]
    attend = lambda free: functools.partial(_sample_attention, w=w, shift_free=free)
    o = lax.cond(bounded, attend(True), attend(False),
                 qb.reshape(B, Q, a), kb.reshape(B, Q, a), vb.reshape(B, Q, a),
                 k_past.transpose(0, 2, 3, 4, 1).reshape(B, a, P),
                 v_past.reshape(B, P * H, V_DIM), bias_c, bias_n)
    r, h_last, conv_new = _lru(xr.reshape(B, Q, lw), xg.reshape(B, Q, lw), h0, conv0, w)
    y = _stage_d(x1, o.reshape(B * Q, a), r.reshape(B * Q, lw), w)
    return (y.reshape(B, Q, D), k.reshape(B, Q, H, 2, LANES)[..., :HEAD_DIM], v.reshape(B, Q, H, V_DIM),
            h_last.reshape(B, lw), conv_new)


def kernel(x_prompt, x_sample, cache_k, cache_v, state_lru, state_conv, rel_bias, norm_ffn1, ffn1_gate, ffn1_up, ffn1_down, norm_mix, w_in, q_norm, k_norm, lambda_q1, lambda_k1, lambda_q2, lambda_k2, subln, conv_w, conv_b, gate_a_w, gate_a_b, gate_x_w, gate_x_b, lru_L, lru_out_norm, w_out, norm_ffn2, ffn2_gate, ffn2_up, ffn2_down):
    p = dict(norm_ffn1=norm_ffn1, ffn1_gate=ffn1_gate, ffn1_up=ffn1_up, ffn1_down=ffn1_down,
             norm_mix=norm_mix, w_in=w_in, q_norm=q_norm, k_norm=k_norm,
             lambda_q1=lambda_q1, lambda_k1=lambda_k1, lambda_q2=lambda_q2, lambda_k2=lambda_k2,
             subln=subln, conv_w=conv_w, conv_b=conv_b, gate_a_w=gate_a_w, gate_a_b=gate_a_b,
             gate_x_w=gate_x_w, gate_x_b=gate_x_b, lru_L=lru_L, lru_out_norm=lru_out_norm,
             w_out=w_out, norm_ffn2=norm_ffn2, ffn2_gate=ffn2_gate, ffn2_up=ffn2_up,
             ffn2_down=ffn2_down)
    depth = cache_k.shape[0]
    assert depth == 1, "the lambda initial value is specialised to a single layer"
    w = _prepare_weights(0, p)
    bounded = _scores_bounded(w, rel_bias)
    yp, kp, vp, hp, cp = _layer_prompt(x_prompt, w, rel_bias, bounded)
    ys, kn, vn, hn, cn = _layer_sample(x_sample, w, rel_bias, bounded, cache_k[0], cache_v[0],
                                       state_lru[0], state_conv[0])
    stack = lambda t: t[None]
    return (yp, ys, stack(kp), stack(vp), stack(hp), stack(cp),
            stack(kn), stack(vn), stack(hn), stack(cn))
```

```python
import functools
import itertools
import math

import numpy as np
import jax
import jax.numpy as jnp
from jax import lax
from jax.experimental import pallas as pl
from jax.experimental.pallas import tpu as pltpu

F32 = jnp.float32
BF16 = jnp.bfloat16

HEAD_DIM = 64
V_DIM = 2 * HEAD_DIM
CHUNK = 64
NUM_BUCKETS = 32
MAX_DISTANCE = 128
CONV_WIDTH = 4
LRU_C = 8.0
EPS = 1e-6
NEG_INF = -1e30
LAM_INIT = 0.8 - 0.6 * math.exp(-0.3 * 0)

LANES = 128
SUBLANES = 8
MXU_DIM = 256
MIB = 1024 * 1024
VMEM_BYTES = 64 * MIB
VMEM_RESERVED_BYTES = 4 * MIB
VMEM_DEFAULT_LIMIT_BYTES = 16 * MIB

FF_CHUNK = MXU_DIM
ATTN_TILE = 512
LAYOUT_TILE = 256
BIAS_BLOCK = LANES
FAR_ROWS = 1024
FAR_TILES = 8
LOG2E = 1.4426950408889634
SCORE_RANGE = 60.0
TOKEN_TILE = 512
STAGE_D_TILE = 1024
MIN_STEPS = 4
SAMPLE_STREAMS = 2
LRU_TILE = 1024
LRU_GROUP = 256


def _vmem_limit(nbytes):
    return int(min(VMEM_BYTES - VMEM_RESERVED_BYTES, max(nbytes, VMEM_DEFAULT_LIMIT_BYTES)))


def _rms(x, g):
    ms = jnp.mean(x * x, axis=-1, keepdims=True)
    return x * lax.rsqrt(ms + EPS) * g


def _sigmoid(x):
    return 0.5 * jnp.tanh(0.5 * x) + 0.5


def _gelu_tanh(x):
    return 0.5 * x * (1.0 + jnp.tanh(math.sqrt(2.0 / math.pi) * (x + 0.044715 * (x * x * x))))


def _const_spec(shape):
    n = len(shape)
    return pl.BlockSpec(shape, lambda *_: (0,) * n, pipeline_mode=pl.Buffered(1))


def _swiglu_into(acc_ref, xn, wg_ref, wu_ref, wd_ref):
    d_ff = wg_ref.shape[1]
    for c in range(0, d_ff, FF_CHUNK):
        cols = slice(c, c + FF_CHUNK)
        g = jnp.dot(xn, wg_ref[:, cols], preferred_element_type=F32)
        u = jnp.dot(xn, wu_ref[:, cols], preferred_element_type=F32)
        h = (g * _sigmoid(g) * u).astype(BF16)
        part = jnp.dot(h, wd_ref[cols, :], preferred_element_type=F32)
        if c == 0:
            acc_ref[...] = part
        else:
            acc_ref[...] += part


def _stage_a_kernel(x_ref, g1_ref, wg_ref, wu_ref, wd_ref, gm_ref, win_ref, gmat_ref, qg_ref, kg_ref,
                    x1_ref, k_ref, v_ref, xr_ref, xg_ref, qb_ref, kb_ref, vb_ref, acc_ref,
                    *, attn_width, lru_width, transposed, tile, q_scale):
    x = x_ref[...]
    xn = _rms(x, g1_ref[...]).astype(BF16)
    _swiglu_into(acc_ref, xn, wg_ref, wu_ref, wd_ref)
    x1 = x + 0.5 * acc_ref[...]
    x1_ref[...] = x1
    hn = _rms(x1, gm_ref[...]).astype(BF16)
    proj = jnp.dot(hn, win_ref[...], preferred_element_type=F32)
    a = attn_width
    q = proj[:, :a]
    k = proj[:, a:2 * a]
    v = proj[:, 2 * a:3 * a]
    xr_ref[...] = proj[:, 3 * a:3 * a + lru_width]
    xg_ref[...] = _gelu_tanh(proj[:, 3 * a + lru_width:]).astype(BF16)

    def group_norm(t, g):
        ms = jnp.dot((t * t).astype(BF16), gmat_ref[...], preferred_element_type=F32)
        return t * lax.rsqrt(ms + EPS) * g

    qn = group_norm(q, qg_ref[...]) * q_scale
    kn = group_norm(k, kg_ref[...])
    n_heads = a // V_DIM
    for h in range(n_heads):
        v_ref[pl.ds(h, x.shape[0], stride=n_heads), :] = v[:, h * V_DIM:(h + 1) * V_DIM]
    kb_ref[...] = kn.astype(BF16)
    if transposed:
        k_ref[0] = kn.T
        for j in range(x.shape[0] // tile):
            rows = slice(j * tile, (j + 1) * tile)
            qb_ref[0, j] = qn[rows, :].T.astype(BF16)
            vb_ref[0, j] = v[rows, :].T.astype(BF16)
    else:
        n_vec = a // HEAD_DIM
        pad = jnp.zeros((x.shape[0], LANES - HEAD_DIM), F32)
        for g in range(n_vec):
            vec = jnp.concatenate([kn[:, g * HEAD_DIM:(g + 1) * HEAD_DIM], pad], axis=1)
            k_ref[pl.ds(g, x.shape[0], stride=n_vec), :] = vec
        qb_ref[...] = qn.astype(BF16)
        vb_ref[...] = v.astype(BF16)


def _stage_a(x, w, *, transposed):
    B, S, D = x.shape
    a = w["attn_width"]
    lw = w["lru_width"]
    tm = TOKEN_TILE
    T = LAYOUT_TILE
    q_scale = HEAD_DIM ** -0.5 * (LOG2E if transposed else 1.0)
    assert (B * S) % tm == 0 and (not transposed or (tm % T == 0 and S % tm == 0))
    nt = S // tm
    x2 = x.reshape(B * S, D)
    d_ff = w["ffn1_gate"].shape[1]
    tok = lambda width: pl.BlockSpec((tm, width), lambda i: (i, 0))
    out_shape = [
        jax.ShapeDtypeStruct((B * S, D), F32),
        jax.ShapeDtypeStruct((B * S, a), F32),
        jax.ShapeDtypeStruct((B * S, a), F32),
        jax.ShapeDtypeStruct((B * S, lw), F32),
        jax.ShapeDtypeStruct((B * S, lw), BF16),
    ]
    out_specs = [tok(D), tok(a), tok(a), tok(lw), tok(lw)]
    H = a // V_DIM
    out_shape[2] = jax.ShapeDtypeStruct((B * S * H, V_DIM), F32)
    out_specs[2] = pl.BlockSpec((tm * H, V_DIM), lambda i: (i, 0))
    if transposed:
        out_shape[1] = jax.ShapeDtypeStruct((B, a, S), F32)
        out_specs[1] = pl.BlockSpec((1, a, tm), lambda i: (i // nt, 0, i % nt))
        tshape = jax.ShapeDtypeStruct((B * nt, tm // T, a, T), BF16)
        tspec = pl.BlockSpec((1, tm // T, a, T), lambda i: (i, 0, 0, 0))
        out_shape += [tshape, jax.ShapeDtypeStruct((B * S, a), BF16), tshape]
        out_specs += [tspec, tok(a), tspec]
    else:
        n_vec = a // HEAD_DIM
        out_shape[1] = jax.ShapeDtypeStruct((B * S * n_vec, LANES), F32)
        out_specs[1] = pl.BlockSpec((tm * n_vec, LANES), lambda i: (i, 0))
        bshape = jax.ShapeDtypeStruct((B * S, a), BF16)
        out_shape += [bshape, bshape, bshape]
        out_specs += [tok(a), tok(a), tok(a)]
    weight_bytes = 2 * (3 * D * d_ff + D * (3 * a + 2 * lw) + a * a)
    act_bytes = 4 * tm * (2 * 2 * D + 2 * 2 * (2 * a + 2 * lw) + D + 2 * (3 * a + 2 * lw) + 4 * FF_CHUNK)
    kern = functools.partial(_stage_a_kernel, attn_width=a, lru_width=lw, transposed=transposed,
                             tile=T, q_scale=q_scale)
    outs = pl.pallas_call(
        kern,
        grid=(B * S // tm,),
        in_specs=[
            tok(D),
            _const_spec((1, D)),
            _const_spec(w["ffn1_gate"].shape),
            _const_spec(w["ffn1_up"].shape),
            _const_spec(w["ffn1_down"].shape),
            _const_spec((1, D)),
            _const_spec(w["w_in"].shape),
            _const_spec((a, a)),
            _const_spec((1, a)),
            _const_spec((1, a)),
        ],
        out_specs=out_specs,
        out_shape=out_shape,
        scratch_shapes=[pltpu.VMEM((tm, D), F32)],
        compiler_params=pltpu.CompilerParams(
            dimension_semantics=("parallel",),
            vmem_limit_bytes=_vmem_limit(weight_bytes + act_bytes)),
        name="stage_a_t" if transposed else "stage_a_n",
    )(x2, w["norm_ffn1"], w["ffn1_gate"], w["ffn1_up"], w["ffn1_down"], w["norm_mix"], w["w_in"],
      w["group_mat"], w["q_gain"], w["k_gain"])
    return outs


def _stage_d_kernel(x1_ref, o_ref, r_ref, woa_ref, wor_ref, g2_ref, wg_ref, wu_ref, wd_ref,
                    y_ref, acc_ref):
    x2 = (x1_ref[...]
          + jnp.dot(o_ref[...], woa_ref[...], preferred_element_type=F32)
          + jnp.dot(r_ref[...], wor_ref[...], preferred_element_type=F32))
    xn = _rms(x2, g2_ref[...]).astype(BF16)
    _swiglu_into(acc_ref, xn, wg_ref, wu_ref, wd_ref)
    y_ref[...] = x2 + 0.5 * acc_ref[...]


def _stage_d(x1, o, r, w):
    N, D = x1.shape
    a = o.shape[1]
    lw = r.shape[1]
    tm = STAGE_D_TILE if N >= MIN_STEPS * STAGE_D_TILE else min(TOKEN_TILE, N)
    assert N % tm == 0
    d_ff = w["ffn2_gate"].shape[1]
    tok = lambda width: pl.BlockSpec((tm, width), lambda i: (i, 0))
    weight_bytes = 2 * (3 * D * d_ff + D * D)
    act_bytes = 4 * tm * (2 * 2 * D + 2 * (a + lw) + 2 * D + 4 * FF_CHUNK)
    return pl.pallas_call(
        _stage_d_kernel,
        grid=(N // tm,),
        in_specs=[
            tok(D), tok(a), tok(lw),
            _const_spec((a, D)), _const_spec((lw, D)), _const_spec((1, D)),
            _const_spec(w["ffn2_gate"].shape), _const_spec(w["ffn2_up"].shape),
            _const_spec(w["ffn2_down"].shape),
        ],
        out_specs=tok(D),
        out_shape=jax.ShapeDtypeStruct((N, D), F32),
        scratch_shapes=[pltpu.VMEM((tm, D), F32)],
        compiler_params=pltpu.CompilerParams(
            dimension_semantics=("parallel",),
            vmem_limit_bytes=_vmem_limit(weight_bytes + act_bytes)),
        name="stage_d",
    )(x1, o, r, w["w_out_attn"], w["w_out_lru"], w["norm_ffn2"],
      w["ffn2_gate"], w["ffn2_up"], w["ffn2_down"])


def _t5_bucket(rel):
    n = NUM_BUCKETS // 2
    max_exact = n // 2
    ret = jnp.where(rel > 0, n, 0)
    rel = jnp.abs(rel)
    relf = jnp.maximum(rel, 1).astype(jnp.float32)
    large = max_exact + (jnp.log(relf / max_exact) / math.log(MAX_DISTANCE / max_exact)
                         * (n - max_exact)).astype(jnp.int32)
    large = jnp.minimum(large, n - 1)
    return ret + jnp.where(rel < max_exact, rel, large)


FAR_BUCKET = NUM_BUCKETS // 2 - 1


def _bucket_tile(q_pos, k_pos, keys_first):
    rel = k_pos[None, :] - q_pos[:, None]
    visible = (k_pos[None, :] // CHUNK) <= (q_pos[:, None] // CHUNK)
    idx = jnp.where(visible, _t5_bucket(rel), -1)
    return idx.T if keys_first else idx


def _bias_kernel(table_ref, idx_ref, out_ref, *, scale):
    h = pl.program_id(0)
    idx = idx_ref[0]
    acc = jnp.zeros(idx.shape, F32)
    for b in range(NUM_BUCKETS):
        acc = jnp.where(idx == b, table_ref[b, h], acc)
    out_ref[0, 0] = jnp.where(idx < 0, NEG_INF, (acc - table_ref[FAR_BUCKET, h]) * scale)


def _bias_tiles(rel_bias, idx, scale=1.0):
    n, R, C = idx.shape
    H = rel_bias.shape[1]
    return pl.pallas_call(
        functools.partial(_bias_kernel, scale=scale),
        grid=(H, n),
        in_specs=[pl.BlockSpec(memory_space=pltpu.SMEM),
                  pl.BlockSpec((1, R, C), lambda h, i: (i, 0, 0))],
        out_specs=pl.BlockSpec((1, 1, R, C), lambda h, i: (h, i, 0, 0)),
        out_shape=jax.ShapeDtypeStruct((H, n, R, C), F32),
        name="bias_tiles",
    )(rel_bias, idx)


def _lambda(lq1_ref, lk1_ref, lq2_ref, lk2_ref):
    s1 = jnp.sum(lq1_ref[...] * lk1_ref[...], axis=-1, keepdims=True)
    s2 = jnp.sum(lq2_ref[...] * lk2_ref[...], axis=-1, keepdims=True)
    return jnp.exp(s1) - jnp.exp(s2) + LAM_INIT


def _split_components(qt):
    first = lax.broadcasted_iota(jnp.int32, qt.shape, 0) < HEAD_DIM
    zero = jnp.zeros_like(qt)
    return jnp.concatenate([jnp.where(first, qt, zero), jnp.where(first, zero, qt)], axis=1)


def _attn_finish(o, sub_ref):
    o = o * lax.rsqrt(jnp.mean(o * o, axis=0, keepdims=True) + EPS)
    return o.T * sub_ref[...] * (1.0 - LAM_INIT)


def _prompt_attn_fixed_kernel(qT_ref, k_ref, vT_ref, bias_ref, lq1_ref, lk1_ref, lq2_ref, lk2_ref,
                              sub_ref, o_ref, l_ref, acc_ref, *, tile, far_rows):
    G = qT_ref.shape[3]
    T = tile
    Bk = bias_ref.shape[2]
    nb = T // Bk
    sub = T // G
    nq = qT_ref.shape[1] // sub
    lam = _lambda(lq1_ref, lk1_ref, lq2_ref, lk2_ref)

    def lane_tiles(ref, first, count):
        return jnp.concatenate([ref[0, first + j] for j in range(count)], axis=1)

    def scores(first_group, count, q2):
        rows = pl.ds(pl.multiple_of(first_group * G, G), count * G)
        return jnp.dot(k_ref[0, rows, :], q2, preferred_element_type=F32)

    def weigh(first_group, count, s):
        p = jnp.exp2(s)
        return (jnp.sum(p, axis=0, keepdims=True),
                jnp.dot(lane_tiles(vT_ref, first_group, count), p.astype(BF16),
                        preferred_element_type=F32))

    def far_groups(first_group, count, q2):
        per = min(far_rows // G, count)
        firsts = [first_group + j for j in range(0, count, per)]
        parts = [weigh(g, per, scores(g, per, q2)) for g in firsts]
        l_ref[...] += functools.reduce(lambda x, y: x + y, [p[0] for p in parts])
        acc_ref[...] += functools.reduce(lambda x, y: x + y, [p[1] for p in parts])

    def q_tile(qi, carry):
        q2 = _split_components(lane_tiles(qT_ref, qi * sub, sub))

        def near(start, count, lead):
            s = scores(start * sub, count * sub, q2)
            rows = []
            for kb in range(count * nb):
                pieces = []
                for comp in range(2):
                    for rb in range(nb):
                        lanes = slice(comp * T + rb * Bk, comp * T + (rb + 1) * Bk)
                        piece = s[kb * Bk:(kb + 1) * Bk, lanes]
                        d = kb - lead - rb
                        if d > 0:
                            piece = jnp.full_like(piece, NEG_INF)
                        elif d >= -1:
                            piece = piece + bias_ref[0, -d]
                        pieces.append(piece)
                rows.append(jnp.concatenate(pieces, axis=1))
            l_near, acc_near = weigh(start * sub, count * sub, jnp.concatenate(rows, axis=0))
            l_ref[...] = l_near
            acc_ref[...] = acc_near

        @pl.when(qi == 0)
        def _():
            near(0, 1, 0)

        @pl.when(qi > 0)
        def _():
            near(qi - 1, 2, nb)

        n_far = jnp.maximum(qi - 1, 0)

        def far(i, c):
            far_groups(i * FAR_TILES * sub, FAR_TILES * sub, q2)
            return c

        lax.fori_loop(0, n_far // FAR_TILES, far, 0)
        done = n_far - n_far % FAR_TILES
        chunk = FAR_TILES // 2
        while chunk >= 1:
            take = (n_far - done) >= chunk

            @pl.when(take)
            def _(done=done, chunk=chunk):
                far_groups(done * sub, chunk * sub, q2)

            done = done + jnp.where(take, chunk, 0)
            chunk //= 2

        on = acc_ref[...] * (1.0 / l_ref[...])
        o = on[:, :T] - lam * on[:, T:]
        rows = pl.ds(pl.multiple_of(qi * T, T), T)
        o_ref[0, rows, :] = _attn_finish(o, sub_ref).astype(o_ref.dtype)
        return carry

    lax.fori_loop(0, nq, q_tile, 0)


def _prompt_attn_online_kernel(qT_ref, k_ref, vT_ref, bias_ref, lq1_ref, lk1_ref, lq2_ref, lk2_ref,
                               sub_ref, o_ref, m_ref, l_ref, acc_ref, *, tile):
    T = tile
    sub = T // qT_ref.shape[3]
    nq = qT_ref.shape[1] // sub
    lam = _lambda(lq1_ref, lk1_ref, lq2_ref, lk2_ref)

    def lane_tiles(ref, i):
        return jnp.concatenate([ref[0, i * sub + j] for j in range(sub)], axis=1)

    def scores(ki, q2):
        kt = k_ref[0, pl.ds(pl.multiple_of(ki * T, T), T), :]
        return jnp.dot(kt, q2, preferred_element_type=F32)

    def pv(p, ki):
        return jnp.dot(lane_tiles(vT_ref, ki), p.astype(BF16), preferred_element_type=F32)

    def update(s, ki):
        m_prev = m_ref[...]
        m_new = jnp.maximum(m_prev, jnp.max(s, axis=0, keepdims=True))
        alpha = jnp.exp2(m_prev - m_new)
        p = jnp.exp2(s - m_new)
        l_ref[...] = alpha * l_ref[...] + jnp.sum(p, axis=0, keepdims=True)
        acc_ref[...] = alpha * acc_ref[...] + pv(p, ki)
        m_ref[...] = m_new

    def q_tile(qi, carry):
        q2 = _split_components(lane_tiles(qT_ref, qi))

        bias = bias_ref[0, 0]
        s = scores(qi, q2) + jnp.concatenate([bias, bias], axis=1)
        m = jnp.max(s, axis=0, keepdims=True)
        p = jnp.exp2(s - m)
        m_ref[...] = m
        l_ref[...] = jnp.sum(p, axis=0, keepdims=True)
        acc_ref[...] = pv(p, qi)
        kp = jnp.maximum(qi - 1, 0)
        bias = jnp.where(qi > 0, bias_ref[0, 1], NEG_INF)
        s = scores(kp, q2) + jnp.concatenate([bias, bias], axis=1)
        n_far = qi - 1

        update(s, kp)

        def far(ki, s_cur):
            s_next = scores(jnp.minimum(ki + 1, jnp.maximum(n_far - 1, 0)), q2)
            update(s_cur, ki)
            return s_next

        lax.fori_loop(0, n_far, far, scores(0, q2))

        on = acc_ref[...] * (1.0 / l_ref[...])
        o = on[:, :T] - lam * on[:, T:]
        rows = pl.ds(pl.multiple_of(qi * T, T), T)
        o_ref[0, rows, :] = _attn_finish(o, sub_ref).astype(o_ref.dtype)
        return carry

    lax.fori_loop(0, nq, q_tile, 0)


def _near_bias(rel_bias, tile, offsets):
    pos = jnp.arange(tile, dtype=jnp.int32)
    base = tile * max(offsets)
    idx = [_bucket_tile(pos + base, pos + base - o * tile, True) for o in offsets]
    return _bias_tiles(rel_bias, jnp.stack(idx), LOG2E)


def _prompt_attention(qT, kb, vT, rel_bias, w, *, fixed_shift):
    T = ATTN_TILE
    B, S = kb.shape[:2]
    G = qT.shape[-1]
    A = kb.shape[-1]
    H = A // V_DIM
    assert S % T == 0 and G % CHUNK == 0 and G >= MAX_DISTANCE and T == 2 * G
    lam_spec = pl.BlockSpec((1, HEAD_DIM), lambda b, h: (0, 0))
    tile_spec = pl.BlockSpec((1, S // G, V_DIM, G), lambda b, h: (b, 0, h, 0))
    seq_spec = pl.BlockSpec((1, S, V_DIM), lambda b, h: (b, 0, h))
    tmp_bytes = 4 * 8 * T * 2 * T
    if fixed_shift:
        assert BIAS_BLOCK % CHUNK == 0 and BIAS_BLOCK >= MAX_DISTANCE and T % BIAS_BLOCK == 0
        bias = _near_bias(rel_bias, BIAS_BLOCK, (0, 1))
        kern = functools.partial(_prompt_attn_fixed_kernel, tile=T, far_rows=FAR_ROWS)
        stats = []
    else:
        bias = _near_bias(rel_bias, T, (0, 1))
        kern = functools.partial(_prompt_attn_online_kernel, tile=T)
        stats = [pltpu.VMEM((1, 2 * T), F32)]
    blk_bytes = 2 * 2 * (3 * S * V_DIM) + 2 * 2 * S * V_DIM + 2 * 4 * bias[0].size
    return pl.pallas_call(
        kern,
        grid=(B, H),
        in_specs=[tile_spec, seq_spec, tile_spec,
                  pl.BlockSpec((1,) + bias.shape[1:], lambda b, h: (h, 0, 0, 0)),
                  lam_spec, lam_spec, lam_spec, lam_spec,
                  pl.BlockSpec((1, V_DIM), lambda b, h: (0, 0))],
        out_specs=seq_spec,
        out_shape=jax.ShapeDtypeStruct((B, S, A), BF16),
        scratch_shapes=stats + [pltpu.VMEM((1, 2 * T), F32), pltpu.VMEM((V_DIM, 2 * T), F32)],
        compiler_params=pltpu.CompilerParams(
            dimension_semantics=("parallel", "parallel"),
            vmem_limit_bytes=_vmem_limit(blk_bytes + tmp_bytes + 4 * V_DIM * 2 * T)),
        name="prompt_attn_fixed" if fixed_shift else "prompt_attn_online",
    )(qT, kb, vT, bias, w["lambda_q1"], w["lambda_k1"], w["lambda_q2"], w["lambda_k2"], w["subln"])


def _sample_attn_kernel(q_ref, kn_ref, vn_ref, ck_ref, cv_ref, bc_ref, bn_ref,
                        lq1_ref, lk1_ref, lq2_ref, lk2_ref, sub_ref, o_ref, *, near, shift_free):
    Q = q_ref.shape[1]
    P = ck_ref.shape[2]
    H = q_ref.shape[2] // V_DIM
    lam = _lambda(lq1_ref, lk1_ref, lq2_ref, lk2_ref)
    first = lax.broadcasted_iota(jnp.int32, (Q, V_DIM), 1) < HEAD_DIM
    nt = (((1,), (1,)), ((), ()))
    far = P - near
    for sb, h in itertools.product(range(q_ref.shape[0]), range(H)):
        cols = slice(h * V_DIM, (h + 1) * V_DIM)
        qh = q_ref[sb, :, cols]
        zero = jnp.zeros_like(qh)
        q2 = jnp.concatenate([jnp.where(first, qh, zero), jnp.where(first, zero, qh)], axis=0)
        k_far = ck_ref[sb, cols, :far].astype(BF16)
        k_near = ck_ref[sb, cols, far:].astype(BF16)
        bc = bc_ref[h]
        bn = bn_ref[h]
        s_far = jnp.dot(q2, k_far, preferred_element_type=F32)
        s_near = (jnp.dot(q2, k_near, preferred_element_type=F32)
                  + jnp.concatenate([bc, bc], axis=0))
        s_new = (lax.dot_general(q2, kn_ref[sb, :, cols], nt, preferred_element_type=F32)
                 + jnp.concatenate([bn, bn], axis=0))
        if not shift_free:
            m = jnp.maximum(jnp.maximum(jnp.max(s_far, axis=-1, keepdims=True),
                                        jnp.max(s_near, axis=-1, keepdims=True)),
                            jnp.max(s_new, axis=-1, keepdims=True))
            s_far, s_near, s_new = s_far - m, s_near - m, s_new - m
        p_far = jnp.exp(s_far)
        p_near = jnp.exp(s_near)
        p_new = jnp.exp(s_new)
        l = (jnp.sum(p_far, axis=-1, keepdims=True) + jnp.sum(p_near, axis=-1, keepdims=True)
             + jnp.sum(p_new, axis=-1, keepdims=True))
        v_far = cv_ref[sb, pl.ds(h, far, stride=H), :].astype(BF16)
        v_near = cv_ref[sb, pl.ds(far * H + h, near, stride=H), :].astype(BF16)
        acc = (jnp.dot(p_far.astype(BF16), v_far, preferred_element_type=F32)
               + jnp.dot(p_near.astype(BF16), v_near, preferred_element_type=F32)
               + jnp.dot(p_new.astype(BF16), vn_ref[sb, :, cols], preferred_element_type=F32))
        on = acc * (1.0 / l)
        o = on[:Q] - lam * on[Q:]
        o = o * lax.rsqrt(jnp.mean(o * o, axis=-1, keepdims=True) + EPS)
        o_ref[sb, :, cols] = (o * sub_ref[...] * (1.0 - LAM_INIT)).astype(o_ref.dtype)


def _sample_attention(qb, kb, vb, cache_k, cache_v, bias_c, bias_n, w, *, shift_free):
    B, Q, A = qb.shape
    P = cache_k.shape[2]
    near = bias_c.shape[-1]
    sb = math.gcd(B, SAMPLE_STREAMS)
    new_spec = pl.BlockSpec((sb, Q, A), lambda b: (b, 0, 0))
    ck_spec = pl.BlockSpec((sb, A, P), lambda b: (b, 0, 0))
    cv_spec = pl.BlockSpec((sb,) + cache_v.shape[1:], lambda b: (b, 0, 0))
    lam_spec = pl.BlockSpec((1, HEAD_DIM), lambda b: (0, 0))
    blk_bytes = sb * (2 * 2 * 4 * P * A + 2 * 4 * 2 * Q * A)
    tmp_bytes = 4 * 2 * Q * (3 * P + 2 * P) + 2 * 2 * P * V_DIM
    return pl.pallas_call(
        functools.partial(_sample_attn_kernel, near=near, shift_free=shift_free),
        grid=(B // sb,),
        in_specs=[new_spec, new_spec, new_spec, ck_spec, cv_spec,
                  pl.BlockSpec(bias_c.shape, lambda b: (0, 0, 0)),
                  pl.BlockSpec(bias_n.shape, lambda b: (0, 0, 0)),
                  lam_spec, lam_spec, lam_spec, lam_spec,
                  pl.BlockSpec((1, V_DIM), lambda b: (0, 0))],
        out_specs=new_spec,
        out_shape=jax.ShapeDtypeStruct((B, Q, A), BF16),
        compiler_params=pltpu.CompilerParams(
            dimension_semantics=("parallel",),
            vmem_limit_bytes=_vmem_limit(blk_bytes + tmp_bytes)),
        name="sample_attn_free" if shift_free else "sample_attn_rowmax",
    )(qb, kb, vb, cache_k, cache_v, bias_c, bias_n,
      w["lambda_q1"], w["lambda_k1"], w["lambda_q2"], w["lambda_k2"], w["subln"])


def _lru_kernel(xr_ref, gate_ref, h0_ref, c0_ref, cw_ref, cb_ref, wgate_ref, bgate_ref, L_ref, gn_ref,
                perm_ref, r_ref, hl_ref, cn_ref, tail_ref, hc_ref):
    t = pl.program_id(1)
    bb, ts, W = xr_ref.shape
    gs = perm_ref.shape[1]
    n = gs // SUBLANES
    tail = CONV_WIDTH - 1
    row = lax.broadcasted_iota(jnp.int32, (SUBLANES, W), 0)

    @pl.when(t == 0)
    def _():
        tail_ref[:, SUBLANES - tail:, :] = c0_ref[...]
        hc_ref[...] = jnp.broadcast_to(h0_ref[...], hc_ref.shape)

    def regroup(x):
        hi = x.astype(BF16)
        lo = (x - hi.astype(F32)).astype(BF16)
        return (jnp.dot(perm_ref[0], hi, preferred_element_type=F32)
                + jnp.dot(perm_ref[0], lo, preferred_element_type=F32))

    def blocks(v):
        return [v[j * SUBLANES:(j + 1) * SUBLANES] for j in range(n)]

    def shift_subsequences(block, entering):
        return jnp.where(row == 0, entering, pltpu.roll(block, 1, 0))

    z = -L_ref[...]
    softplus = jnp.maximum(z, 0.0) + jnp.log1p(jnp.exp(-jnp.abs(z)))

    def coefficients(sid, rows, before):
        x = xr_ref[sid, rows, :]
        xp = blocks(regroup(x))
        lead = [shift_subsequences(xp[n - k], before[tail - k:tail - k + 1, :])
                for k in range(tail, 0, -1)]
        ext = lead + xp
        xc = cb_ref[...]
        for j in range(CONV_WIDTH):
            xc = xc + jnp.concatenate(ext[j:j + n], axis=0) * cw_ref[j:j + 1, :]

        gates = (jnp.dot(xc.astype(BF16), wgate_ref[...], preferred_element_type=F32)
                 + bgate_ref[...])
        r = _sigmoid(gates[:, :W])
        i = _sigmoid(gates[:, W:])
        log_a = -LRU_C * r * softplus
        a = blocks(jnp.exp(log_a))
        th = jnp.tanh(log_a)
        u = -2.0 * th / (1.0 - th)
        b = blocks(jnp.where(u > 0.0, u * lax.rsqrt(u), 0.0) * (i * xc))
        return x[gs - tail:, :], a, b

    def recur(sid, rows, a, b, h_prev):
        end_a, end_b = a[0], b[0]
        for j in range(1, n):
            end_b = a[j] * end_b + b[j]
            end_a = a[j] * end_a
        d = 1
        while d < SUBLANES:
            keep = row >= d
            a_sh = jnp.where(keep, pltpu.roll(end_a, d, 0), 1.0)
            b_sh = jnp.where(keep, pltpu.roll(end_b, d, 0), 0.0)
            end_b = end_a * b_sh + end_b
            end_a = end_a * a_sh
            d *= 2
        h_end = end_a * h_prev + end_b
        state = shift_subsequences(h_end, h_prev)
        states = []
        for j in range(n):
            state = a[j] * state + b[j]
            states.append(state)
        h = jnp.concatenate(states, axis=0)

        gate = jnp.dot(perm_ref[0], gate_ref[sid, rows, :], preferred_element_type=F32)
        normed = _rms(h * gate, gn_ref[...]).astype(BF16)
        r_ref[sid, rows, :] = jnp.dot(perm_ref[1], normed,
                                      preferred_element_type=F32).astype(r_ref.dtype)
        return h_end[SUBLANES - 1:, :]

    groups = []
    befores = []
    for sid in range(bb):
        before = tail_ref[sid, SUBLANES - tail:, :]
        for r0 in range(0, ts, gs):
            rows = slice(r0, r0 + gs)
            before, a, b = coefficients(sid, rows, before)
            groups.append((sid, rows, a, b))
        befores.append(before)
    states = [hc_ref[sid, 0:1, :] for sid in range(bb)]
    for sid, rows, a, b in groups:
        states[sid] = recur(sid, rows, a, b, states[sid])
    for sid in range(bb):
        tail_ref[sid, SUBLANES - tail:, :] = befores[sid]
        hc_ref[sid] = jnp.broadcast_to(states[sid], hc_ref.shape[1:])

    @pl.when(t == pl.num_programs(1) - 1)
    def _():
        for sid in range(bb):
            hl_ref[sid] = states[sid]
            cn_ref[sid] = befores[sid]


def _lru(xr, xg, h0, conv0, w):
    B, S, W = xr.shape
    ts = min(LRU_TILE, S)
    bb = math.gcd(B, max(1, LRU_TILE // ts))
    assert S % ts == 0 and ts % SUBLANES == 0 and S >= CONV_WIDTH - 1
    tail = CONV_WIDTH - 1
    seq = pl.BlockSpec((bb, ts, W), lambda b, t: (b, t, 0))
    per_b = lambda r: pl.BlockSpec((bb, r, W), lambda b, t: (b, 0, 0))
    const = lambda shape: pl.BlockSpec(shape, lambda b, t: (0, 0))
    gs = min(LRU_GROUP, ts)
    assert ts % gs == 0 and gs % SUBLANES == 0 and gs >= SUBLANES * (CONV_WIDTH - 1)
    n = gs // SUBLANES
    time = np.arange(gs)
    perm = np.zeros((2, gs, gs), np.float32)
    perm[0, SUBLANES * (time % n) + time // n, time] = 1.0
    perm[1] = perm[0].T
    return pl.pallas_call(
        _lru_kernel,
        grid=(B // bb, S // ts),
        in_specs=[seq, seq, per_b(1), per_b(tail),
                  const((CONV_WIDTH, W)), const((1, W)), const((W, 2 * W)), const((1, 2 * W)),
                  const((1, W)), const((1, W)),
                  pl.BlockSpec((2, gs, gs), lambda b, t: (0, 0, 0))],
        out_specs=[seq, per_b(1), per_b(tail)],
        out_shape=[jax.ShapeDtypeStruct((B, S, W), BF16),
                   jax.ShapeDtypeStruct((B, 1, W), F32),
                   jax.ShapeDtypeStruct((B, tail, W), F32)],
        scratch_shapes=[pltpu.VMEM((bb, SUBLANES, W), F32), pltpu.VMEM((bb, SUBLANES, W), F32)],
        compiler_params=pltpu.CompilerParams(dimension_semantics=("parallel", "arbitrary")),
        name="rg_lru",
    )(xr, xg, h0.reshape(B, 1, W), conv0, w["conv_w"], w["conv_b"], w["w_gate"], w["b_gate"],
      w["lru_L"], w["lru_out_norm"], jnp.asarray(perm, BF16))


def _block_diag(wb):
    n, c, d = wb.shape
    eye = jnp.eye(n, dtype=wb.dtype)
    return (eye[:, None, :, None] * wb[:, :, None, :]).reshape(n * c, n * d)


def _prepare_weights(l, p):
    D, d_ff = p["ffn1_gate"].shape[1:]
    assert d_ff % FF_CHUNK == 0
    lw = p["conv_w"].shape[-1]
    in_width = p["w_in"].shape[-1]
    a = (in_width - 2 * lw) // 3
    assert a % V_DIM == 0

    group = np.arange(a) // HEAD_DIM
    w = dict(attn_width=a, lru_width=lw)
    for name in ("ffn1_gate", "ffn1_up", "ffn1_down", "ffn2_gate", "ffn2_up", "ffn2_down"):
        w[name] = p[name][l].astype(BF16)
    for name in ("norm_ffn1", "norm_mix", "norm_ffn2", "lru_out_norm", "conv_b", "lru_L", "subln",
                 "lambda_q1", "lambda_k1", "lambda_q2", "lambda_k2"):
        w[name] = p[name][l][None, :]
    w["w_in"] = p["w_in"][l].astype(BF16)
    w["group_mat"] = jnp.asarray((group[:, None] == group[None, :]) / HEAD_DIM, BF16)
    w["q_gain"] = jnp.tile(p["q_norm"][l], a // HEAD_DIM)[None, :]
    w["k_gain"] = jnp.tile(p["k_norm"][l], a // HEAD_DIM)[None, :]
    w["w_out_attn"] = p["w_out"][l][:a].astype(BF16)
    w["w_out_lru"] = p["w_out"][l][a:].astype(BF16)
    w["conv_w"] = p["conv_w"][l]
    w["w_gate"] = jnp.concatenate([_block_diag(p["gate_a_w"][l]), _block_diag(p["gate_x_w"][l])],
                                  axis=1).astype(BF16)
    w["b_gate"] = jnp.concatenate([p["gate_a_b"][l], p["gate_x_b"][l]])[None, :]
    return w


def _scores_bounded(w, rel_bias):
    score_bound = (HEAD_DIM ** 0.5 * LOG2E * (1 + 2.0 ** -7)
                   * jnp.max(jnp.abs(w["q_gain"])) * jnp.max(jnp.abs(w["k_gain"])))
    bias_bound = 2 * LOG2E * jnp.max(jnp.abs(rel_bias))
    return score_bound + bias_bound <= SCORE_RANGE


def _layer_prompt(x, w, rel_bias, bounded):
    B, S, D = x.shape
    T = ATTN_TILE
    a, lw = w["attn_width"], w["lru_width"]
    H = a // V_DIM
    x1, kT, v, xr, xg, qT, kb, vT = _stage_a(x, w, transposed=True)
    k = kT.reshape(B, H, 2, HEAD_DIM, S).transpose(0, 4, 1, 2, 3)
    nl = S // LAYOUT_TILE
    attend = lambda fixed: functools.partial(_prompt_attention, w=w, fixed_shift=fixed)
    o = lax.cond(bounded, attend(True), attend(False),
                 qT.reshape(B, nl, a, LAYOUT_TILE), kb.reshape(B, S, a),
                 vT.reshape(B, nl, a, LAYOUT_TILE), rel_bias)
    r, h_last, conv_new = _lru(xr.reshape(B, S, lw), xg.reshape(B, S, lw),
                               jnp.zeros((B, lw), F32), jnp.zeros((B, CONV_WIDTH - 1, lw), F32), w)
    y = _stage_d(x1, o.reshape(B * S, a), r.reshape(B * S, lw), w)
    return (y.reshape(B, S, D), k, v.reshape(B, S, H, V_DIM), h_last.reshape(B, lw), conv_new)


def _layer_sample(x, w, rel_bias, bounded, k_past, v_past, h0, conv0):
    B, Q, D = x.shape
    P = k_past.shape[1]
    a, lw = w["attn_width"], w["lru_width"]
    H = a // V_DIM
    near = LANES
    assert P % CHUNK == 0 and Q <= CHUNK and near >= MAX_DISTANCE and P > near
    x1, k, v, xr, xg, qb, kb, vb = _stage_a(x, w, transposed=False)
    q_pos = P + jnp.arange(Q, dtype=jnp.int32)
    idx_c = _bucket_tile(q_pos, jnp.arange(P - near, P, dtype=jnp.int32), False)
    idx_n = _bucket_tile(q_pos, q_pos, False)
    bias_c = _bias_tiles(rel_bias, idx_c[None])[:, 0]
    bias_n = _bias_tiles(rel_bias, idx_n[None])[:, 0]
    attend = lambda free: functools.partial(_sample_attention, w=w, shift_free=free)
    o = lax.cond(bounded, attend(True), attend(False),
                 qb.reshape(B, Q, a), kb.reshape(B, Q, a), vb.reshape(B, Q, a),
                 k_past.transpose(0, 2, 3, 4, 1).reshape(B, a, P),
                 v_past.reshape(B, P * H, V_DIM), bias_c, bias_n)
    r, h_last, conv_new = _lru(xr.reshape(B, Q, lw), xg.reshape(B, Q, lw), h0, conv0, w)
    y = _stage_d(x1, o.reshape(B * Q, a), r.reshape(B * Q, lw), w)
    return (y.reshape(B, Q, D), k.reshape(B, Q, H, 2, LANES)[..., :HEAD_DIM], v.reshape(B, Q, H, V_DIM),
            h_last.reshape(B, lw), conv_new)


def kernel(x_prompt, x_sample, cache_k, cache_v, state_lru, state_conv, rel_bias, norm_ffn1, ffn1_gate, ffn1_up, ffn1_down, norm_mix, w_in, q_norm, k_norm, lambda_q1, lambda_k1, lambda_q2, lambda_k2, subln, conv_w, conv_b, gate_a_w, gate_a_b, gate_x_w, gate_x_b, lru_L, lru_out_norm, w_out, norm_ffn2, ffn2_gate, ffn2_up, ffn2_down):
    p = dict(norm_ffn1=norm_ffn1, ffn1_gate=ffn1_gate, ffn1_up=ffn1_up, ffn1_down=ffn1_down,
             norm_mix=norm_mix, w_in=w_in, q_norm=q_norm, k_norm=k_norm,
             lambda_q1=lambda_q1, lambda_k1=lambda_k1, lambda_q2=lambda_q2, lambda_k2=lambda_k2,
             subln=subln, conv_w=conv_w, conv_b=conv_b, gate_a_w=gate_a_w, gate_a_b=gate_a_b,
             gate_x_w=gate_x_w, gate_x_b=gate_x_b, lru_L=lru_L, lru_out_norm=lru_out_norm,
             w_out=w_out, norm_ffn2=norm_ffn2, ffn2_gate=ffn2_gate, ffn2_up=ffn2_up,
             ffn2_down=ffn2_down)
    depth = cache_k.shape[0]
    assert depth == 1, "the lambda initial value is specialised to a single layer"
    w = _prepare_weights(0, p)
    bounded = _scores_bounded(w, rel_bias)
    yp, kp, vp, hp, cp = _layer_prompt(x_prompt, w, rel_bias, bounded)
    ys, kn, vn, hn, cn = _layer_sample(x_sample, w, rel_bias, bounded, cache_k[0], cache_v[0],
                                       state_lru[0], state_conv[0])
    stack = lambda t: t[None]
    return (yp, ys, stack(kp), stack(vp), stack(hp), stack(cp),
            stack(kn), stack(vn), stack(hn), stack(cn))
```
